```python
import math
import jax, jax.numpy as jnp
from jax import lax
import numpy as np

D_MODEL = 1024
BATCH = 8
SEQ = 2048
DEPTH = 4

MEM_LEN = 256
CONV_WIDTH = D_MODEL
CONV_K = 3
DIFF_HEADS = 8
DIFF_DK = 64
DIFF_DV = 2 * DIFF_DK
DIFF_Q = DIFF_HEADS * 2 * DIFF_DK
DIFF_V = DIFF_HEADS * DIFF_DV
MEM_HEADS = 4
MEM_DH = 256
MEM_W = MEM_HEADS * MEM_DH
N_BRANCH = 3
D_FF = 4 * D_MODEL
REL_BUCKETS = 32
REL_MAX_DIST = 128
Q_BLOCK = 128
LN_EPS = 1e-5
DEEPNORM_ALPHA = (2 * DEPTH) ** 0.25
DEEPNORM_BETA = (8 * DEPTH) ** -0.25
IN_GROUPS = [CONV_WIDTH, CONV_WIDTH, CONV_WIDTH, DIFF_Q, DIFF_Q, DIFF_V, MEM_W, N_BRANCH * D_MODEL]
N_IN = sum(IN_GROUPS)
IN_SPLITS = [int(v) for v in np.cumsum(IN_GROUPS)[:-1]]

kernel_name = 'hybrid_conv_diffattn_memory_encoder'


def layer_norm(x, g, b):
    xf = x.astype(jnp.float32)
    mu = jnp.mean(xf, axis=-1, keepdims=True)
    var = jnp.mean(jnp.square(xf - mu), axis=-1, keepdims=True)
    return ((xf - mu) * lax.rsqrt(var + LN_EPS) * g + b).astype(x.dtype)


def rms_norm(x, g):
    xf = x.astype(jnp.float32)
    return (xf * lax.rsqrt(jnp.mean(xf * xf, axis=-1, keepdims=True) + LN_EPS) * g).astype(x.dtype)


def t5_bucket(rel):
    nb = REL_BUCKETS // 2
    max_exact = nb // 2
    ret = (rel > 0).astype(jnp.int32) * nb
    n = jnp.abs(rel)
    nf = jnp.maximum(n, 1).astype(jnp.float32)
    large = max_exact + (jnp.log(nf / max_exact) / math.log(REL_MAX_DIST / max_exact)
                         * (nb - max_exact)).astype(jnp.int32)
    large = jnp.minimum(large, nb - 1)
    return ret + jnp.where(n < max_exact, n, large)


def diff_attention(q1, q2, k1, k2, v, rel_bias, lam):
    B, S, H, DK = q1.shape
    nblk = S // Q_BLOCK
    scale = DIFF_DK ** -0.5
    qb1 = (q1 * scale).reshape(B, nblk, Q_BLOCK, H, DK).transpose(1, 0, 2, 3, 4)
    qb2 = (q2 * scale).reshape(B, nblk, Q_BLOCK, H, DK).transpose(1, 0, 2, 3, 4)
    kpos = jnp.arange(S, dtype=jnp.int32)

    def one_block(args):
        i, qa, qb = args
        qpos = i * Q_BLOCK + jnp.arange(Q_BLOCK, dtype=jnp.int32)
        bucket = t5_bucket(kpos[None, :] - qpos[:, None])
        bias = jnp.take(rel_bias, bucket, axis=0).transpose(2, 0, 1).astype(jnp.float32)
        s1 = jnp.einsum('bqhd,bkhd->bhqk', qa, k1).astype(jnp.float32) + bias[None]
        s2 = jnp.einsum('bqhd,bkhd->bhqk', qb, k2).astype(jnp.float32) + bias[None]
        p = jax.nn.softmax(s1, axis=-1) - lam * jax.nn.softmax(s2, axis=-1)
        return jnp.einsum('bhqk,bkhd->bqhd', p.astype(v.dtype), v)

    out = lax.map(one_block, (jnp.arange(nblk, dtype=jnp.int32), qb1, qb2))
    return out.transpose(1, 0, 2, 3, 4).reshape(B, S, H, v.shape[-1])


def token_mix(x, mem, w_in, b_gate, conv_w, w_conv_out, lam_p, subln_g, w_diff_out,
              rel_bias, w_mem_kv, w_mem_out, w_o, lam_init):
    B, S, D = x.shape
    proj = x @ w_in
    c_h, c_b, c_c, dq, dk, dv, mq, gl = jnp.split(proj, IN_SPLITS, axis=-1)

    y = lax.conv_general_dilated(c_c * c_h, conv_w[:, None, :].astype(x.dtype), (1,),
                                 [(CONV_K // 2, CONV_K // 2)],
                                 dimension_numbers=('NWC', 'WIO', 'NWC'),
                                 feature_group_count=CONV_WIDTH)
    out_a = (c_b * y) @ w_conv_out

    lf = lam_p.astype(jnp.float32)
    lam = (jnp.exp(jnp.sum(lf[0] * lf[1])) - jnp.exp(jnp.sum(lf[2] * lf[3])) + lam_init)
    q = dq.reshape(B, S, DIFF_HEADS, 2, DIFF_DK)
    k = dk.reshape(B, S, DIFF_HEADS, 2, DIFF_DK)
    v = dv.reshape(B, S, DIFF_HEADS, DIFF_DV)
    att = diff_attention(q[..., 0, :], q[..., 1, :], k[..., 0, :], k[..., 1, :], v, rel_bias, lam)
    att = rms_norm(att, subln_g) * (1.0 - lam_init)
    out_b = att.reshape(B, S, DIFF_V) @ w_diff_out

    mk, mv = jnp.split(mem @ w_mem_kv, 2, axis=-1)
    mk = mk.reshape(B, -1, MEM_HEADS, MEM_DH)
    mv = mv.reshape(B, -1, MEM_HEADS, MEM_DH)
    mqh = mq.reshape(B, S, MEM_HEADS, MEM_DH) * (MEM_DH ** -0.5)
    s = jnp.einsum('bqhd,bkhd->bhqk', mqh, mk).astype(jnp.float32)
    p = jax.nn.softmax(s, axis=-1)
    o = jnp.einsum('bhqk,bkhd->bqhd', p.astype(mv.dtype), mv).reshape(B, S, MEM_W)
    out_c = o @ w_mem_out

    g = jax.nn.sigmoid((gl + b_gate).astype(jnp.float32)).astype(x.dtype)
    ga, gb, gc = jnp.split(g, N_BRANCH, axis=-1)
    merged = ga * out_a + gb * out_b + gc * out_c
    return merged @ w_o


def setup_inputs(seed: int = 0) -> dict:
    key = jax.random.key(seed)
    ks = jax.random.split(key, 20)
    L, D = DEPTH, D_MODEL
    nrm = lambda k, shp: jax.random.normal(k, shp, dtype=jnp.float32)
    return {
        'x': nrm(ks[0], (BATCH, SEQ, D)),
        'mem': nrm(ks[1], (BATCH, MEM_LEN, D)),
        'w_in': nrm(ks[2], (L, D, N_IN)) * D ** -0.5,
        'b_gate': 0.1 * nrm(ks[3], (L, N_BRANCH * D)),
        'conv_w': nrm(ks[4], (L, CONV_K, CONV_WIDTH)) * CONV_K ** -0.5,
        'w_conv_out': nrm(ks[5], (L, CONV_WIDTH, D)) * CONV_WIDTH ** -0.5,
        'diff_lambda': 0.1 * nrm(ks[6], (L, 4, DIFF_DK)),
        'subln_g': 1.0 + 0.1 * nrm(ks[7], (L, DIFF_DV)),
        'w_diff_out': nrm(ks[8], (L, DIFF_V, D)) * DIFF_V ** -0.5,
        'rel_bias': 0.5 * nrm(ks[9], (REL_BUCKETS, DIFF_HEADS)),
        'w_mem_kv': nrm(ks[10], (L, D, 2 * MEM_W)) * D ** -0.5,
        'w_mem_out': nrm(ks[11], (L, MEM_W, D)) * MEM_W ** -0.5,
        'w_o': nrm(ks[12], (L, D, D)) * (D ** -0.5 * DEEPNORM_BETA),
        'ln1_g': 1.0 + 0.1 * nrm(ks[13], (L, D)),
        'ln1_b': 0.1 * nrm(ks[14], (L, D)),
        'w_mlp1': nrm(ks[15], (L, D, D_FF)) * D ** -0.5,
        'w_mlp2': nrm(ks[16], (L, D_FF, D)) * (D_FF ** -0.5 * DEEPNORM_BETA),
        'ln2_g': 1.0 + 0.1 * nrm(ks[17], (L, D)),
        'ln2_b': 0.1 * nrm(ks[18], (L, D)),
    }


def reference(x, mem, w_in, b_gate, conv_w, w_conv_out, diff_lambda, subln_g, w_diff_out,
              rel_bias, w_mem_kv, w_mem_out, w_o, ln1_g, ln1_b, w_mlp1, w_mlp2, ln2_g, ln2_b):
    for l in range(DEPTH):
        lam_init = 0.8 - 0.6 * math.exp(-0.3 * l)
        h = token_mix(x, mem, w_in[l], b_gate[l], conv_w[l], w_conv_out[l], diff_lambda[l],
                      subln_g[l], w_diff_out[l], rel_bias, w_mem_kv[l], w_mem_out[l], w_o[l],
                      lam_init)
        x = layer_norm(DEEPNORM_ALPHA * x + h, ln1_g[l], ln1_b[l])
        f = jnp.square(jax.nn.relu(x @ w_mlp1[l])) @ w_mlp2[l]
        x = layer_norm(DEEPNORM_ALPHA * x + f, ln2_g[l], ln2_b[l])
    return x
```

```python
import functools
import math

import jax
import jax.numpy as jnp
from jax import lax
from jax.experimental import pallas as pl
from jax.experimental.pallas import tpu as pltpu

F32 = jnp.float32
BF16 = jnp.bfloat16

D_MODEL = 1024
CONV_K = 3
DIFF_HEADS = 8
DIFF_DK = 64
DIFF_DV = 2 * DIFF_DK
MEM_HEADS = 4
MEM_DH = 256
N_BRANCH = 3
REL_BUCKETS = 32
REL_MAX_DIST = 128
LN_EPS = 1e-5

COL_CH, COL_CB, COL_CC = 0, D_MODEL, 2 * D_MODEL
COL_QKV = 3 * D_MODEL
COL_MQ = 6 * D_MODEL
COL_GATE = 7 * D_MODEL

VMEM_LIMIT = 56 * 1024 * 1024

NT_DIMS = (((1,), (1,)), ((), ()))
TN_DIMS = (((0,), (0,)), ((), ()))


def _params(*sem):
    return pltpu.CompilerParams(dimension_semantics=sem, vmem_limit_bytes=VMEM_LIMIT)


def _layer_norm(y, g, b):
    mu = jnp.mean(y, axis=-1, keepdims=True)
    yc = y - mu
    var = jnp.mean(yc * yc, axis=-1, keepdims=True)
    return yc * lax.rsqrt(var + LN_EPS) * g + b


def _mm_kernel(x_ref, w_ref, o_ref):
    o_ref[...] = jnp.dot(x_ref[...], w_ref[...], preferred_element_type=F32).astype(o_ref.dtype)


def _matmul(x, w_stack, layer, col0, ncols, tm, tn, out_dtype):
    m, k = x.shape
    cb0 = col0 // tn
    return pl.pallas_call(
        _mm_kernel,
        grid=(m // tm, ncols // tn),
        in_specs=[
            pl.BlockSpec((tm, k), lambda i, j: (i, 0)),
            pl.BlockSpec((None, k, tn), lambda i, j: (layer, 0, cb0 + j)),
        ],
        out_specs=pl.BlockSpec((tm, tn), lambda i, j: (i, j)),
        out_shape=jax.ShapeDtypeStruct((m, ncols), out_dtype),
        compiler_params=_params("parallel", "arbitrary"),
        name="matmul",
    )(x, w_stack)


def _conv_kernel(x_ref, wh_ref, wb_ref, wc_ref, cw_ref, z_ref, u_ref):
    s = x_ref.shape[0]
    x = x_ref[...]
    pad = jnp.zeros((8, u_ref.shape[1]), F32)
    u_ref[0:8, :] = pad
    u_ref[s + 8:s + 16, :] = pad
    u_ref[8:s + 8, :] = (jnp.dot(x, wc_ref[...], preferred_element_type=F32)
                         * jnp.dot(x, wh_ref[...], preferred_element_type=F32))
    cb = jnp.dot(x, wb_ref[...], preferred_element_type=F32)
    cw = cw_ref[...]
    y = (cw[0:1, :] * u_ref[7:s + 7, :] + cw[1:2, :] * u_ref[8:s + 8, :]
         + cw[2:3, :] * u_ref[9:s + 9, :])
    z_ref[...] = (cb * y).astype(z_ref.dtype)


def _conv_branch(xb, w_in, conv_w, layer, batch, seq, tc=256):
    t, d = xb.shape
    nct = D_MODEL // tc
    return pl.pallas_call(
        _conv_kernel,
        grid=(batch, nct),
        in_specs=[
            pl.BlockSpec((seq, d), lambda b, j: (b, 0)),
            pl.BlockSpec((None, d, tc), lambda b, j: (layer, 0, COL_CH // tc + j)),
            pl.BlockSpec((None, d, tc), lambda b, j: (layer, 0, COL_CB // tc + j)),
            pl.BlockSpec((None, d, tc), lambda b, j: (layer, 0, COL_CC // tc + j)),
            pl.BlockSpec((None, CONV_K, tc), lambda b, j: (layer, 0, j)),
        ],
        out_specs=pl.BlockSpec((seq, tc), lambda b, j: (b, j)),
        out_shape=jax.ShapeDtypeStruct((t, D_MODEL), BF16),
        scratch_shapes=[pltpu.VMEM((seq + 16, tc), F32)],
        compiler_params=_params("parallel", "arbitrary"),
        name="conv_branch",
    )(xb, w_in, w_in, w_in, conv_w)


def _bias_kernel(rel_ref, bkt_ref, o_ref):
    h = pl.program_id(0)
    bkt = bkt_ref[...]
    acc = jnp.zeros(bkt.shape, F32)
    for j in range(REL_BUCKETS):
        acc = jnp.where(bkt == j, rel_ref[j, h], acc)
    o_ref[...] = acc


def _rel_bias_kt(rel_bias, bucket_kt, tq=256, tk=512):
    s = bucket_kt.shape[0]
    return pl.pallas_call(
        _bias_kernel,
        grid=(DIFF_HEADS, s // tk, s // tq),
        in_specs=[
            pl.BlockSpec(memory_space=pltpu.SMEM),
            pl.BlockSpec((tk, tq), lambda h, i, j: (i, j)),
        ],
        out_specs=pl.BlockSpec((None, tk, tq), lambda h, i, j: (h, i, j)),
        out_shape=jax.ShapeDtypeStruct((DIFF_HEADS, s, s), F32),
        compiler_params=_params("parallel", "parallel", "arbitrary"),
        name="rel_bias",
    )(rel_bias, bucket_kt)


def _t5_bucket(rel):
    nb = REL_BUCKETS // 2
    max_exact = nb // 2
    ret = (rel > 0).astype(jnp.int32) * nb
    n = jnp.abs(rel)
    nf = jnp.maximum(n, 1).astype(F32)
    large = max_exact + (jnp.log(nf / max_exact) / math.log(REL_MAX_DIST / max_exact)
                         * (nb - max_exact)).astype(jnp.int32)
    large = jnp.minimum(large, nb - 1)
    return ret + jnp.where(n < max_exact, n, large)


def _diff_attn_kernel(lam_ref, q_ref, k_ref, v_ref, bias_ref, g_ref, o_ref,
                      s1_ref, s2_ref, e1_ref, e2_ref, *, lam_init, kc):
    tq = q_ref.shape[0]
    s = k_ref.shape[0]
    q = q_ref[...] * jnp.asarray(DIFF_DK ** -0.5, q_ref.dtype)
    lane = lax.broadcasted_iota(jnp.int32, q.shape, 1)
    zero = jnp.zeros_like(q)
    q1 = jnp.where(lane < DIFF_DK, q, zero)
    q2 = jnp.where(lane >= DIFF_DK, q, zero)

    m1 = jnp.full((8, tq), -jnp.inf, F32)
    m2 = jnp.full((8, tq), -jnp.inf, F32)
    for c in range(s // kc):
        sl = slice(c * kc, (c + 1) * kc)
        kch = k_ref[sl, :]
        b = bias_ref[sl, :]
        s1 = lax.dot_general(kch, q1, NT_DIMS, preferred_element_type=F32) + b
        s2 = lax.dot_general(kch, q2, NT_DIMS, preferred_element_type=F32) + b
        s1_ref[sl, :] = s1
        s2_ref[sl, :] = s2
        m1 = jnp.maximum(m1, jnp.max(s1.reshape(kc // 8, 8, tq), axis=0))
        m2 = jnp.maximum(m2, jnp.max(s2.reshape(kc // 8, 8, tq), axis=0))
    m1 = jnp.max(m1, axis=0, keepdims=True)
    m2 = jnp.max(m2, axis=0, keepdims=True)

    l1 = jnp.zeros((8, tq), F32)
    l2 = jnp.zeros((8, tq), F32)
    for c in range(s // kc):
        sl = slice(c * kc, (c + 1) * kc)
        e1 = jnp.exp(s1_ref[sl, :] - m1)
        e2 = jnp.exp(s2_ref[sl, :] - m2)
        l1 = l1 + jnp.sum(e1.reshape(kc // 8, 8, tq), axis=0)
        l2 = l2 + jnp.sum(e2.reshape(kc // 8, 8, tq), axis=0)
        e1_ref[sl, :] = e1.astype(e1_ref.dtype)
        e2_ref[sl, :] = e2.astype(e2_ref.dtype)
    l1 = jnp.sum(l1, axis=0, keepdims=True)
    l2 = jnp.sum(l2, axis=0, keepdims=True)

    v = v_ref[...]
    a1 = lax.dot_general(v, e1_ref[...], TN_DIMS, preferred_element_type=F32)
    a2 = lax.dot_general(v, e2_ref[...], TN_DIMS, preferred_element_type=F32)

    lf = lam_ref[...]
    lam = (jnp.exp(jnp.sum(lf[0:1, :] * lf[1:2, :], axis=-1, keepdims=True))
           - jnp.exp(jnp.sum(lf[2:3, :] * lf[3:4, :], axis=-1, keepdims=True)) + lam_init)
    ot = a1 * (1.0 / l1) - a2 * (lam / l2)
    o = ot.T
    ms = jnp.mean(o * o, axis=-1, keepdims=True)
    o_ref[...] = (o * lax.rsqrt(ms + LN_EPS) * g_ref[...] * (1.0 - lam_init)).astype(o_ref.dtype)


def _diff_attention(qkv, bias_kt, diff_lambda, subln_g, layer, batch, seq, lam_init, tq=256, kc=256):
    t = qkv.shape[0]
    nqt = seq // tq
    kern = functools.partial(_diff_attn_kernel, lam_init=lam_init, kc=kc)
    return pl.pallas_call(
        kern,
        grid=(DIFF_HEADS, nqt, batch),
        in_specs=[
            pl.BlockSpec((None, 4, DIFF_DK), lambda h, i, b: (layer, 0, 0)),
            pl.BlockSpec((tq, DIFF_DV), lambda h, i, b: (b * nqt + i, h)),
            pl.BlockSpec((seq, DIFF_DV), lambda h, i, b: (b, DIFF_HEADS + h)),
            pl.BlockSpec((seq, DIFF_DV), lambda h, i, b: (b, 2 * DIFF_HEADS + h)),
            pl.BlockSpec((None, seq, tq), lambda h, i, b: (h, 0, i)),
            pl.BlockSpec((None, 1, DIFF_DV), lambda h, i, b: (layer, 0, 0)),
        ],
        out_specs=pl.BlockSpec((tq, DIFF_DV), lambda h, i, b: (b * nqt + i, h)),
        out_shape=jax.ShapeDtypeStruct((t, DIFF_HEADS * DIFF_DV), BF16),
        scratch_shapes=[
            pltpu.VMEM((seq, tq), F32),
            pltpu.VMEM((seq, tq), F32),
            pltpu.VMEM((seq, tq), BF16),
            pltpu.VMEM((seq, tq), BF16),
        ],
        compiler_params=_params("parallel", "parallel", "arbitrary"),
        name="diff_attn",
    )(diff_lambda, qkv, qkv, qkv, bias_kt, subln_g)


def _merge_kernel(x_ref, xb_ref, z_ref, att_ref, mkv_ref,
                  wmq_ref, wga_ref, wgb_ref, wgc_ref, bg_ref,
                  wca_ref, wdo_ref, wmo_ref, wo_ref, g_ref, b_ref,
                  y_ref, yb_ref, o_scr, *, alpha):
    xb = xb_ref[...]
    mq = (jnp.dot(xb, wmq_ref[...], preferred_element_type=F32) * (MEM_DH ** -0.5)).astype(BF16)
    w = MEM_HEADS * MEM_DH
    for h in range(MEM_HEADS):
        hs = slice(h * MEM_DH, (h + 1) * MEM_DH)
        sc = lax.dot_general(mq[:, hs], mkv_ref[:, hs], NT_DIMS, preferred_element_type=F32)
        e = jnp.exp(sc - jnp.max(sc, axis=-1, keepdims=True))
        p = e * (1.0 / jnp.sum(e, axis=-1, keepdims=True))
        o_scr[:, hs] = jnp.dot(p.astype(BF16), mkv_ref[:, w + h * MEM_DH:w + (h + 1) * MEM_DH],
                               preferred_element_type=F32).astype(BF16)
    out_c = jnp.dot(o_scr[...], wmo_ref[...], preferred_element_type=F32)
    out_a = jnp.dot(z_ref[...], wca_ref[...], preferred_element_type=F32)
    out_b = jnp.dot(att_ref[...], wdo_ref[...], preferred_element_type=F32)
    d = D_MODEL

    def gate(wg_ref, j):
        gl = jnp.dot(xb, wg_ref[...], preferred_element_type=F32) + bg_ref[:, j * d:(j + 1) * d]
        return 1.0 / (1.0 + jnp.exp(-gl))

    merged = gate(wga_ref, 0) * out_a + gate(wgb_ref, 1) * out_b + gate(wgc_ref, 2) * out_c
    hmix = jnp.dot(merged.astype(BF16), wo_ref[...], preferred_element_type=F32)
    y = _layer_norm(alpha * x_ref[...] + hmix, g_ref[...], b_ref[...])
    y_ref[...] = y
    yb_ref[...] = y.astype(BF16)


def _merge(x, xb, z, att, mkv, w_in, b_gate, w_conv_out, w_diff_out, w_mem_out, w_o, ln_g, ln_b,
           layer, seq, mem_len, alpha, tm=256):
    t, d = x.shape
    per_b = seq // tm
    row = lambda i: (i, 0)
    wspec = lambda: pl.BlockSpec((None, d, d), lambda i: (layer, 0, 0), pipeline_mode=pl.Buffered(1))
    win = lambda col: pl.BlockSpec((None, d, d), lambda i: (layer, 0, col // d), pipeline_mode=pl.Buffered(1))
    vec = lambda n: pl.BlockSpec((None, 1, n), lambda i: (layer, 0, 0))
    kern = functools.partial(_merge_kernel, alpha=alpha)
    return pl.pallas_call(
        kern,
        grid=(t // tm,),
        in_specs=[
            pl.BlockSpec((tm, d), row),
            pl.BlockSpec((tm, d), row),
            pl.BlockSpec((tm, d), row),
            pl.BlockSpec((tm, d), row),
            pl.BlockSpec((mem_len, 2 * d), lambda i: (i // per_b, 0)),
            win(COL_MQ), win(COL_GATE), win(COL_GATE + d), win(COL_GATE + 2 * d),
            vec(N_BRANCH * d),
            wspec(), wspec(), wspec(), wspec(),
            vec(d), vec(d),
        ],
        out_specs=[pl.BlockSpec((tm, d), row), pl.BlockSpec((tm, d), row)],
        out_shape=[jax.ShapeDtypeStruct((t, d), F32), jax.ShapeDtypeStruct((t, d), BF16)],
        scratch_shapes=[pltpu.VMEM((tm, d), BF16)],
        compiler_params=_params("parallel"),
        name="merge",
    )(x, xb, z, att, mkv, w_in, w_in, w_in, w_in, b_gate,
      w_conv_out, w_diff_out, w_mem_out, w_o, ln_g, ln_b)


def _mlp_kernel(x_ref, xb_ref, w1_ref, w2_ref, g_ref, b_ref, y_ref, yb_ref, *, alpha, fc):
    xb = xb_ref[...]
    f = jnp.zeros(x_ref.shape, F32)
    for c in range(w1_ref.shape[1] // fc):
        cs = slice(c * fc, (c + 1) * fc)
        hid = jnp.maximum(jnp.dot(xb, w1_ref[:, cs], preferred_element_type=F32), 0.0)
        f = f + jnp.dot((hid * hid).astype(BF16), w2_ref[cs, :], preferred_element_type=F32)
    y = _layer_norm(alpha * x_ref[...] + f, g_ref[...], b_ref[...])
    y_ref[...] = y
    yb_ref[...] = y.astype(BF16)


def _mlp(x, xb, w1, w2, ln_g, ln_b, layer, alpha, tm=512, fc=1024):
    t, d = x.shape
    dff = w1.shape[2]
    row = lambda i: (i, 0)
    vec = lambda n: pl.BlockSpec((None, 1, n), lambda i: (layer, 0, 0))
    kern = functools.partial(_mlp_kernel, alpha=alpha, fc=fc)
    return pl.pallas_call(
        kern,
        grid=(t // tm,),
        in_specs=[
            pl.BlockSpec((tm, d), row),
            pl.BlockSpec((tm, d), row),
            pl.BlockSpec((None, d, dff), lambda i: (layer, 0, 0), pipeline_mode=pl.Buffered(1)),
            pl.BlockSpec((None, dff, d), lambda i: (layer, 0, 0), pipeline_mode=pl.Buffered(1)),
            vec(d), vec(d),
        ],
        out_specs=[pl.BlockSpec((tm, d), row), pl.BlockSpec((tm, d), row)],
        out_shape=[jax.ShapeDtypeStruct((t, d), F32), jax.ShapeDtypeStruct((t, d), BF16)],
        compiler_params=_params("parallel"),
        name="mlp",
    )(x, xb, w1, w2, ln_g, ln_b)


def kernel(x, mem, w_in, b_gate, conv_w, w_conv_out, diff_lambda, subln_g, w_diff_out, rel_bias,
           w_mem_kv, w_mem_out, w_o, ln1_g, ln1_b, w_mlp1, w_mlp2, ln2_g, ln2_b):
    batch, seq, d = x.shape
    mem_len = mem.shape[1]
    depth = w_in.shape[0]
    alpha = (2 * depth) ** 0.25

    w_in_b = w_in.astype(BF16)
    w_conv_out_b = w_conv_out.astype(BF16)
    w_diff_out_b = w_diff_out.astype(BF16)
    w_mem_kv_b = w_mem_kv.astype(BF16)
    w_mem_out_b = w_mem_out.astype(BF16)
    w_o_b = w_o.astype(BF16)
    w_mlp1_b = w_mlp1.astype(BF16)
    w_mlp2_b = w_mlp2.astype(BF16)
    mem_b = mem.reshape(batch * mem_len, d).astype(BF16)
    vec3 = lambda a: a.reshape(depth, 1, a.shape[-1])

    pos = jnp.arange(seq, dtype=jnp.int32)
    bucket_kt = _t5_bucket(pos[:, None] - pos[None, :])
    bias_kt = _rel_bias_kt(rel_bias, bucket_kt)

    xf = x.reshape(batch * seq, d)
    xb = xf.astype(BF16)
    for l in range(depth):
        lam_init = 0.8 - 0.6 * math.exp(-0.3 * l)
        qkv = _matmul(xb, w_in_b, l, COL_QKV, 3 * d, 1024, 1024, BF16)
        att = _diff_attention(qkv, bias_kt, diff_lambda, vec3(subln_g), l, batch, seq, lam_init)
        z = _conv_branch(xb, w_in_b, conv_w, l, batch, seq)
        mkv = _matmul(mem_b, w_mem_kv_b, l, 0, 2 * d, 1024, 1024, BF16)
        xf, xb = _merge(xf, xb, z, att, mkv, w_in_b, vec3(b_gate), w_conv_out_b, w_diff_out_b,
                        w_mem_out_b, w_o_b, vec3(ln1_g), vec3(ln1_b), l, seq, mem_len, alpha)
        xf, xb = _mlp(xf, xb, w_mlp1_b, w_mlp2_b, vec3(ln2_g), vec3(ln2_b), l, alpha)
    return xf.reshape(batch, seq, d)
```

```python
import functools
import math

import jax
import jax.numpy as jnp
from jax import lax
from jax.experimental import pallas as pl
from jax.experimental.pallas import tpu as pltpu

F32 = jnp.float32
BF16 = jnp.bfloat16

D_MODEL = 1024
CONV_K = 3
DIFF_HEADS = 8
DIFF_DK = 64
DIFF_DV = 2 * DIFF_DK
MEM_HEADS = 4
MEM_DH = 256
N_BRANCH = 3
REL_BUCKETS = 32
REL_MAX_DIST = 128
LN_EPS = 1e-5
LOG2E = math.log2(math.e)
SUM_ROWS = 16

COL_CH, COL_CB, COL_CC = 0, D_MODEL, 2 * D_MODEL
COL_Q, COL_K, COL_V = 3 * D_MODEL, 4 * D_MODEL, 5 * D_MODEL
COL_MQ = 6 * D_MODEL
COL_GATE = 7 * D_MODEL

VMEM_LIMIT = 56 * 1024 * 1024

NT_DIMS = (((1,), (1,)), ((), ()))


def _params(*sem):
    return pltpu.CompilerParams(dimension_semantics=sem, vmem_limit_bytes=VMEM_LIMIT)


def _layer_norm(y, g, b):
    mu = jnp.mean(y, axis=-1, keepdims=True)
    yc = y - mu
    var = jnp.mean(yc * yc, axis=-1, keepdims=True)
    return yc * lax.rsqrt(var + LN_EPS) * g + b


def _mm_kernel(x_ref, w_ref, o_ref):
    o_ref[...] = jnp.dot(x_ref[...], w_ref[...], preferred_element_type=F32).astype(o_ref.dtype)


def _matmul(x, w_stack, layer, col0, ncols, tm, tn, out_dtype):
    m, k = x.shape
    cb0 = col0 // tn
    return pl.pallas_call(
        _mm_kernel,
        grid=(m // tm, ncols // tn),
        in_specs=[
            pl.BlockSpec((tm, k), lambda i, j: (i, 0)),
            pl.BlockSpec((None, k, tn), lambda i, j: (layer, 0, cb0 + j)),
        ],
        out_specs=pl.BlockSpec((tm, tn), lambda i, j: (i, j)),
        out_shape=jax.ShapeDtypeStruct((m, ncols), out_dtype),
        compiler_params=_params("parallel", "arbitrary"),
        name="matmul",
    )(x, w_stack)


def _mm_t_kernel(wt_ref, x_ref, o_ref, *, scale):
    acc = lax.dot_general(wt_ref[...], x_ref[...], NT_DIMS, preferred_element_type=F32)
    o_ref[...] = (acc * scale).astype(o_ref.dtype)


def _matmul_t(wt_stack, x, layer, tn, tm, scale, out_dtype):
    m, k = x.shape
    n = wt_stack.shape[1]
    return pl.pallas_call(
        functools.partial(_mm_t_kernel, scale=scale),
        grid=(m // tm, n // tn),
        in_specs=[
            pl.BlockSpec((None, tn, k), lambda i, j: (layer, j, 0)),
            pl.BlockSpec((tm, k), lambda i, j: (i, 0)),
        ],
        out_specs=pl.BlockSpec((tn, tm), lambda i, j: (j, i)),
        out_shape=jax.ShapeDtypeStruct((n, m), out_dtype),
        compiler_params=_params("parallel", "arbitrary"),
        name="matmul_t",
    )(wt_stack, x)


def _conv_kernel(x_ref, wh_ref, wb_ref, wc_ref, cw_ref, z_ref, u_ref):
    s = x_ref.shape[0]
    x = x_ref[...]
    pad = jnp.zeros((8, u_ref.shape[1]), F32)
    u_ref[0:8, :] = pad
    u_ref[s + 8:s + 16, :] = pad
    u_ref[8:s + 8, :] = (jnp.dot(x, wc_ref[...], preferred_element_type=F32)
                         * jnp.dot(x, wh_ref[...], preferred_element_type=F32))
    cb = jnp.dot(x, wb_ref[...], preferred_element_type=F32)
    cw = cw_ref[...]
    y = (cw[0:1, :] * u_ref[7:s + 7, :] + cw[1:2, :] * u_ref[8:s + 8, :]
         + cw[2:3, :] * u_ref[9:s + 9, :])
    z_ref[...] = (cb * y).astype(z_ref.dtype)


def _conv_branch(xb, w_in, conv_w, layer, batch, seq, tc=256):
    t, d = xb.shape
    nct = D_MODEL // tc
    return pl.pallas_call(
        _conv_kernel,
        grid=(batch, nct),
        in_specs=[
            pl.BlockSpec((seq, d), lambda b, j: (b, 0)),
            pl.BlockSpec((None, d, tc), lambda b, j: (layer, 0, COL_CH // tc + j)),
            pl.BlockSpec((None, d, tc), lambda b, j: (layer, 0, COL_CB // tc + j)),
            pl.BlockSpec((None, d, tc), lambda b, j: (layer, 0, COL_CC // tc + j)),
            pl.BlockSpec((None, CONV_K, tc), lambda b, j: (layer, 0, j)),
        ],
        out_specs=pl.BlockSpec((seq, tc), lambda b, j: (b, j)),
        out_shape=jax.ShapeDtypeStruct((t, D_MODEL), BF16),
        scratch_shapes=[pltpu.VMEM((seq + 16, tc), F32)],
        compiler_params=_params("parallel", "arbitrary"),
        name="conv_branch",
    )(xb, w_in, w_in, w_in, conv_w)


def _bias_kernel(rel_ref, bkt_ref, o_ref):
    h = pl.program_id(0)
    bkt = bkt_ref[...]
    acc = jnp.zeros(bkt.shape, F32)
    for j in range(REL_BUCKETS):
        acc = jnp.where(bkt == j, rel_ref[j, h], acc)
    o_ref[...] = acc * LOG2E


def _rel_bias_kt(rel_bias, bucket_kt, tq=256, tk=512):
    s = bucket_kt.shape[0]
    return pl.pallas_call(
        _bias_kernel,
        grid=(DIFF_HEADS, s // tk, s // tq),
        in_specs=[
            pl.BlockSpec(memory_space=pltpu.SMEM),
            pl.BlockSpec((tk, tq), lambda h, i, j: (i, j)),
        ],
        out_specs=pl.BlockSpec((None, tk, tq), lambda h, i, j: (h, i, j)),
        out_shape=jax.ShapeDtypeStruct((DIFF_HEADS, s, s), F32),
        compiler_params=_params("parallel", "parallel", "arbitrary"),
        name="rel_bias",
    )(rel_bias, bucket_kt)


def _t5_bucket(rel):
    nb = REL_BUCKETS // 2
    max_exact = nb // 2
    ret = (rel > 0).astype(jnp.int32) * nb
    n = jnp.abs(rel)
    nf = jnp.maximum(n, 1).astype(F32)
    large = max_exact + (jnp.log(nf / max_exact) / math.log(REL_MAX_DIST / max_exact)
                         * (nb - max_exact)).astype(jnp.int32)
    large = jnp.minimum(large, nb - 1)
    return ret + jnp.where(n < max_exact, n, large)


def _diff_attn_kernel(lam_ref, qt_ref, k_ref, bias_ref, vt_ref, g_ref, o_ref,
                      s0_ref, s1_ref, m0_ref, m1_ref, e_ref, a0_ref, a1_ref, *, lam_init, kc):
    tq = qt_ref.shape[1]
    s = k_ref.shape[0]
    n = 2 * tq
    t = pl.program_id(0)

    @pl.when(t == 0)
    def _():
        s1_ref[...] = jnp.zeros(s1_ref.shape, F32)
        m1_ref[...] = jnp.zeros(m1_ref.shape, F32)
        a1_ref[...] = jnp.ones(a1_ref.shape, F32)

    def step(sw_ref, mw_ref, sr_ref, mr_ref, aw_ref, ar_ref):
        qt = qt_ref[...]
        row = lax.broadcasted_iota(jnp.int32, qt.shape, 0)
        zero = jnp.zeros_like(qt)
        q12 = jnp.concatenate([jnp.where(row < DIFF_DK, qt, zero), jnp.where(row >= DIFF_DK, qt, zero)], axis=1)
        b = bias_ref[...]
        sc = jnp.dot(k_ref[...], q12, preferred_element_type=F32) + jnp.concatenate([b, b], axis=1)
        sw_ref[...] = sc
        mw_ref[...] = jnp.max(sc.reshape(s // 8, 8, n), axis=0)

        m = jnp.max(mr_ref[...], axis=0, keepdims=True)
        for c in range(s // kc):
            sl = slice(c * kc, (c + 1) * kc)
            e_ref[sl, :] = jnp.exp2(sr_ref[sl, :] - m).astype(BF16)
        orow = lax.broadcasted_iota(jnp.int32, (SUM_ROWS, s), 0)
        ones = jnp.where(orow == 0, 1.0, 0.0).astype(BF16)
        vt = jnp.concatenate([vt_ref[...], ones], axis=0)
        aw_ref[...] = jnp.dot(vt, e_ref[...], preferred_element_type=F32)

        acc = ar_ref[...]
        lf = lam_ref[...]
        lam = (jnp.exp(jnp.sum(lf[0:1, :] * lf[1:2, :], axis=-1, keepdims=True))
               - jnp.exp(jnp.sum(lf[2:3, :] * lf[3:4, :], axis=-1, keepdims=True)) + lam_init)
        r = 1.0 / acc[DIFF_DV:DIFF_DV + 1, :]
        ot = acc[:DIFF_DV, :tq] * r[:, :tq] - acc[:DIFF_DV, tq:] * (lam * r[:, tq:])
        o = ot.T
        ms = jnp.mean(o * o, axis=-1, keepdims=True)
        o_ref[...] = (o * lax.rsqrt(ms + LN_EPS) * g_ref[...] * (1.0 - lam_init)).astype(o_ref.dtype)

    @pl.when(t % 2 == 0)
    def _():
        step(s0_ref, m0_ref, s1_ref, m1_ref, a0_ref, a1_ref)

    @pl.when(t % 2 == 1)
    def _():
        step(s1_ref, m1_ref, s0_ref, m0_ref, a1_ref, a0_ref)


def _diff_attention(qt, k, vt, bias_kt, diff_lambda, subln_g, layer, batch, seq, lam_init, tq=256, kc=256):
    tokens = k.shape[0]
    nqt = seq // tq
    ntile = DIFF_HEADS * nqt * batch

    def dec(tile):
        return tile // (batch * nqt), (tile // batch) % nqt, tile % batch

    cur = lambda t: dec(jnp.minimum(t, ntile - 1))
    prev = lambda t: dec(jnp.clip(t - 1, 0, ntile - 1))
    prev2 = lambda t: dec(jnp.maximum(t - 2, 0))

    def qt_map(t):
        h, i, b = cur(t)
        return (h, b * nqt + i)

    def k_map(t):
        h, i, b = cur(t)
        return (b, h)

    def bias_map(t):
        h, i, b = cur(t)
        return (h, 0, i)

    def vt_map(t):
        h, i, b = prev(t)
        return (h, b)

    def o_map(t):
        h, i, b = prev2(t)
        return (b * nqt + i, h)

    kern = functools.partial(_diff_attn_kernel, lam_init=lam_init, kc=kc)
    return pl.pallas_call(
        kern,
        grid=(ntile + 2,),
        in_specs=[
            pl.BlockSpec((None, 4, DIFF_DK), lambda t: (layer, 0, 0)),
            pl.BlockSpec((DIFF_DV, tq), qt_map),
            pl.BlockSpec((seq, DIFF_DV), k_map),
            pl.BlockSpec((None, seq, tq), bias_map),
            pl.BlockSpec((DIFF_DV, seq), vt_map),
            pl.BlockSpec((None, 1, DIFF_DV), lambda t: (layer, 0, 0)),
        ],
        out_specs=pl.BlockSpec((tq, DIFF_DV), o_map),
        out_shape=jax.ShapeDtypeStruct((tokens, DIFF_HEADS * DIFF_DV), BF16),
        scratch_shapes=[
            pltpu.VMEM((seq, 2 * tq), F32), pltpu.VMEM((seq, 2 * tq), F32),
            pltpu.VMEM((8, 2 * tq), F32), pltpu.VMEM((8, 2 * tq), F32),
            pltpu.VMEM((seq, 2 * tq), BF16),
            pltpu.VMEM((DIFF_DV + SUM_ROWS, 2 * tq), F32), pltpu.VMEM((DIFF_DV + SUM_ROWS, 2 * tq), F32),
        ],
        compiler_params=_params("arbitrary"),
        name="diff_attn",
    )(diff_lambda, qt, k, bias_kt, vt, subln_g)


def _merge_kernel(x_ref, xb_ref, z_ref, att_ref, mkv_ref,
                  wmq_ref, wga_ref, wgb_ref, wgc_ref, bg_ref,
                  wca_ref, wdo_ref, wmo_ref, wo_ref, g_ref, b_ref,
                  y_ref, yb_ref, o_scr, *, alpha):
    xb = xb_ref[...]
    mq = (jnp.dot(xb, wmq_ref[...], preferred_element_type=F32) * (MEM_DH ** -0.5)).astype(BF16)
    w = MEM_HEADS * MEM_DH
    for h in range(MEM_HEADS):
        hs = slice(h * MEM_DH, (h + 1) * MEM_DH)
        sc = lax.dot_general(mq[:, hs], mkv_ref[:, hs], NT_DIMS, preferred_element_type=F32)
        e = jnp.exp(sc - jnp.max(sc, axis=-1, keepdims=True))
        p = e * (1.0 / jnp.sum(e, axis=-1, keepdims=True))
        o_scr[:, hs] = jnp.dot(p.astype(BF16), mkv_ref[:, w + h * MEM_DH:w + (h + 1) * MEM_DH],
                               preferred_element_type=F32).astype(BF16)
    out_c = jnp.dot(o_scr[...], wmo_ref[...], preferred_element_type=F32)
    out_a = jnp.dot(z_ref[...], wca_ref[...], preferred_element_type=F32)
    out_b = jnp.dot(att_ref[...], wdo_ref[...], preferred_element_type=F32)
    d = D_MODEL

    def gate(wg_ref, j):
        gl = jnp.dot(xb, wg_ref[...], preferred_element_type=F32) + bg_ref[:, j * d:(j + 1) * d]
        return 1.0 / (1.0 + jnp.exp(-gl))

    merged = gate(wga_ref, 0) * out_a + gate(wgb_ref, 1) * out_b + gate(wgc_ref, 2) * out_c
    hmix = jnp.dot(merged.astype(BF16), wo_ref[...], preferred_element_type=F32)
    y = _layer_norm(alpha * x_ref[...] + hmix, g_ref[...], b_ref[...])
    y_ref[...] = y
    yb_ref[...] = y.astype(BF16)


def _merge(x, xb, z, att, mkv, w_in, b_gate, w_conv_out, w_diff_out, w_mem_out, w_o, ln_g, ln_b,
           layer, seq, mem_len, alpha, tm=256):
    t, d = x.shape
    per_b = seq // tm
    row = lambda i: (i, 0)
    wspec = lambda: pl.BlockSpec((None, d, d), lambda i: (layer, 0, 0), pipeline_mode=pl.Buffered(1))
    win = lambda col: pl.BlockSpec((None, d, d), lambda i: (layer, 0, col // d), pipeline_mode=pl.Buffered(1))
    vec = lambda n: pl.BlockSpec((None, 1, n), lambda i: (layer, 0, 0))
    kern = functools.partial(_merge_kernel, alpha=alpha)
    return pl.pallas_call(
        kern,
        grid=(t // tm,),
        in_specs=[
            pl.BlockSpec((tm, d), row),
            pl.BlockSpec((tm, d), row),
            pl.BlockSpec((tm, d), row),
            pl.BlockSpec((tm, d), row),
            pl.BlockSpec((mem_len, 2 * d), lambda i: (i // per_b, 0)),
            win(COL_MQ), win(COL_GATE), win(COL_GATE + d), win(COL_GATE + 2 * d),
            vec(N_BRANCH * d),
            wspec(), wspec(), wspec(), wspec(),
            vec(d), vec(d),
        ],
        out_specs=[pl.BlockSpec((tm, d), row), pl.BlockSpec((tm, d), row)],
        out_shape=[jax.ShapeDtypeStruct((t, d), F32), jax.ShapeDtypeStruct((t, d), BF16)],
        scratch_shapes=[pltpu.VMEM((tm, d), BF16)],
        compiler_params=_params("parallel"),
        name="merge",
    )(x, xb, z, att, mkv, w_in, w_in, w_in, w_in, b_gate,
      w_conv_out, w_diff_out, w_mem_out, w_o, ln_g, ln_b)


def _mlp_kernel(x_ref, xb_ref, w1_ref, w2_ref, g_ref, b_ref, y_ref, yb_ref, *, alpha, fc):
    xb = xb_ref[...]
    f = jnp.zeros(x_ref.shape, F32)
    for c in range(w1_ref.shape[1] // fc):
        cs = slice(c * fc, (c + 1) * fc)
        hid = jnp.maximum(jnp.dot(xb, w1_ref[:, cs], preferred_element_type=F32), 0.0)
        f = f + jnp.dot((hid * hid).astype(BF16), w2_ref[cs, :], preferred_element_type=F32)
    y = _layer_norm(alpha * x_ref[...] + f, g_ref[...], b_ref[...])
    y_ref[...] = y
    yb_ref[...] = y.astype(BF16)


def _mlp(x, xb, w1, w2, ln_g, ln_b, layer, alpha, tm=512, fc=1024):
    t, d = x.shape
    dff = w1.shape[2]
    row = lambda i: (i, 0)
    vec = lambda n: pl.BlockSpec((None, 1, n), lambda i: (layer, 0, 0))
    kern = functools.partial(_mlp_kernel, alpha=alpha, fc=fc)
    return pl.pallas_call(
        kern,
        grid=(t // tm,),
        in_specs=[
            pl.BlockSpec((tm, d), row),
            pl.BlockSpec((tm, d), row),
            pl.BlockSpec((None, d, dff), lambda i: (layer, 0, 0), pipeline_mode=pl.Buffered(1)),
            pl.BlockSpec((None, dff, d), lambda i: (layer, 0, 0), pipeline_mode=pl.Buffered(1)),
            vec(d), vec(d),
        ],
        out_specs=[pl.BlockSpec((tm, d), row), pl.BlockSpec((tm, d), row)],
        out_shape=[jax.ShapeDtypeStruct((t, d), F32), jax.ShapeDtypeStruct((t, d), BF16)],
        compiler_params=_params("parallel"),
        name="mlp",
    )(x, xb, w1, w2, ln_g, ln_b)


def kernel(x, mem, w_in, b_gate, conv_w, w_conv_out, diff_lambda, subln_g, w_diff_out, rel_bias,
           w_mem_kv, w_mem_out, w_o, ln1_g, ln1_b, w_mlp1, w_mlp2, ln2_g, ln2_b):
    batch, seq, d = x.shape
    mem_len = mem.shape[1]
    depth = w_in.shape[0]
    alpha = (2 * depth) ** 0.25

    w_in_b = w_in.astype(BF16)
    w_conv_out_b = w_conv_out.astype(BF16)
    w_diff_out_b = w_diff_out.astype(BF16)
    w_mem_kv_b = w_mem_kv.astype(BF16)
    w_mem_out_b = w_mem_out.astype(BF16)
    w_o_b = w_o.astype(BF16)
    w_mlp1_b = w_mlp1.astype(BF16)
    w_mlp2_b = w_mlp2.astype(BF16)
    mem_b = mem.reshape(batch * mem_len, d).astype(BF16)
    wq_t = jnp.swapaxes(w_in_b[:, :, COL_Q:COL_K], 1, 2)
    wv_t = jnp.swapaxes(w_in_b[:, :, COL_V:COL_MQ], 1, 2)
    vec3 = lambda a: a.reshape(depth, 1, a.shape[-1])

    pos = jnp.arange(seq, dtype=jnp.int32)
    bucket_kt = _t5_bucket(pos[:, None] - pos[None, :])
    bias_kt = _rel_bias_kt(rel_bias, bucket_kt)

    xf = x.reshape(batch * seq, d)
    xb = xf.astype(BF16)
    for l in range(depth):
        lam_init = 0.8 - 0.6 * math.exp(-0.3 * l)
        qt = _matmul_t(wq_t, xb, l, 1024, 1024, DIFF_DK ** -0.5 * LOG2E, BF16)
        k = _matmul(xb, w_in_b, l, COL_K, d, 1024, 1024, BF16)
        vt = _matmul_t(wv_t, xb, l, 1024, 1024, 1.0, BF16)
        att = _diff_attention(qt, k, vt, bias_kt, diff_lambda, vec3(subln_g), l, batch, seq, lam_init)
        z = _conv_branch(xb, w_in_b, conv_w, l, batch, seq)
        mkv = _matmul(mem_b, w_mem_kv_b, l, 0, 2 * d, 1024, 1024, BF16)
        xf, xb = _merge(xf, xb, z, att, mkv, w_in_b, vec3(b_gate), w_conv_out_b, w_diff_out_b,
                        w_mem_out_b, w_o_b, vec3(ln1_g), vec3(ln1_b), l, seq, mem_len, alpha)
        xf, xb = _mlp(xf, xb, w_mlp1_b, w_mlp2_b, vec3(ln2_g), vec3(ln2_b), l, alpha)
    return xf.reshape(batch, seq, d)
```

```python
import functools
import math

import jax
import jax.numpy as jnp
from jax import lax
from jax.experimental import pallas as pl
from jax.experimental.pallas import tpu as pltpu

F32 = jnp.float32
BF16 = jnp.bfloat16

D_MODEL = 1024
CONV_K = 3
DIFF_HEADS = 8
DIFF_DK = 64
DIFF_DV = 2 * DIFF_DK
MEM_HEADS = 4
MEM_DH = 256
N_BRANCH = 3
REL_BUCKETS = 32
REL_MAX_DIST = 128
LN_EPS = 1e-5
LOG2E = math.log2(math.e)
SUM_ROWS = 16

COL_CH, COL_CB, COL_CC = 0, D_MODEL, 2 * D_MODEL
COL_Q, COL_K, COL_V = 3 * D_MODEL, 4 * D_MODEL, 5 * D_MODEL
COL_MQ = 6 * D_MODEL
COL_GATE = 7 * D_MODEL

VMEM_LIMIT = 56 * 1024 * 1024

NT_DIMS = (((1,), (1,)), ((), ()))


def _params(*sem):
    return pltpu.CompilerParams(dimension_semantics=sem, vmem_limit_bytes=VMEM_LIMIT)


def _layer_norm(y, g, b):
    mu = jnp.mean(y, axis=-1, keepdims=True)
    yc = y - mu
    var = jnp.mean(yc * yc, axis=-1, keepdims=True)
    return yc * lax.rsqrt(var + LN_EPS) * g + b


def _mm_kernel(x_ref, w_ref, o_ref):
    o_ref[...] = jnp.dot(x_ref[...], w_ref[...], preferred_element_type=F32).astype(o_ref.dtype)


def _matmul(x, w_stack, layer, col0, ncols, tm, tn, out_dtype):
    m, k = x.shape
    cb0 = col0 // tn
    return pl.pallas_call(
        _mm_kernel,
        grid=(m // tm, ncols // tn),
        in_specs=[
            pl.BlockSpec((tm, k), lambda i, j: (i, 0)),
            pl.BlockSpec((None, k, tn), lambda i, j: (layer, 0, cb0 + j)),
        ],
        out_specs=pl.BlockSpec((tm, tn), lambda i, j: (i, j)),
        out_shape=jax.ShapeDtypeStruct((m, ncols), out_dtype),
        compiler_params=_params("parallel", "arbitrary"),
        name="matmul",
    )(x, w_stack)


def _mm_t_kernel(wt_ref, x_ref, o_ref, *, scale):
    acc = lax.dot_general(wt_ref[...], x_ref[...], NT_DIMS, preferred_element_type=F32)
    o_ref[...] = (acc * scale).astype(o_ref.dtype)


def _matmul_t(wt_stack, x, layer, tn, tm, scale, out_dtype):
    m, k = x.shape
    n = wt_stack.shape[1]
    return pl.pallas_call(
        functools.partial(_mm_t_kernel, scale=scale),
        grid=(m // tm, n // tn),
        in_specs=[
            pl.BlockSpec((None, tn, k), lambda i, j: (layer, j, 0)),
            pl.BlockSpec((tm, k), lambda i, j: (i, 0)),
        ],
        out_specs=pl.BlockSpec((tn, tm), lambda i, j: (j, i)),
        out_shape=jax.ShapeDtypeStruct((n, m), out_dtype),
        compiler_params=_params("parallel", "arbitrary"),
        name="matmul_t",
    )(wt_stack, x)


def _conv_kernel(x_ref, wh_ref, wb_ref, wc_ref, cw_ref, z_ref, u_ref):
    s = x_ref.shape[0]
    x = x_ref[...]
    pad = jnp.zeros((8, u_ref.shape[1]), F32)
    u_ref[0:8, :] = pad
    u_ref[s + 8:s + 16, :] = pad
    u_ref[8:s + 8, :] = (jnp.dot(x, wc_ref[...], preferred_element_type=F32)
                         * jnp.dot(x, wh_ref[...], preferred_element_type=F32))
    cb = jnp.dot(x, wb_ref[...], preferred_element_type=F32)
    cw = cw_ref[...]
    y = (cw[0:1, :] * u_ref[7:s + 7, :] + cw[1:2, :] * u_ref[8:s + 8, :]
         + cw[2:3, :] * u_ref[9:s + 9, :])
    z_ref[...] = (cb * y).astype(z_ref.dtype)


def _conv_branch(xb, w_in, conv_w, layer, batch, seq, tc=256):
    t, d = xb.shape
    nct = D_MODEL // tc
    return pl.pallas_call(
        _conv_kernel,
        grid=(batch, nct),
        in_specs=[
            pl.BlockSpec((seq, d), lambda b, j: (b, 0)),
            pl.BlockSpec((None, d, tc), lambda b, j: (layer, 0, COL_CH // tc + j)),
            pl.BlockSpec((None, d, tc), lambda b, j: (layer, 0, COL_CB // tc + j)),
            pl.BlockSpec((None, d, tc), lambda b, j: (layer, 0, COL_CC // tc + j)),
            pl.BlockSpec((None, CONV_K, tc), lambda b, j: (layer, 0, j)),
        ],
        out_specs=pl.BlockSpec((seq, tc), lambda b, j: (b, j)),
        out_shape=jax.ShapeDtypeStruct((t, D_MODEL), BF16),
        scratch_shapes=[pltpu.VMEM((seq + 16, tc), F32)],
        compiler_params=_params("parallel", "arbitrary"),
        name="conv_branch",
    )(xb, w_in, w_in, w_in, conv_w)


def _bias_kernel(rel_ref, bkt_ref, o_ref):
    h = pl.program_id(0)
    bkt = bkt_ref[...]
    lo = jnp.min(bkt)
    hi = jnp.max(bkt)

    @pl.when(lo == hi)
    def _():
        o_ref[...] = jnp.full(bkt.shape, rel_ref[lo, h], F32) * LOG2E

    @pl.when(lo != hi)
    def _():
        acc = jnp.zeros(bkt.shape, F32)
        for j in range(REL_BUCKETS):
            acc = jnp.where(bkt == j, rel_ref[j, h], acc)
        o_ref[...] = acc * LOG2E


def _rel_bias_kt(rel_bias, bucket_kt, tq=256, tk=128):
    s = bucket_kt.shape[0]
    return pl.pallas_call(
        _bias_kernel,
        grid=(DIFF_HEADS, s // tk, s // tq),
        in_specs=[
            pl.BlockSpec(memory_space=pltpu.SMEM),
            pl.BlockSpec((tk, tq), lambda h, i, j: (i, j)),
        ],
        out_specs=pl.BlockSpec((None, tk, tq), lambda h, i, j: (h, i, j)),
        out_shape=jax.ShapeDtypeStruct((DIFF_HEADS, s, s), F32),
        compiler_params=_params("parallel", "parallel", "arbitrary"),
        name="rel_bias",
    )(rel_bias, bucket_kt)


def _t5_bucket(rel):
    nb = REL_BUCKETS // 2
    max_exact = nb // 2
    ret = (rel > 0).astype(jnp.int32) * nb
    n = jnp.abs(rel)
    nf = jnp.maximum(n, 1).astype(F32)
    large = max_exact + (jnp.log(nf / max_exact) / math.log(REL_MAX_DIST / max_exact)
                         * (nb - max_exact)).astype(jnp.int32)
    large = jnp.minimum(large, nb - 1)
    return ret + jnp.where(n < max_exact, n, large)


def _diff_attn_kernel(lam_ref, qt_ref, k_ref, bias_ref, vt_ref, g_ref, o_ref,
                      s0_ref, s1_ref, m0_ref, m1_ref, e_ref, a0_ref, a1_ref, *, lam_init, kc):
    tq = qt_ref.shape[1]
    s = k_ref.shape[0]
    n = 2 * tq
    t = pl.program_id(0)

    @pl.when(t == 0)
    def _():
        s1_ref[...] = jnp.zeros(s1_ref.shape, F32)
        m1_ref[...] = jnp.zeros(m1_ref.shape, F32)
        a1_ref[...] = jnp.ones(a1_ref.shape, F32)

    def step(sw_ref, mw_ref, sr_ref, mr_ref, aw_ref, ar_ref):
        qt = qt_ref[...]
        row = lax.broadcasted_iota(jnp.int32, qt.shape, 0)
        zero = jnp.zeros_like(qt)
        q12 = jnp.concatenate([jnp.where(row < DIFF_DK, qt, zero), jnp.where(row >= DIFF_DK, qt, zero)], axis=1)
        b = bias_ref[...]
        sc = jnp.dot(k_ref[...], q12, preferred_element_type=F32) + jnp.concatenate([b, b], axis=1)
        sw_ref[...] = sc
        mw_ref[...] = jnp.max(sc.reshape(s // 8, 8, n), axis=0)

        m = jnp.max(mr_ref[...], axis=0, keepdims=True)
        for c in range(s // kc):
            sl = slice(c * kc, (c + 1) * kc)
            e_ref[sl, :] = jnp.exp2(sr_ref[sl, :] - m).astype(BF16)
        orow = lax.broadcasted_iota(jnp.int32, (SUM_ROWS, s), 0)
        ones = jnp.where(orow == 0, 1.0, 0.0).astype(BF16)
        vt = jnp.concatenate([vt_ref[...], ones], axis=0)
        aw_ref[...] = jnp.dot(vt, e_ref[...], preferred_element_type=F32)

        acc = ar_ref[...]
        lf = lam_ref[...]
        lam = (jnp.exp(jnp.sum(lf[0:1, :] * lf[1:2, :], axis=-1, keepdims=True))
               - jnp.exp(jnp.sum(lf[2:3, :] * lf[3:4, :], axis=-1, keepdims=True)) + lam_init)
        r = 1.0 / acc[DIFF_DV:DIFF_DV + 1, :]
        ot = acc[:DIFF_DV, :tq] * r[:, :tq] - acc[:DIFF_DV, tq:] * (lam * r[:, tq:])
        o = ot.T
        ms = jnp.mean(o * o, axis=-1, keepdims=True)
        o_ref[...] = (o * lax.rsqrt(ms + LN_EPS) * g_ref[...] * (1.0 - lam_init)).astype(o_ref.dtype)

    @pl.when(t % 2 == 0)
    def _():
        step(s0_ref, m0_ref, s1_ref, m1_ref, a0_ref, a1_ref)

    @pl.when(t % 2 == 1)
    def _():
        step(s1_ref, m1_ref, s0_ref, m0_ref, a1_ref, a0_ref)


def _diff_attention(qt, k, vt, bias_kt, diff_lambda, subln_g, layer, batch, seq, lam_init, tq=512, kc=256):
    tokens = k.shape[0]
    nqt = seq // tq
    ntile = DIFF_HEADS * nqt * batch

    def dec(tile):
        return tile // (batch * nqt), (tile // batch) % nqt, tile % batch

    cur = lambda t: dec(jnp.minimum(t, ntile - 1))
    prev = lambda t: dec(jnp.clip(t - 1, 0, ntile - 1))
    prev2 = lambda t: dec(jnp.maximum(t - 2, 0))

    def qt_map(t):
        h, i, b = cur(t)
        return (h, b * nqt + i)

    def k_map(t):
        h, i, b = cur(t)
        return (b, h)

    def bias_map(t):
        h, i, b = cur(t)
        return (h, 0, i)

    def vt_map(t):
        h, i, b = prev(t)
        return (h, b)

    def o_map(t):
        h, i, b = prev2(t)
        return (b * nqt + i, h)

    kern = functools.partial(_diff_attn_kernel, lam_init=lam_init, kc=kc)
    return pl.pallas_call(
        kern,
        grid=(ntile + 2,),
        in_specs=[
            pl.BlockSpec((None, 4, DIFF_DK), lambda t: (layer, 0, 0)),
            pl.BlockSpec((DIFF_DV, tq), qt_map),
            pl.BlockSpec((seq, DIFF_DV), k_map),
            pl.BlockSpec((None, seq, tq), bias_map),
            pl.BlockSpec((DIFF_DV, seq), vt_map),
            pl.BlockSpec((None, 1, DIFF_DV), lambda t: (layer, 0, 0)),
        ],
        out_specs=pl.BlockSpec((tq, DIFF_DV), o_map),
        out_shape=jax.ShapeDtypeStruct((tokens, DIFF_HEADS * DIFF_DV), BF16),
        scratch_shapes=[
            pltpu.VMEM((seq, 2 * tq), F32), pltpu.VMEM((seq, 2 * tq), F32),
            pltpu.VMEM((8, 2 * tq), F32), pltpu.VMEM((8, 2 * tq), F32),
            pltpu.VMEM((seq, 2 * tq), BF16),
            pltpu.VMEM((DIFF_DV + SUM_ROWS, 2 * tq), F32), pltpu.VMEM((DIFF_DV + SUM_ROWS, 2 * tq), F32),
        ],
        compiler_params=_params("arbitrary"),
        name="diff_attn",
    )(diff_lambda, qt, k, bias_kt, vt, subln_g)


def _merge_kernel(x_ref, xb_ref, z_ref, att_ref, mkv_ref,
                  wmq_ref, wga_ref, wgb_ref, wgc_ref, bg_ref,
                  wca_ref, wdo_ref, wmo_ref, wo_ref, g_ref, b_ref,
                  y_ref, yb_ref, o_scr, *, alpha):
    xb = xb_ref[...]
    mq = (jnp.dot(xb, wmq_ref[...], preferred_element_type=F32) * (MEM_DH ** -0.5)).astype(BF16)
    w = MEM_HEADS * MEM_DH
    for h in range(MEM_HEADS):
        hs = slice(h * MEM_DH, (h + 1) * MEM_DH)
        sc = lax.dot_general(mq[:, hs], mkv_ref[:, hs], NT_DIMS, preferred_element_type=F32)
        e = jnp.exp(sc - jnp.max(sc, axis=-1, keepdims=True))
        p = e * (1.0 / jnp.sum(e, axis=-1, keepdims=True))
        o_scr[:, hs] = jnp.dot(p.astype(BF16), mkv_ref[:, w + h * MEM_DH:w + (h + 1) * MEM_DH],
                               preferred_element_type=F32).astype(BF16)
    out_c = jnp.dot(o_scr[...], wmo_ref[...], preferred_element_type=F32)
    out_a = jnp.dot(z_ref[...], wca_ref[...], preferred_element_type=F32)
    out_b = jnp.dot(att_ref[...], wdo_ref[...], preferred_element_type=F32)
    d = D_MODEL

    def gate(wg_ref, j):
        gl = jnp.dot(xb, wg_ref[...], preferred_element_type=F32) + bg_ref[:, j * d:(j + 1) * d]
        return 1.0 / (1.0 + jnp.exp(-gl))

    merged = gate(wga_ref, 0) * out_a + gate(wgb_ref, 1) * out_b + gate(wgc_ref, 2) * out_c
    hmix = jnp.dot(merged.astype(BF16), wo_ref[...], preferred_element_type=F32)
    y = _layer_norm(alpha * x_ref[...] + hmix, g_ref[...], b_ref[...])
    y_ref[...] = y
    yb_ref[...] = y.astype(BF16)


def _merge(x, xb, z, att, mkv, w_in, b_gate, w_conv_out, w_diff_out, w_mem_out, w_o, ln_g, ln_b,
           layer, seq, mem_len, alpha, tm=512):
    t, d = x.shape
    per_b = seq // tm
    row = lambda i: (i, 0)
    wspec = lambda: pl.BlockSpec((None, d, d), lambda i: (layer, 0, 0), pipeline_mode=pl.Buffered(1))
    win = lambda col: pl.BlockSpec((None, d, d), lambda i: (layer, 0, col // d), pipeline_mode=pl.Buffered(1))
    vec = lambda n: pl.BlockSpec((None, 1, n), lambda i: (layer, 0, 0))
    kern = functools.partial(_merge_kernel, alpha=alpha)
    return pl.pallas_call(
        kern,
        grid=(t // tm,),
        in_specs=[
            pl.BlockSpec((tm, d), row),
            pl.BlockSpec((tm, d), row),
            pl.BlockSpec((tm, d), row),
            pl.BlockSpec((tm, d), row),
            pl.BlockSpec((mem_len, 2 * d), lambda i: (i // per_b, 0)),
            win(COL_MQ), win(COL_GATE), win(COL_GATE + d), win(COL_GATE + 2 * d),
            vec(N_BRANCH * d),
            wspec(), wspec(), wspec(), wspec(),
            vec(d), vec(d),
        ],
        out_specs=[pl.BlockSpec((tm, d), row), pl.BlockSpec((tm, d), row)],
        out_shape=[jax.ShapeDtypeStruct((t, d), F32), jax.ShapeDtypeStruct((t, d), BF16)],
        scratch_shapes=[pltpu.VMEM((tm, d), BF16)],
        compiler_params=_params("parallel"),
        name="merge",
    )(x, xb, z, att, mkv, w_in, w_in, w_in, w_in, b_gate,
      w_conv_out, w_diff_out, w_mem_out, w_o, ln_g, ln_b)


def _mlp_kernel(x_ref, xb_ref, w1_ref, w2_ref, g_ref, b_ref, y_ref, yb_ref, *, alpha, fc):
    xb = xb_ref[...]
    f = jnp.zeros(x_ref.shape, F32)
    for c in range(w1_ref.shape[1] // fc):
        cs = slice(c * fc, (c + 1) * fc)
        hid = jnp.maximum(jnp.dot(xb, w1_ref[:, cs], preferred_element_type=F32), 0.0)
        f = f + jnp.dot((hid * hid).astype(BF16), w2_ref[cs, :], preferred_element_type=F32)
    y = _layer_norm(alpha * x_ref[...] + f, g_ref[...], b_ref[...])
    y_ref[...] = y
    yb_ref[...] = y.astype(BF16)


def _mlp(x, xb, w1, w2, ln_g, ln_b, layer, alpha, tm=512, fc=1024):
    t, d = x.shape
    dff = w1.shape[2]
    row = lambda i: (i, 0)
    vec = lambda n: pl.BlockSpec((None, 1, n), lambda i: (layer, 0, 0))
    kern = functools.partial(_mlp_kernel, alpha=alpha, fc=fc)
    return pl.pallas_call(
        kern,
        grid=(t // tm,),
        in_specs=[
            pl.BlockSpec((tm, d), row),
            pl.BlockSpec((tm, d), row),
            pl.BlockSpec((None, d, dff), lambda i: (layer, 0, 0), pipeline_mode=pl.Buffered(1)),
            pl.BlockSpec((None, dff, d), lambda i: (layer, 0, 0), pipeline_mode=pl.Buffered(1)),
            vec(d), vec(d),
        ],
        out_specs=[pl.BlockSpec((tm, d), row), pl.BlockSpec((tm, d), row)],
        out_shape=[jax.ShapeDtypeStruct((t, d), F32), jax.ShapeDtypeStruct((t, d), BF16)],
        compiler_params=_params("parallel"),
        name="mlp",
    )(x, xb, w1, w2, ln_g, ln_b)


def kernel(x, mem, w_in, b_gate, conv_w, w_conv_out, diff_lambda, subln_g, w_diff_out, rel_bias,
           w_mem_kv, w_mem_out, w_o, ln1_g, ln1_b, w_mlp1, w_mlp2, ln2_g, ln2_b):
    batch, seq, d = x.shape
    mem_len = mem.shape[1]
    depth = w_in.shape[0]
    alpha = (2 * depth) ** 0.25

    w_in_b = w_in.astype(BF16)
    w_conv_out_b = w_conv_out.astype(BF16)
    w_diff_out_b = w_diff_out.astype(BF16)
    w_mem_kv_b = w_mem_kv.astype(BF16)
    w_mem_out_b = w_mem_out.astype(BF16)
    w_o_b = w_o.astype(BF16)
    w_mlp1_b = w_mlp1.astype(BF16)
    w_mlp2_b = w_mlp2.astype(BF16)
    mem_b = mem.reshape(batch * mem_len, d).astype(BF16)
    wq_t = jnp.swapaxes(w_in_b[:, :, COL_Q:COL_K], 1, 2)
    wv_t = jnp.swapaxes(w_in_b[:, :, COL_V:COL_MQ], 1, 2)
    vec3 = lambda a: a.reshape(depth, 1, a.shape[-1])

    pos = jnp.arange(seq, dtype=jnp.int32)
    bucket_kt = _t5_bucket(pos[:, None] - pos[None, :])
    bias_kt = _rel_bias_kt(rel_bias, bucket_kt)

    xf = x.reshape(batch * seq, d)
    xb = xf.astype(BF16)
    for l in range(depth):
        lam_init = 0.8 - 0.6 * math.exp(-0.3 * l)
        qt = _matmul_t(wq_t, xb, l, 1024, 1024, DIFF_DK ** -0.5 * LOG2E, BF16)
        k = _matmul(xb, w_in_b, l, COL_K, d, 1024, 1024, BF16)
        vt = _matmul_t(wv_t, xb, l, 1024, 1024, 1.0, BF16)
        att = _diff_attention(qt, k, vt, bias_kt, diff_lambda, vec3(subln_g), l, batch, seq, lam_init)
        z = _conv_branch(xb, w_in_b, conv_w, l, batch, seq)
        mkv = _matmul(mem_b, w_mem_kv_b, l, 0, 2 * d, 1024, 1024, BF16)
        xf, xb = _merge(xf, xb, z, att, mkv, w_in_b, vec3(b_gate), w_conv_out_b, w_diff_out_b,
                        w_mem_out_b, w_o_b, vec3(ln1_g), vec3(ln1_b), l, seq, mem_len, alpha)
        xf, xb = _mlp(xf, xb, w_mlp1_b, w_mlp2_b, vec3(ln2_g), vec3(ln2_b), l, alpha)
    return xf.reshape(batch, seq, d)
```

```python
import functools
import math

import jax
import jax.numpy as jnp
from jax import lax
from jax.experimental import pallas as pl
from jax.experimental.pallas import tpu as pltpu

F32 = jnp.float32
BF16 = jnp.bfloat16

D_MODEL = 1024
CONV_K = 3
DIFF_HEADS = 8
DIFF_DK = 64
DIFF_DV = 2 * DIFF_DK
MEM_HEADS = 4
MEM_DH = 256
N_BRANCH = 3
REL_BUCKETS = 32
REL_MAX_DIST = 128
LN_EPS = 1e-5
LOG2E = math.log2(math.e)
SUM_ROWS = 16
LANES = 128

COL_CH, COL_CB, COL_CC = 0, D_MODEL, 2 * D_MODEL
COL_Q, COL_K, COL_V = 3 * D_MODEL, 4 * D_MODEL, 5 * D_MODEL
COL_MQ = 6 * D_MODEL
COL_GATE = 7 * D_MODEL

VMEM_LIMIT = 56 * 1024 * 1024

NT_DIMS = (((1,), (1,)), ((), ()))


def _params(*sem):
    return pltpu.CompilerParams(dimension_semantics=sem, vmem_limit_bytes=VMEM_LIMIT)


def _layer_norm(y, g, b):
    mu = jnp.mean(y, axis=-1, keepdims=True)
    yc = y - mu
    var = jnp.mean(yc * yc, axis=-1, keepdims=True)
    return yc * lax.rsqrt(var + LN_EPS) * g + b


def _mm_kernel(x_ref, w_ref, o_ref):
    o_ref[...] = jnp.dot(x_ref[...], w_ref[...], preferred_element_type=F32).astype(o_ref.dtype)


def _matmul(x, w_stack, layer, col0, ncols, tm, tn, out_dtype):
    m, k = x.shape
    cb0 = col0 // tn
    return pl.pallas_call(
        _mm_kernel,
        grid=(m // tm, ncols // tn),
        in_specs=[
            pl.BlockSpec((tm, k), lambda i, j: (i, 0)),
            pl.BlockSpec((None, k, tn), lambda i, j: (layer, 0, cb0 + j)),
        ],
        out_specs=pl.BlockSpec((tm, tn), lambda i, j: (i, j)),
        out_shape=jax.ShapeDtypeStruct((m, ncols), out_dtype),
        compiler_params=_params("parallel", "arbitrary"),
        name="matmul",
    )(x, w_stack)


def _mm_t_kernel(wt_ref, x_ref, o_ref, *, scale):
    acc = lax.dot_general(wt_ref[...], x_ref[...], NT_DIMS, preferred_element_type=F32)
    o_ref[...] = (acc * scale).astype(o_ref.dtype)


def _matmul_t(wt_stack, x, layer, tn, tm, scale, out_dtype):
    m, k = x.shape
    n = wt_stack.shape[1]
    return pl.pallas_call(
        functools.partial(_mm_t_kernel, scale=scale),
        grid=(m // tm, n // tn),
        in_specs=[
            pl.BlockSpec((None, tn, k), lambda i, j: (layer, j, 0)),
            pl.BlockSpec((tm, k), lambda i, j: (i, 0)),
        ],
        out_specs=pl.BlockSpec((tn, tm), lambda i, j: (j, i)),
        out_shape=jax.ShapeDtypeStruct((n, m), out_dtype),
        compiler_params=_params("parallel", "arbitrary"),
        name="matmul_t",
    )(wt_stack, x)


def _conv_kernel(x_ref, wh_ref, wb_ref, wc_ref, cw_ref, z_ref, u_ref):
    s = x_ref.shape[0]
    x = x_ref[...]
    pad = jnp.zeros((8, u_ref.shape[1]), F32)
    u_ref[0:8, :] = pad
    u_ref[s + 8:s + 16, :] = pad
    u_ref[8:s + 8, :] = (jnp.dot(x, wc_ref[...], preferred_element_type=F32)
                         * jnp.dot(x, wh_ref[...], preferred_element_type=F32))
    cb = jnp.dot(x, wb_ref[...], preferred_element_type=F32)
    cw = cw_ref[...]
    y = (cw[0:1, :] * u_ref[7:s + 7, :] + cw[1:2, :] * u_ref[8:s + 8, :]
         + cw[2:3, :] * u_ref[9:s + 9, :])
    z_ref[...] = (cb * y).astype(z_ref.dtype)


def _conv_branch(xb, w_in, conv_w, layer, batch, seq, tc=256):
    t, d = xb.shape
    nct = D_MODEL // tc
    return pl.pallas_call(
        _conv_kernel,
        grid=(batch, nct),
        in_specs=[
            pl.BlockSpec((seq, d), lambda b, j: (b, 0)),
            pl.BlockSpec((None, d, tc), lambda b, j: (layer, 0, COL_CH // tc + j)),
            pl.BlockSpec((None, d, tc), lambda b, j: (layer, 0, COL_CB // tc + j)),
            pl.BlockSpec((None, d, tc), lambda b, j: (layer, 0, COL_CC // tc + j)),
            pl.BlockSpec((None, CONV_K, tc), lambda b, j: (layer, 0, j)),
        ],
        out_specs=pl.BlockSpec((seq, tc), lambda b, j: (b, j)),
        out_shape=jax.ShapeDtypeStruct((t, D_MODEL), BF16),
        scratch_shapes=[pltpu.VMEM((seq + 16, tc), F32)],
        compiler_params=_params("parallel", "arbitrary"),
        name="conv_branch",
    )(xb, w_in, w_in, w_in, conv_w)


def _bias_kernel(rel_ref, lo_ref, hi_ref, bkt_ref, o_ref, *, rows):
    j = pl.program_id(0)
    h = pl.program_id(1)

    def chunk(c, carry):
        r0 = pl.multiple_of(c * rows, rows)
        lo = lo_ref[c, j]

        @pl.when(lo == hi_ref[c, j])
        def _():
            o_ref[pl.ds(r0, rows), :] = jnp.full((rows, o_ref.shape[1]), rel_ref[lo, h], F32) * LOG2E

        @pl.when(lo != hi_ref[c, j])
        def _():
            bkt = bkt_ref[pl.ds(r0, rows), :]
            acc = jnp.zeros(bkt.shape, F32)
            for b in range(REL_BUCKETS):
                acc = jnp.where(bkt == b, rel_ref[b, h], acc)
            o_ref[pl.ds(r0, rows), :] = acc * LOG2E

        return carry

    lax.fori_loop(0, bkt_ref.shape[0] // rows, chunk, 0)


def _rel_bias_kt(rel_bias, bucket_kt, tq=256, rows=128):
    s = bucket_kt.shape[0]
    blocks = bucket_kt.reshape(s // rows, rows, s // tq, tq)
    smem = pl.BlockSpec(memory_space=pltpu.SMEM)
    return pl.pallas_call(
        functools.partial(_bias_kernel, rows=rows),
        grid=(s // tq, DIFF_HEADS),
        in_specs=[smem, smem, smem, pl.BlockSpec((s, tq), lambda j, h: (0, j))],
        out_specs=pl.BlockSpec((None, s, tq), lambda j, h: (h, 0, j)),
        out_shape=jax.ShapeDtypeStruct((DIFF_HEADS, s, s), F32),
        compiler_params=_params("parallel", "arbitrary"),
        name="rel_bias",
    )(rel_bias, blocks.min(axis=(1, 3)), blocks.max(axis=(1, 3)), bucket_kt)


def _t5_bucket(rel):
    nb = REL_BUCKETS // 2
    max_exact = nb // 2
    ret = (rel > 0).astype(jnp.int32) * nb
    n = jnp.abs(rel)
    nf = jnp.maximum(n, 1).astype(F32)
    large = max_exact + (jnp.log(nf / max_exact) / math.log(REL_MAX_DIST / max_exact)
                         * (nb - max_exact)).astype(jnp.int32)
    large = jnp.minimum(large, nb - 1)
    return ret + jnp.where(n < max_exact, n, large)


def _diff_attn_kernel(lam_ref, qt_ref, k_ref, bias_ref, vt_ref, g_ref, o_ref,
                      s0_ref, s1_ref, m0_ref, m1_ref, e_ref, a0_ref, a1_ref, *, lam_init, kc):
    tq = qt_ref.shape[1]
    s = k_ref.shape[0]
    n = 2 * tq
    t = pl.program_id(0)

    @pl.when(t == 0)
    def _():
        s1_ref[...] = jnp.zeros(s1_ref.shape, F32)
        m1_ref[...] = jnp.zeros(m1_ref.shape, F32)
        a1_ref[...] = jnp.ones(a1_ref.shape, F32)

    def step(sw_ref, mw_ref, sr_ref, mr_ref, aw_ref, ar_ref):
        qt = qt_ref[...]
        row = lax.broadcasted_iota(jnp.int32, qt.shape, 0)
        zero = jnp.zeros_like(qt)
        q12 = jnp.concatenate([jnp.where(row < DIFF_DK, qt, zero), jnp.where(row >= DIFF_DK, qt, zero)], axis=1)
        b = bias_ref[...]
        sc = jnp.dot(k_ref[...], q12, preferred_element_type=F32) + jnp.concatenate([b, b], axis=1)
        sw_ref[:, :n] = sc
        mw_ref[...] = jnp.max(sc.reshape(s // 8, 8, n), axis=0)

        m = jnp.max(mr_ref[...], axis=0, keepdims=True)
        for c in range(s // kc):
            sl = slice(c * kc, (c + 1) * kc)
            e_ref[sl, :n] = jnp.exp2(sr_ref[sl, :n] - m).astype(BF16)
        orow = lax.broadcasted_iota(jnp.int32, (SUM_ROWS, s), 0)
        ones = jnp.where(orow == 0, 1.0, 0.0).astype(BF16)
        vt = jnp.concatenate([vt_ref[...], ones], axis=0)
        aw_ref[...] = jnp.dot(vt, e_ref[:, :n], preferred_element_type=F32)

        acc = ar_ref[...]
        lf = lam_ref[...]
        lam = (jnp.exp(jnp.sum(lf[0:1, :] * lf[1:2, :], axis=-1, keepdims=True))
               - jnp.exp(jnp.sum(lf[2:3, :] * lf[3:4, :], axis=-1, keepdims=True)) + lam_init)
        r = 1.0 / acc[DIFF_DV:DIFF_DV + 1, :]
        ot = acc[:DIFF_DV, :tq] * r[:, :tq] - acc[:DIFF_DV, tq:] * (lam * r[:, tq:])
        o = ot.T
        ms = jnp.mean(o * o, axis=-1, keepdims=True)
        o_ref[...] = (o * lax.rsqrt(ms + LN_EPS) * g_ref[...] * (1.0 - lam_init)).astype(o_ref.dtype)

    @pl.when(t % 2 == 0)
    def _():
        step(s0_ref, m0_ref, s1_ref, m1_ref, a0_ref, a1_ref)

    @pl.when(t % 2 == 1)
    def _():
        step(s1_ref, m1_ref, s0_ref, m0_ref, a1_ref, a0_ref)


def _diff_attention(qt, k, vt, bias_kt, diff_lambda, subln_g, layer, batch, seq, lam_init, tq=512, kc=256):
    tokens = k.shape[0]
    nqt = seq // tq
    ntile = DIFF_HEADS * nqt * batch

    def dec(tile):
        return tile // (batch * nqt), (tile // batch) % nqt, tile % batch

    cur = lambda t: dec(jnp.minimum(t, ntile - 1))
    prev = lambda t: dec(jnp.clip(t - 1, 0, ntile - 1))
    prev2 = lambda t: dec(jnp.maximum(t - 2, 0))

    def qt_map(t):
        h, i, b = cur(t)
        return (h, b * nqt + i)

    def k_map(t):
        h, i, b = cur(t)
        return (b, h)

    def bias_map(t):
        h, i, b = cur(t)
        return (h, 0, i)

    def vt_map(t):
        h, i, b = prev(t)
        return (h, b)

    def o_map(t):
        h, i, b = prev2(t)
        return (b * nqt + i, h)

    kern = functools.partial(_diff_attn_kernel, lam_init=lam_init, kc=kc)
    return pl.pallas_call(
        kern,
        grid=(ntile + 2,),
        in_specs=[
            pl.BlockSpec((None, 4, DIFF_DK), lambda t: (layer, 0, 0)),
            pl.BlockSpec((DIFF_DV, tq), qt_map),
            pl.BlockSpec((seq, DIFF_DV), k_map),
            pl.BlockSpec((None, seq, tq), bias_map),
            pl.BlockSpec((DIFF_DV, seq), vt_map),
            pl.BlockSpec((None, 1, DIFF_DV), lambda t: (layer, 0, 0)),
        ],
        out_specs=pl.BlockSpec((tq, DIFF_DV), o_map),
        out_shape=jax.ShapeDtypeStruct((tokens, DIFF_HEADS * DIFF_DV), BF16),
        scratch_shapes=[
            pltpu.VMEM((seq, 2 * tq + LANES), F32), pltpu.VMEM((seq, 2 * tq + LANES), F32),
            pltpu.VMEM((8, 2 * tq), F32), pltpu.VMEM((8, 2 * tq), F32),
            pltpu.VMEM((seq, 2 * tq + LANES), BF16),
            pltpu.VMEM((DIFF_DV + SUM_ROWS, 2 * tq), F32), pltpu.VMEM((DIFF_DV + SUM_ROWS, 2 * tq), F32),
        ],
        compiler_params=_params("arbitrary"),
        name="diff_attn",
    )(diff_lambda, qt, k, bias_kt, vt, subln_g)


def _merge_kernel(x_ref, xb_ref, z_ref, att_ref, mkv_ref,
                  wmq_ref, wga_ref, wgb_ref, wgc_ref, bg_ref,
                  wca_ref, wdo_ref, wmo_ref, wo_ref, g_ref, b_ref,
                  y_ref, yb_ref, o_scr, *, alpha):
    xb = xb_ref[...]
    mq = (jnp.dot(xb, wmq_ref[...], preferred_element_type=F32) * (MEM_DH ** -0.5)).astype(BF16)
    w = MEM_HEADS * MEM_DH
    for h in range(MEM_HEADS):
        hs = slice(h * MEM_DH, (h + 1) * MEM_DH)
        sc = lax.dot_general(mq[:, hs], mkv_ref[:, hs], NT_DIMS, preferred_element_type=F32)
        e = jnp.exp(sc - jnp.max(sc, axis=-1, keepdims=True))
        p = e * (1.0 / jnp.sum(e, axis=-1, keepdims=True))
        o_scr[:, hs] = jnp.dot(p.astype(BF16), mkv_ref[:, w + h * MEM_DH:w + (h + 1) * MEM_DH],
                               preferred_element_type=F32).astype(BF16)
    out_c = jnp.dot(o_scr[...], wmo_ref[...], preferred_element_type=F32)
    out_a = jnp.dot(z_ref[...], wca_ref[...], preferred_element_type=F32)
    out_b = jnp.dot(att_ref[...], wdo_ref[...], preferred_element_type=F32)
    d = D_MODEL

    def gate(wg_ref, j):
        gl = jnp.dot(xb, wg_ref[...], preferred_element_type=F32) + bg_ref[:, j * d:(j + 1) * d]
        return 1.0 / (1.0 + jnp.exp(-gl))

    merged = gate(wga_ref, 0) * out_a + gate(wgb_ref, 1) * out_b + gate(wgc_ref, 2) * out_c
    hmix = jnp.dot(merged.astype(BF16), wo_ref[...], preferred_element_type=F32)
    y = _layer_norm(alpha * x_ref[...] + hmix, g_ref[...], b_ref[...])
    y_ref[...] = y
    yb_ref[...] = y.astype(BF16)


def _merge(x, xb, z, att, mkv, w_in, b_gate, w_conv_out, w_diff_out, w_mem_out, w_o, ln_g, ln_b,
           layer, seq, mem_len, alpha, tm=512):
    t, d = x.shape
    per_b = seq // tm
    row = lambda i: (i, 0)
    wspec = lambda: pl.BlockSpec((None, d, d), lambda i: (layer, 0, 0), pipeline_mode=pl.Buffered(1))
    win = lambda col: pl.BlockSpec((None, d, d), lambda i: (layer, 0, col // d), pipeline_mode=pl.Buffered(1))
    vec = lambda n: pl.BlockSpec((None, 1, n), lambda i: (layer, 0, 0))
    kern = functools.partial(_merge_kernel, alpha=alpha)
    return pl.pallas_call(
        kern,
        grid=(t // tm,),
        in_specs=[
            pl.BlockSpec((tm, d), row),
            pl.BlockSpec((tm, d), row),
            pl.BlockSpec((tm, d), row),
            pl.BlockSpec((tm, d), row),
            pl.BlockSpec((mem_len, 2 * d), lambda i: (i // per_b, 0)),
            win(COL_MQ), win(COL_GATE), win(COL_GATE + d), win(COL_GATE + 2 * d),
            vec(N_BRANCH * d),
            wspec(), wspec(), wspec(), wspec(),
            vec(d), vec(d),
        ],
        out_specs=[pl.BlockSpec((tm, d), row), pl.BlockSpec((tm, d), row)],
        out_shape=[jax.ShapeDtypeStruct((t, d), F32), jax.ShapeDtypeStruct((t, d), BF16)],
        scratch_shapes=[pltpu.VMEM((tm, d), BF16)],
        compiler_params=_params("parallel"),
        name="merge",
    )(x, xb, z, att, mkv, w_in, w_in, w_in, w_in, b_gate,
      w_conv_out, w_diff_out, w_mem_out, w_o, ln_g, ln_b)


def _mlp_kernel(x_ref, xb_ref, w1_ref, w2_ref, g_ref, b_ref, y_ref, yb_ref, *, alpha, fc):
    xb = xb_ref[...]
    f = jnp.zeros(x_ref.shape, F32)
    for c in range(w1_ref.shape[1] // fc):
        cs = slice(c * fc, (c + 1) * fc)
        hid = jnp.maximum(jnp.dot(xb, w1_ref[:, cs], preferred_element_type=F32), 0.0)
        f = f + jnp.dot((hid * hid).astype(BF16), w2_ref[cs, :], preferred_element_type=F32)
    y = _layer_norm(alpha * x_ref[...] + f, g_ref[...], b_ref[...])
    y_ref[...] = y
    yb_ref[...] = y.astype(BF16)


def _mlp(x, xb, w1, w2, ln_g, ln_b, layer, alpha, tm=512, fc=1024):
    t, d = x.shape
    dff = w1.shape[2]
    row = lambda i: (i, 0)
    vec = lambda n: pl.BlockSpec((None, 1, n), lambda i: (layer, 0, 0))
    kern = functools.partial(_mlp_kernel, alpha=alpha, fc=fc)
    return pl.pallas_call(
        kern,
        grid=(t // tm,),
        in_specs=[
            pl.BlockSpec((tm, d), row),
            pl.BlockSpec((tm, d), row),
            pl.BlockSpec((None, d, dff), lambda i: (layer, 0, 0), pipeline_mode=pl.Buffered(1)),
            pl.BlockSpec((None, dff, d), lambda i: (layer, 0, 0), pipeline_mode=pl.Buffered(1)),
            vec(d), vec(d),
        ],
        out_specs=[pl.BlockSpec((tm, d), row), pl.BlockSpec((tm, d), row)],
        out_shape=[jax.ShapeDtypeStruct((t, d), F32), jax.ShapeDtypeStruct((t, d), BF16)],
        compiler_params=_params("parallel"),
        name="mlp",
    )(x, xb, w1, w2, ln_g, ln_b)


def kernel(x, mem, w_in, b_gate, conv_w, w_conv_out, diff_lambda, subln_g, w_diff_out, rel_bias,
           w_mem_kv, w_mem_out, w_o, ln1_g, ln1_b, w_mlp1, w_mlp2, ln2_g, ln2_b):
    batch, seq, d = x.shape
    mem_len = mem.shape[1]
    depth = w_in.shape[0]
    alpha = (2 * depth) ** 0.25

    w_in_b = w_in.astype(BF16)
    w_conv_out_b = w_conv_out.astype(BF16)
    w_diff_out_b = w_diff_out.astype(BF16)
    w_mem_kv_b = w_mem_kv.astype(BF16)
    w_mem_out_b = w_mem_out.astype(BF16)
    w_o_b = w_o.astype(BF16)
    w_mlp1_b = w_mlp1.astype(BF16)
    w_mlp2_b = w_mlp2.astype(BF16)
    mem_b = mem.reshape(batch * mem_len, d).astype(BF16)
    wq_t = jnp.swapaxes(w_in_b[:, :, COL_Q:COL_K], 1, 2)
    wv_t = jnp.swapaxes(w_in_b[:, :, COL_V:COL_MQ], 1, 2)
    vec3 = lambda a: a.reshape(depth, 1, a.shape[-1])

    pos = jnp.arange(seq, dtype=jnp.int32)
    bucket_kt = _t5_bucket(pos[:, None] - pos[None, :])
    bias_kt = _rel_bias_kt(rel_bias, bucket_kt)

    xf = x.reshape(batch * seq, d)
    xb = xf.astype(BF16)
    for l in range(depth):
        lam_init = 0.8 - 0.6 * math.exp(-0.3 * l)
        qt = _matmul_t(wq_t, xb, l, 1024, 1024, DIFF_DK ** -0.5 * LOG2E, BF16)
        k = _matmul(xb, w_in_b, l, COL_K, d, 1024, 1024, BF16)
        vt = _matmul_t(wv_t, xb, l, 1024, 1024, 1.0, BF16)
        att = _diff_attention(qt, k, vt, bias_kt, diff_lambda, vec3(subln_g), l, batch, seq, lam_init)
        z = _conv_branch(xb, w_in_b, conv_w, l, batch, seq)
        mkv = _matmul(mem_b, w_mem_kv_b, l, 0, 2 * d, 1024, 1024, BF16)
        xf, xb = _merge(xf, xb, z, att, mkv, w_in_b, vec3(b_gate), w_conv_out_b, w_diff_out_b,
                        w_mem_out_b, w_o_b, vec3(ln1_g), vec3(ln1_b), l, seq, mem_len, alpha)
        xf, xb = _mlp(xf, xb, w_mlp1_b, w_mlp2_b, vec3(ln2_g), vec3(ln2_b), l, alpha)
    return xf.reshape(batch, seq, d)
```

```python
import functools
import math

import jax
import jax.numpy as jnp
from jax import lax
from jax.experimental import pallas as pl
from jax.experimental.pallas import tpu as pltpu

F32 = jnp.float32
BF16 = jnp.bfloat16

D_MODEL = 1024
CONV_K = 3
DIFF_HEADS = 8
DIFF_DK = 64
DIFF_DV = 2 * DIFF_DK
MEM_HEADS = 4
MEM_DH = 256
N_BRANCH = 3
REL_BUCKETS = 32
REL_MAX_DIST = 128
LN_EPS = 1e-5
LOG2E = math.log2(math.e)
SUM_ROWS = 16

COL_CH, COL_CB, COL_CC = 0, D_MODEL, 2 * D_MODEL
COL_Q, COL_K, COL_V = 3 * D_MODEL, 4 * D_MODEL, 5 * D_MODEL
COL_MQ = 6 * D_MODEL
COL_GATE = 7 * D_MODEL

VMEM_LIMIT = 56 * 1024 * 1024

NT_DIMS = (((1,), (1,)), ((), ()))


def _params(*sem):
    return pltpu.CompilerParams(dimension_semantics=sem, vmem_limit_bytes=VMEM_LIMIT)


def _layer_norm(y, g, b):
    mu = jnp.mean(y, axis=-1, keepdims=True)
    yc = y - mu
    var = jnp.mean(yc * yc, axis=-1, keepdims=True)
    return yc * lax.rsqrt(var + LN_EPS) * g + b


def _mm_kernel(x_ref, w_ref, o_ref):
    o_ref[...] = jnp.dot(x_ref[...], w_ref[...], preferred_element_type=F32).astype(o_ref.dtype)


def _matmul(x, w_stack, layer, col0, ncols, tm, tn, out_dtype):
    m, k = x.shape
    cb0 = col0 // tn
    return pl.pallas_call(
        _mm_kernel,
        grid=(m // tm, ncols // tn),
        in_specs=[
            pl.BlockSpec((tm, k), lambda i, j: (i, 0)),
            pl.BlockSpec((None, k, tn), lambda i, j: (layer, 0, cb0 + j)),
        ],
        out_specs=pl.BlockSpec((tm, tn), lambda i, j: (i, j)),
        out_shape=jax.ShapeDtypeStruct((m, ncols), out_dtype),
        compiler_params=_params("parallel", "arbitrary"),
        name="matmul",
    )(x, w_stack)


def _mm_heads_kernel(x_ref, w_ref, o_ref):
    heads, _, dh = o_ref.shape
    res = jnp.dot(x_ref[...], w_ref[...], preferred_element_type=F32).astype(o_ref.dtype)
    for h in range(heads):
        o_ref[h] = res[:, h * dh:(h + 1) * dh]


def _matmul_heads(x, w_stack, layer, col0, heads, dh, tm, out_dtype):
    m, k = x.shape
    n = heads * dh
    return pl.pallas_call(
        _mm_heads_kernel,
        grid=(m // tm,),
        in_specs=[
            pl.BlockSpec((tm, k), lambda i: (i, 0)),
            pl.BlockSpec((None, k, n), lambda i: (layer, 0, col0 // n)),
        ],
        out_specs=pl.BlockSpec((heads, tm, dh), lambda i: (0, i, 0)),
        out_shape=jax.ShapeDtypeStruct((heads, m, dh), out_dtype),
        compiler_params=_params("parallel"),
        name="matmul_heads",
    )(x, w_stack)


def _mm_t_kernel(wt_ref, x_ref, o_ref, *, scale):
    acc = lax.dot_general(wt_ref[...], x_ref[...], NT_DIMS, preferred_element_type=F32)
    o_ref[...] = (acc * scale).astype(o_ref.dtype)


def _matmul_t(wt_stack, x, layer, tn, tm, scale, out_dtype):
    m, k = x.shape
    n = wt_stack.shape[1]
    return pl.pallas_call(
        functools.partial(_mm_t_kernel, scale=scale),
        grid=(m // tm, n // tn),
        in_specs=[
            pl.BlockSpec((None, tn, k), lambda i, j: (layer, j, 0)),
            pl.BlockSpec((tm, k), lambda i, j: (i, 0)),
        ],
        out_specs=pl.BlockSpec((tn, tm), lambda i, j: (j, i)),
        out_shape=jax.ShapeDtypeStruct((n, m), out_dtype),
        compiler_params=_params("parallel", "arbitrary"),
        name="matmul_t",
    )(wt_stack, x)


def _conv_kernel(x_ref, wh_ref, wb_ref, wc_ref, cw_ref, z_ref, u_ref):
    s = x_ref.shape[0]
    x = x_ref[...]
    pad = jnp.zeros((8, u_ref.shape[1]), F32)
    u_ref[0:8, :] = pad
    u_ref[s + 8:s + 16, :] = pad
    u_ref[8:s + 8, :] = (jnp.dot(x, wc_ref[...], preferred_element_type=F32)
                         * jnp.dot(x, wh_ref[...], preferred_element_type=F32))
    cb = jnp.dot(x, wb_ref[...], preferred_element_type=F32)
    cw = cw_ref[...]
    y = (cw[0:1, :] * u_ref[7:s + 7, :] + cw[1:2, :] * u_ref[8:s + 8, :]
         + cw[2:3, :] * u_ref[9:s + 9, :])
    z_ref[...] = (cb * y).astype(z_ref.dtype)


def _conv_branch(xb, w_in, conv_w, layer, batch, seq, tc=256):
    t, d = xb.shape
    nct = D_MODEL // tc
    return pl.pallas_call(
        _conv_kernel,
        grid=(batch, nct),
        in_specs=[
            pl.BlockSpec((seq, d), lambda b, j: (b, 0)),
            pl.BlockSpec((None, d, tc), lambda b, j: (layer, 0, COL_CH // tc + j)),
            pl.BlockSpec((None, d, tc), lambda b, j: (layer, 0, COL_CB // tc + j)),
            pl.BlockSpec((None, d, tc), lambda b, j: (layer, 0, COL_CC // tc + j)),
            pl.BlockSpec((None, CONV_K, tc), lambda b, j: (layer, 0, j)),
        ],
        out_specs=pl.BlockSpec((seq, tc), lambda b, j: (b, j)),
        out_shape=jax.ShapeDtypeStruct((t, D_MODEL), BF16),
        scratch_shapes=[pltpu.VMEM((seq + 16, tc), F32)],
        compiler_params=_params("parallel", "arbitrary"),
        name="conv_branch",
    )(xb, w_in, w_in, w_in, conv_w)


def _bias_kernel(rel_ref, lo_ref, hi_ref, bkt_ref, o_ref, *, rows):
    j = pl.program_id(0)
    h = pl.program_id(1)

    def chunk(c, carry):
        r0 = pl.multiple_of(c * rows, rows)
        lo = lo_ref[c, j]

        @pl.when(lo == hi_ref[c, j])
        def _():
            o_ref[pl.ds(r0, rows), :] = jnp.full((rows, o_ref.shape[1]), rel_ref[lo, h], F32) * LOG2E

        @pl.when(lo != hi_ref[c, j])
        def _():
            bkt = bkt_ref[pl.ds(r0, rows), :]
            acc = jnp.zeros(bkt.shape, F32)
            for b in range(REL_BUCKETS):
                acc = jnp.where(bkt == b, rel_ref[b, h], acc)
            o_ref[pl.ds(r0, rows), :] = acc * LOG2E

        return carry

    lax.fori_loop(0, bkt_ref.shape[0] // rows, chunk, 0)


def _rel_bias_kt(rel_bias, bucket_kt, tq=256, rows=128):
    s = bucket_kt.shape[0]
    blocks = bucket_kt.reshape(s // rows, rows, s // tq, tq)
    smem = pl.BlockSpec(memory_space=pltpu.SMEM)
    return pl.pallas_call(
        functools.partial(_bias_kernel, rows=rows),
        grid=(s // tq, DIFF_HEADS),
        in_specs=[smem, smem, smem, pl.BlockSpec((s, tq), lambda j, h: (0, j))],
        out_specs=pl.BlockSpec((None, s, tq), lambda j, h: (h, 0, j)),
        out_shape=jax.ShapeDtypeStruct((DIFF_HEADS, s, s), F32),
        compiler_params=_params("parallel", "arbitrary"),
        name="rel_bias",
    )(rel_bias, blocks.min(axis=(1, 3)), blocks.max(axis=(1, 3)), bucket_kt)


def _t5_bucket(rel):
    nb = REL_BUCKETS // 2
    max_exact = nb // 2
    ret = (rel > 0).astype(jnp.int32) * nb
    n = jnp.abs(rel)
    nf = jnp.maximum(n, 1).astype(F32)
    large = max_exact + (jnp.log(nf / max_exact) / math.log(REL_MAX_DIST / max_exact)
                         * (nb - max_exact)).astype(jnp.int32)
    large = jnp.minimum(large, nb - 1)
    return ret + jnp.where(n < max_exact, n, large)


def _diff_attn_kernel(lam_ref, qt_ref, k_ref, bias_ref, vt_ref, g_ref, o_ref,
                      s0_ref, s1_ref, m0_ref, m1_ref, e_ref, a0_ref, a1_ref, *, lam_init, kc):
    tq = qt_ref.shape[1]
    s = k_ref.shape[0]
    n = 2 * tq
    t = pl.program_id(0)

    @pl.when(t == 0)
    def _():
        s1_ref[...] = jnp.zeros(s1_ref.shape, F32)
        m1_ref[...] = jnp.zeros(m1_ref.shape, F32)
        a1_ref[...] = jnp.ones(a1_ref.shape, F32)

    def step(sw_ref, mw_ref, sr_ref, mr_ref, aw_ref, ar_ref):
        qt = qt_ref[...]
        row = lax.broadcasted_iota(jnp.int32, qt.shape, 0)
        zero = jnp.zeros_like(qt)
        q12 = jnp.concatenate([jnp.where(row < DIFF_DK, qt, zero), jnp.where(row >= DIFF_DK, qt, zero)], axis=1)
        b = bias_ref[...]
        sc = jnp.dot(k_ref[...], q12, preferred_element_type=F32) + jnp.concatenate([b, b], axis=1)
        sw_ref[...] = sc
        mw_ref[...] = jnp.max(sc.reshape(s // 8, 8, n), axis=0)

        m = jnp.max(mr_ref[...], axis=0, keepdims=True)
        for c in range(s // kc):
            sl = slice(c * kc, (c + 1) * kc)
            e_ref[sl, :] = jnp.exp2(sr_ref[sl, :] - m).astype(BF16)
        orow = lax.broadcasted_iota(jnp.int32, (SUM_ROWS, s), 0)
        ones = jnp.where(orow == 0, 1.0, 0.0).astype(BF16)
        vt = jnp.concatenate([vt_ref[...], ones], axis=0)
        aw_ref[...] = jnp.dot(vt, e_ref[...], preferred_element_type=F32)

        acc = ar_ref[...]
        lf = lam_ref[...]
        lam = (jnp.exp(jnp.sum(lf[0:1, :] * lf[1:2, :], axis=-1, keepdims=True))
               - jnp.exp(jnp.sum(lf[2:3, :] * lf[3:4, :], axis=-1, keepdims=True)) + lam_init)
        r = 1.0 / acc[DIFF_DV:DIFF_DV + 1, :]
        ot = acc[:DIFF_DV, :tq] * r[:, :tq] - acc[:DIFF_DV, tq:] * (lam * r[:, tq:])
        o = ot.T
        ms = jnp.mean(o * o, axis=-1, keepdims=True)
        o_ref[...] = (o * lax.rsqrt(ms + LN_EPS) * g_ref[...] * (1.0 - lam_init)).astype(o_ref.dtype)

    @pl.when(t % 2 == 0)
    def _():
        step(s0_ref, m0_ref, s1_ref, m1_ref, a0_ref, a1_ref)

    @pl.when(t % 2 == 1)
    def _():
        step(s1_ref, m1_ref, s0_ref, m0_ref, a1_ref, a0_ref)


def _diff_attention(qt, k, vt, bias_kt, diff_lambda, subln_g, layer, batch, seq, lam_init, tq=512, kc=256):
    tokens = k.shape[1]
    nqt = seq // tq
    ntile = DIFF_HEADS * nqt * batch

    def dec(tile):
        return tile // (batch * nqt), (tile // batch) % nqt, tile % batch

    cur = lambda t: dec(jnp.minimum(t, ntile - 1))
    prev = lambda t: dec(jnp.clip(t - 1, 0, ntile - 1))
    prev2 = lambda t: dec(jnp.maximum(t - 2, 0))

    def qt_map(t):
        h, i, b = cur(t)
        return (h, b * nqt + i)

    def k_map(t):
        h, i, b = cur(t)
        return (h, b, 0)

    def bias_map(t):
        h, i, b = cur(t)
        return (h, 0, i)

    def vt_map(t):
        h, i, b = prev(t)
        return (h, b)

    def o_map(t):
        h, i, b = prev2(t)
        return (h, b * nqt + i, 0)

    kern = functools.partial(_diff_attn_kernel, lam_init=lam_init, kc=kc)
    return pl.pallas_call(
        kern,
        grid=(ntile + 2,),
        in_specs=[
            pl.BlockSpec((None, 4, DIFF_DK), lambda t: (layer, 0, 0)),
            pl.BlockSpec((DIFF_DV, tq), qt_map),
            pl.BlockSpec((None, seq, DIFF_DV), k_map),
            pl.BlockSpec((None, seq, tq), bias_map),
            pl.BlockSpec((DIFF_DV, seq), vt_map),
            pl.BlockSpec((None, 1, DIFF_DV), lambda t: (layer, 0, 0)),
        ],
        out_specs=pl.BlockSpec((None, tq, DIFF_DV), o_map),
        out_shape=jax.ShapeDtypeStruct((DIFF_HEADS, tokens, DIFF_DV), BF16),
        scratch_shapes=[
            pltpu.VMEM((seq, 2 * tq), F32), pltpu.VMEM((seq, 2 * tq), F32),
            pltpu.VMEM((8, 2 * tq), F32), pltpu.VMEM((8, 2 * tq), F32),
            pltpu.VMEM((seq, 2 * tq), BF16),
            pltpu.VMEM((DIFF_DV + SUM_ROWS, 2 * tq), F32), pltpu.VMEM((DIFF_DV + SUM_ROWS, 2 * tq), F32),
        ],
        compiler_params=_params("arbitrary"),
        name="diff_attn",
    )(diff_lambda, qt, k, bias_kt, vt, subln_g)


def _merge_kernel(x_ref, xb_ref, z_ref, att_ref, mkv_ref,
                  wmq_ref, wga_ref, wgb_ref, wgc_ref, bg_ref,
                  wca_ref, wdo_ref, wmo_ref, wo_ref, g_ref, b_ref,
                  y_ref, yb_ref, o_scr, *, alpha):
    xb = xb_ref[...]
    mq = (jnp.dot(xb, wmq_ref[...], preferred_element_type=F32) * (MEM_DH ** -0.5)).astype(BF16)
    w = MEM_HEADS * MEM_DH
    for h in range(MEM_HEADS):
        hs = slice(h * MEM_DH, (h + 1) * MEM_DH)
        sc = lax.dot_general(mq[:, hs], mkv_ref[:, hs], NT_DIMS, preferred_element_type=F32)
        e = jnp.exp(sc - jnp.max(sc, axis=-1, keepdims=True))
        p = e * (1.0 / jnp.sum(e, axis=-1, keepdims=True))
        o_scr[:, hs] = jnp.dot(p.astype(BF16), mkv_ref[:, w + h * MEM_DH:w + (h + 1) * MEM_DH],
                               preferred_element_type=F32).astype(BF16)
    out_c = jnp.dot(o_scr[...], wmo_ref[...], preferred_element_type=F32)
    out_a = jnp.dot(z_ref[...], wca_ref[...], preferred_element_type=F32)
    att = jnp.concatenate([att_ref[h] for h in range(att_ref.shape[0])], axis=1)
    out_b = jnp.dot(att, wdo_ref[...], preferred_element_type=F32)
    d = D_MODEL

    def gate(wg_ref, j):
        gl = jnp.dot(xb, wg_ref[...], preferred_element_type=F32) + bg_ref[:, j * d:(j + 1) * d]
        return 1.0 / (1.0 + jnp.exp(-gl))

    merged = gate(wga_ref, 0) * out_a + gate(wgb_ref, 1) * out_b + gate(wgc_ref, 2) * out_c
    hmix = jnp.dot(merged.astype(BF16), wo_ref[...], preferred_element_type=F32)
    y = _layer_norm(alpha * x_ref[...] + hmix, g_ref[...], b_ref[...])
    y_ref[...] = y
    yb_ref[...] = y.astype(BF16)


def _merge(x, xb, z, att, mkv, w_in, b_gate, w_conv_out, w_diff_out, w_mem_out, w_o, ln_g, ln_b,
           layer, seq, mem_len, alpha, tm=512):
    t, d = x.shape
    per_b = seq // tm
    row = lambda i: (i, 0)
    wspec = lambda: pl.BlockSpec((None, d, d), lambda i: (layer, 0, 0), pipeline_mode=pl.Buffered(1))
    win = lambda col: pl.BlockSpec((None, d, d), lambda i: (layer, 0, col // d), pipeline_mode=pl.Buffered(1))
    vec = lambda n: pl.BlockSpec((None, 1, n), lambda i: (layer, 0, 0))
    kern = functools.partial(_merge_kernel, alpha=alpha)
    return pl.pallas_call(
        kern,
        grid=(t // tm,),
        in_specs=[
            pl.BlockSpec((tm, d), row),
            pl.BlockSpec((tm, d), row),
            pl.BlockSpec((tm, d), row),
            pl.BlockSpec((DIFF_HEADS, tm, DIFF_DV), lambda i: (0, i, 0)),
            pl.BlockSpec((mem_len, 2 * d), lambda i: (i // per_b, 0)),
            win(COL_MQ), win(COL_GATE), win(COL_GATE + d), win(COL_GATE + 2 * d),
            vec(N_BRANCH * d),
            wspec(), wspec(), wspec(), wspec(),
            vec(d), vec(d),
        ],
        out_specs=[pl.BlockSpec((tm, d), row), pl.BlockSpec((tm, d), row)],
        out_shape=[jax.ShapeDtypeStruct((t, d), F32), jax.ShapeDtypeStruct((t, d), BF16)],
        scratch_shapes=[pltpu.VMEM((tm, d), BF16)],
        compiler_params=_params("parallel"),
        name="merge",
    )(x, xb, z, att, mkv, w_in, w_in, w_in, w_in, b_gate,
      w_conv_out, w_diff_out, w_mem_out, w_o, ln_g, ln_b)


def _mlp_kernel(x_ref, xb_ref, w1_ref, w2_ref, g_ref, b_ref, y_ref, yb_ref, *, alpha, fc):
    xb = xb_ref[...]
    f = jnp.zeros(x_ref.shape, F32)
    for c in range(w1_ref.shape[1] // fc):
        cs = slice(c * fc, (c + 1) * fc)
        hid = jnp.maximum(jnp.dot(xb, w1_ref[:, cs], preferred_element_type=F32), 0.0)
        f = f + jnp.dot((hid * hid).astype(BF16), w2_ref[cs, :], preferred_element_type=F32)
    y = _layer_norm(alpha * x_ref[...] + f, g_ref[...], b_ref[...])
    y_ref[...] = y
    yb_ref[...] = y.astype(BF16)


def _mlp(x, xb, w1, w2, ln_g, ln_b, layer, alpha, tm=512, fc=1024):
    t, d = x.shape
    dff = w1.shape[2]
    row = lambda i: (i, 0)
    vec = lambda n: pl.BlockSpec((None, 1, n), lambda i: (layer, 0, 0))
    kern = functools.partial(_mlp_kernel, alpha=alpha, fc=fc)
    return pl.pallas_call(
        kern,
        grid=(t // tm,),
        in_specs=[
            pl.BlockSpec((tm, d), row),
            pl.BlockSpec((tm, d), row),
            pl.BlockSpec((None, d, dff), lambda i: (layer, 0, 0), pipeline_mode=pl.Buffered(1)),
            pl.BlockSpec((None, dff, d), lambda i: (layer, 0, 0), pipeline_mode=pl.Buffered(1)),
            vec(d), vec(d),
        ],
        out_specs=[pl.BlockSpec((tm, d), row), pl.BlockSpec((tm, d), row)],
        out_shape=[jax.ShapeDtypeStruct((t, d), F32), jax.ShapeDtypeStruct((t, d), BF16)],
        compiler_params=_params("parallel"),
        name="mlp",
    )(x, xb, w1, w2, ln_g, ln_b)


def kernel(x, mem, w_in, b_gate, conv_w, w_conv_out, diff_lambda, subln_g, w_diff_out, rel_bias,
           w_mem_kv, w_mem_out, w_o, ln1_g, ln1_b, w_mlp1, w_mlp2, ln2_g, ln2_b):
    batch, seq, d = x.shape
    mem_len = mem.shape[1]
    depth = w_in.shape[0]
    alpha = (2 * depth) ** 0.25

    w_in_b = w_in.astype(BF16)
    w_conv_out_b = w_conv_out.astype(BF16)
    w_diff_out_b = w_diff_out.astype(BF16)
    w_mem_kv_b = w_mem_kv.astype(BF16)
    w_mem_out_b = w_mem_out.astype(BF16)
    w_o_b = w_o.astype(BF16)
    w_mlp1_b = w_mlp1.astype(BF16)
    w_mlp2_b = w_mlp2.astype(BF16)
    mem_b = mem.reshape(batch * mem_len, d).astype(BF16)
    wq_t = jnp.swapaxes(w_in_b[:, :, COL_Q:COL_K], 1, 2)
    wv_t = jnp.swapaxes(w_in_b[:, :, COL_V:COL_MQ], 1, 2)
    vec3 = lambda a: a.reshape(depth, 1, a.shape[-1])

    pos = jnp.arange(seq, dtype=jnp.int32)
    bucket_kt = _t5_bucket(pos[:, None] - pos[None, :])
    bias_kt = _rel_bias_kt(rel_bias, bucket_kt)

    xf = x.reshape(batch * seq, d)
    xb = xf.astype(BF16)
    for l in range(depth):
        lam_init = 0.8 - 0.6 * math.exp(-0.3 * l)
        qt = _matmul_t(wq_t, xb, l, 1024, 1024, DIFF_DK ** -0.5 * LOG2E, BF16)
        k = _matmul_heads(xb, w_in_b, l, COL_K, DIFF_HEADS, 2 * DIFF_DK, 1024, BF16)
        vt = _matmul_t(wv_t, xb, l, 1024, 1024, 1.0, BF16)
        att = _diff_attention(qt, k, vt, bias_kt, diff_lambda, vec3(subln_g), l, batch, seq, lam_init)
        z = _conv_branch(xb, w_in_b, conv_w, l, batch, seq)
        mkv = _matmul(mem_b, w_mem_kv_b, l, 0, 2 * d, 1024, 1024, BF16)
        xf, xb = _merge(xf, xb, z, att, mkv, w_in_b, vec3(b_gate), w_conv_out_b, w_diff_out_b,
                        w_mem_out_b, w_o_b, vec3(ln1_g), vec3(ln1_b), l, seq, mem_len, alpha)
        xf, xb = _mlp(xf, xb, w_mlp1_b, w_mlp2_b, vec3(ln2_g), vec3(ln2_b), l, alpha)
    return xf.reshape(batch, seq, d)
```

```python
import functools
import math

import jax
import jax.numpy as jnp
from jax import lax
from jax.experimental import pallas as pl
from jax.experimental.pallas import tpu as pltpu

F32 = jnp.float32
BF16 = jnp.bfloat16

D_MODEL = 1024
CONV_K = 3
DIFF_HEADS = 8
DIFF_DK = 64
DIFF_DV = 2 * DIFF_DK
MEM_HEADS = 4
MEM_DH = 256
N_BRANCH = 3
REL_BUCKETS = 32
REL_MAX_DIST = 128
LN_EPS = 1e-5
LOG2E = math.log2(math.e)
SUM_ROWS = 16

COL_CH, COL_CB, COL_CC = 0, D_MODEL, 2 * D_MODEL
COL_Q, COL_K, COL_V = 3 * D_MODEL, 4 * D_MODEL, 5 * D_MODEL
COL_MQ = 6 * D_MODEL
COL_GATE = 7 * D_MODEL

VMEM_LIMIT = 56 * 1024 * 1024

NT_DIMS = (((1,), (1,)), ((), ()))


def _params(*sem):
    return pltpu.CompilerParams(dimension_semantics=sem, vmem_limit_bytes=VMEM_LIMIT)


def _layer_norm(y, g, b):
    mu = jnp.mean(y, axis=-1, keepdims=True)
    yc = y - mu
    var = jnp.mean(yc * yc, axis=-1, keepdims=True)
    return yc * lax.rsqrt(var + LN_EPS) * g + b


def _mm_kernel(x_ref, w_ref, o_ref):
    o_ref[...] = jnp.dot(x_ref[...], w_ref[...], preferred_element_type=F32).astype(o_ref.dtype)


def _matmul(x, w_stack, layer, col0, ncols, tm, tn, out_dtype):
    m, k = x.shape
    cb0 = col0 // tn
    return pl.pallas_call(
        _mm_kernel,
        grid=(m // tm, ncols // tn),
        in_specs=[
            pl.BlockSpec((tm, k), lambda i, j: (i, 0)),
            pl.BlockSpec((None, k, tn), lambda i, j: (layer, 0, cb0 + j)),
        ],
        out_specs=pl.BlockSpec((tm, tn), lambda i, j: (i, j)),
        out_shape=jax.ShapeDtypeStruct((m, ncols), out_dtype),
        compiler_params=_params("parallel", "arbitrary"),
        name="matmul",
    )(x, w_stack)


def _mm_heads_kernel(x_ref, w_ref, o_ref):
    heads, _, dh = o_ref.shape
    res = jnp.dot(x_ref[...], w_ref[...], preferred_element_type=F32).astype(o_ref.dtype)
    for h in range(heads):
        o_ref[h] = res[:, h * dh:(h + 1) * dh]


def _matmul_heads(x, w_stack, layer, col0, heads, dh, tm, out_dtype):
    m, k = x.shape
    n = heads * dh
    return pl.pallas_call(
        _mm_heads_kernel,
        grid=(m // tm,),
        in_specs=[
            pl.BlockSpec((tm, k), lambda i: (i, 0)),
            pl.BlockSpec((None, k, n), lambda i: (layer, 0, col0 // n)),
        ],
        out_specs=pl.BlockSpec((heads, tm, dh), lambda i: (0, i, 0)),
        out_shape=jax.ShapeDtypeStruct((heads, m, dh), out_dtype),
        compiler_params=_params("parallel"),
        name="matmul_heads",
    )(x, w_stack)


def _mm_t_kernel(wt_ref, x_ref, o_ref, *, scale):
    acc = lax.dot_general(wt_ref[...], x_ref[...], NT_DIMS, preferred_element_type=F32)
    o_ref[...] = (acc * scale).astype(o_ref.dtype)


def _matmul_t(wt_stack, x, layer, tn, tm, scale, out_dtype):
    m, k = x.shape
    n = wt_stack.shape[1]
    return pl.pallas_call(
        functools.partial(_mm_t_kernel, scale=scale),
        grid=(m // tm, n // tn),
        in_specs=[
            pl.BlockSpec((None, tn, k), lambda i, j: (layer, j, 0)),
            pl.BlockSpec((tm, k), lambda i, j: (i, 0)),
        ],
        out_specs=pl.BlockSpec((tn, tm), lambda i, j: (j, i)),
        out_shape=jax.ShapeDtypeStruct((n, m), out_dtype),
        compiler_params=_params("parallel", "arbitrary"),
        name="matmul_t",
    )(wt_stack, x)


def _conv_kernel(x_ref, wh_ref, wb_ref, wc_ref, cw_ref, z_ref, u_ref):
    s = x_ref.shape[0]
    x = x_ref[...]
    pad = jnp.zeros((8, u_ref.shape[1]), F32)
    u_ref[0:8, :] = pad
    u_ref[s + 8:s + 16, :] = pad
    u_ref[8:s + 8, :] = (jnp.dot(x, wc_ref[...], preferred_element_type=F32)
                         * jnp.dot(x, wh_ref[...], preferred_element_type=F32))
    cb = jnp.dot(x, wb_ref[...], preferred_element_type=F32)
    cw = cw_ref[...]
    y = (cw[0:1, :] * u_ref[7:s + 7, :] + cw[1:2, :] * u_ref[8:s + 8, :]
         + cw[2:3, :] * u_ref[9:s + 9, :])
    z_ref[...] = (cb * y).astype(z_ref.dtype)


def _conv_branch(xb, w_in, conv_w, layer, batch, seq, tc=512):
    t, d = xb.shape
    nct = D_MODEL // tc
    return pl.pallas_call(
        _conv_kernel,
        grid=(batch, nct),
        in_specs=[
            pl.BlockSpec((seq, d), lambda b, j: (b, 0)),
            pl.BlockSpec((None, d, tc), lambda b, j: (layer, 0, COL_CH // tc + j)),
            pl.BlockSpec((None, d, tc), lambda b, j: (layer, 0, COL_CB // tc + j)),
            pl.BlockSpec((None, d, tc), lambda b, j: (layer, 0, COL_CC // tc + j)),
            pl.BlockSpec((None, CONV_K, tc), lambda b, j: (layer, 0, j)),
        ],
        out_specs=pl.BlockSpec((seq, tc), lambda b, j: (b, j)),
        out_shape=jax.ShapeDtypeStruct((t, D_MODEL), BF16),
        scratch_shapes=[pltpu.VMEM((seq + 16, tc), F32)],
        compiler_params=_params("parallel", "arbitrary"),
        name="conv_branch",
    )(xb, w_in, w_in, w_in, conv_w)


def _bias_kernel(rel_ref, lo_ref, hi_ref, bkt_ref, o_ref, *, rows):
    j = pl.program_id(0)
    h = pl.program_id(1)

    def chunk(c, carry):
        r0 = pl.multiple_of(c * rows, rows)
        lo = lo_ref[c, j]

        @pl.when(lo == hi_ref[c, j])
        def _():
            o_ref[pl.ds(r0, rows), :] = jnp.full((rows, o_ref.shape[1]), rel_ref[lo, h], F32) * LOG2E

        @pl.when(lo != hi_ref[c, j])
        def _():
            bkt = bkt_ref[pl.ds(r0, rows), :]
            acc = jnp.zeros(bkt.shape, F32)
            for b in range(REL_BUCKETS):
                acc = jnp.where(bkt == b, rel_ref[b, h], acc)
            o_ref[pl.ds(r0, rows), :] = acc * LOG2E

        return carry

    lax.fori_loop(0, bkt_ref.shape[0] // rows, chunk, 0)


def _rel_bias_kt(rel_bias, bucket_kt, tq=256, rows=128):
    s = bucket_kt.shape[0]
    blocks = bucket_kt.reshape(s // rows, rows, s // tq, tq)
    smem = pl.BlockSpec(memory_space=pltpu.SMEM)
    return pl.pallas_call(
        functools.partial(_bias_kernel, rows=rows),
        grid=(s // tq, DIFF_HEADS),
        in_specs=[smem, smem, smem, pl.BlockSpec((s, tq), lambda j, h: (0, j))],
        out_specs=pl.BlockSpec((None, s, tq), lambda j, h: (h, 0, j)),
        out_shape=jax.ShapeDtypeStruct((DIFF_HEADS, s, s), F32),
        compiler_params=_params("parallel", "arbitrary"),
        name="rel_bias",
    )(rel_bias, blocks.min(axis=(1, 3)), blocks.max(axis=(1, 3)), bucket_kt)


def _t5_bucket(rel):
    nb = REL_BUCKETS // 2
    max_exact = nb // 2
    ret = (rel > 0).astype(jnp.int32) * nb
    n = jnp.abs(rel)
    nf = jnp.maximum(n, 1).astype(F32)
    large = max_exact + (jnp.log(nf / max_exact) / math.log(REL_MAX_DIST / max_exact)
                         * (nb - max_exact)).astype(jnp.int32)
    large = jnp.minimum(large, nb - 1)
    return ret + jnp.where(n < max_exact, n, large)


def _diff_attn_kernel(lam_ref, qt_ref, k_ref, bias_ref, vt_ref, g_ref, o_ref,
                      s0_ref, s1_ref, m0_ref, m1_ref, e_ref, a0_ref, a1_ref, *, lam_init, kc):
    tq = qt_ref.shape[1]
    s = k_ref.shape[0]
    n = 2 * tq
    t = pl.program_id(0)

    @pl.when(t == 0)
    def _():
        s1_ref[...] = jnp.zeros(s1_ref.shape, F32)
        m1_ref[...] = jnp.zeros(m1_ref.shape, F32)
        a1_ref[...] = jnp.ones(a1_ref.shape, F32)

    def step(sw_ref, mw_ref, sr_ref, mr_ref, aw_ref, ar_ref):
        qt = qt_ref[...]
        row = lax.broadcasted_iota(jnp.int32, qt.shape, 0)
        zero = jnp.zeros_like(qt)
        q12 = jnp.concatenate([jnp.where(row < DIFF_DK, qt, zero), jnp.where(row >= DIFF_DK, qt, zero)], axis=1)
        b = bias_ref[...]
        sc = jnp.dot(k_ref[...], q12, preferred_element_type=F32) + jnp.concatenate([b, b], axis=1)
        sw_ref[...] = sc
        mw_ref[...] = jnp.max(sc.reshape(s // 8, 8, n), axis=0)

        m = jnp.max(mr_ref[...], axis=0, keepdims=True)
        for c in range(s // kc):
            sl = slice(c * kc, (c + 1) * kc)
            e_ref[sl, :] = jnp.exp2(sr_ref[sl, :] - m).astype(BF16)
        orow = lax.broadcasted_iota(jnp.int32, (SUM_ROWS, s), 0)
        ones = jnp.where(orow == 0, 1.0, 0.0).astype(BF16)
        vt = jnp.concatenate([vt_ref[...], ones], axis=0)
        aw_ref[...] = jnp.dot(vt, e_ref[...], preferred_element_type=F32)

        acc = ar_ref[...]
        lf = lam_ref[...]
        lam = (jnp.exp(jnp.sum(lf[0:1, :] * lf[1:2, :], axis=-1, keepdims=True))
               - jnp.exp(jnp.sum(lf[2:3, :] * lf[3:4, :], axis=-1, keepdims=True)) + lam_init)
        r = 1.0 / acc[DIFF_DV:DIFF_DV + 1, :]
        ot = acc[:DIFF_DV, :tq] * r[:, :tq] - acc[:DIFF_DV, tq:] * (lam * r[:, tq:])
        o = ot.T
        ms = jnp.mean(o * o, axis=-1, keepdims=True)
        o_ref[...] = (o * lax.rsqrt(ms + LN_EPS) * g_ref[...] * (1.0 - lam_init)).astype(o_ref.dtype)

    @pl.when(t % 2 == 0)
    def _():
        step(s0_ref, m0_ref, s1_ref, m1_ref, a0_ref, a1_ref)

    @pl.when(t % 2 == 1)
    def _():
        step(s1_ref, m1_ref, s0_ref, m0_ref, a1_ref, a0_ref)


def _diff_attention(qt, k, vt, bias_kt, diff_lambda, subln_g, layer, batch, seq, lam_init, tq=512, kc=256):
    tokens = k.shape[1]
    nqt = seq // tq
    ntile = DIFF_HEADS * nqt * batch

    def dec(tile):
        return tile // (batch * nqt), (tile // batch) % nqt, tile % batch

    cur = lambda t: dec(jnp.minimum(t, ntile - 1))
    prev = lambda t: dec(jnp.clip(t - 1, 0, ntile - 1))
    prev2 = lambda t: dec(jnp.maximum(t - 2, 0))

    def qt_map(t):
        h, i, b = cur(t)
        return (h, b * nqt + i)

    def k_map(t):
        h, i, b = cur(t)
        return (h, b, 0)

    def bias_map(t):
        h, i, b = cur(t)
        return (h, 0, i)

    def vt_map(t):
        h, i, b = prev(t)
        return (h, b)

    def o_map(t):
        h, i, b = prev2(t)
        return (h, b * nqt + i, 0)

    kern = functools.partial(_diff_attn_kernel, lam_init=lam_init, kc=kc)
    return pl.pallas_call(
        kern,
        grid=(ntile + 2,),
        in_specs=[
            pl.BlockSpec((None, 4, DIFF_DK), lambda t: (layer, 0, 0)),
            pl.BlockSpec((DIFF_DV, tq), qt_map),
            pl.BlockSpec((None, seq, DIFF_DV), k_map),
            pl.BlockSpec((None, seq, tq), bias_map),
            pl.BlockSpec((DIFF_DV, seq), vt_map),
            pl.BlockSpec((None, 1, DIFF_DV), lambda t: (layer, 0, 0)),
        ],
        out_specs=pl.BlockSpec((None, tq, DIFF_DV), o_map),
        out_shape=jax.ShapeDtypeStruct((DIFF_HEADS, tokens, DIFF_DV), BF16),
        scratch_shapes=[
            pltpu.VMEM((seq, 2 * tq), F32), pltpu.VMEM((seq, 2 * tq), F32),
            pltpu.VMEM((8, 2 * tq), F32), pltpu.VMEM((8, 2 * tq), F32),
            pltpu.VMEM((seq, 2 * tq), BF16),
            pltpu.VMEM((DIFF_DV + SUM_ROWS, 2 * tq), F32), pltpu.VMEM((DIFF_DV + SUM_ROWS, 2 * tq), F32),
        ],
        compiler_params=_params("arbitrary"),
        name="diff_attn",
    )(diff_lambda, qt, k, bias_kt, vt, subln_g)


def _merge_kernel(x_ref, xb_ref, z_ref, att_ref, mkv_ref,
                  wmq_ref, wga_ref, wgb_ref, wgc_ref, bg_ref,
                  wca_ref, wdo_ref, wmo_ref, wo_ref, g_ref, b_ref,
                  y_ref, yb_ref, o_scr, *, alpha):
    xb = xb_ref[...]
    mq = (jnp.dot(xb, wmq_ref[...], preferred_element_type=F32) * (MEM_DH ** -0.5)).astype(BF16)
    w = MEM_HEADS * MEM_DH
    for h in range(MEM_HEADS):
        hs = slice(h * MEM_DH, (h + 1) * MEM_DH)
        sc = lax.dot_general(mq[:, hs], mkv_ref[:, hs], NT_DIMS, preferred_element_type=F32)
        e = jnp.exp(sc - jnp.max(sc, axis=-1, keepdims=True))
        p = e * (1.0 / jnp.sum(e, axis=-1, keepdims=True))
        o_scr[:, hs] = jnp.dot(p.astype(BF16), mkv_ref[:, w + h * MEM_DH:w + (h + 1) * MEM_DH],
                               preferred_element_type=F32).astype(BF16)
    out_c = jnp.dot(o_scr[...], wmo_ref[...], preferred_element_type=F32)
    out_a = jnp.dot(z_ref[...], wca_ref[...], preferred_element_type=F32)
    att = jnp.concatenate([att_ref[h] for h in range(att_ref.shape[0])], axis=1)
    out_b = jnp.dot(att, wdo_ref[...], preferred_element_type=F32)
    d = D_MODEL

    def gate(wg_ref, j):
        gl = jnp.dot(xb, wg_ref[...], preferred_element_type=F32) + bg_ref[:, j * d:(j + 1) * d]
        return 1.0 / (1.0 + jnp.exp(-gl))

    merged = gate(wga_ref, 0) * out_a + gate(wgb_ref, 1) * out_b + gate(wgc_ref, 2) * out_c
    hmix = jnp.dot(merged.astype(BF16), wo_ref[...], preferred_element_type=F32)
    y = _layer_norm(alpha * x_ref[...] + hmix, g_ref[...], b_ref[...])
    y_ref[...] = y
    yb_ref[...] = y.astype(BF16)


def _merge(x, xb, z, att, mkv, w_in, b_gate, w_conv_out, w_diff_out, w_mem_out, w_o, ln_g, ln_b,
           layer, seq, mem_len, alpha, tm=512):
    t, d = x.shape
    per_b = seq // tm
    row = lambda i: (i, 0)
    wspec = lambda: pl.BlockSpec((None, d, d), lambda i: (layer, 0, 0), pipeline_mode=pl.Buffered(1))
    win = lambda col: pl.BlockSpec((None, d, d), lambda i: (layer, 0, col // d), pipeline_mode=pl.Buffered(1))
    vec = lambda n: pl.BlockSpec((None, 1, n), lambda i: (layer, 0, 0))
    kern = functools.partial(_merge_kernel, alpha=alpha)
    return pl.pallas_call(
        kern,
        grid=(t // tm,),
        in_specs=[
            pl.BlockSpec((tm, d), row),
            pl.BlockSpec((tm, d), row),
            pl.BlockSpec((tm, d), row),
            pl.BlockSpec((DIFF_HEADS, tm, DIFF_DV), lambda i: (0, i, 0)),
            pl.BlockSpec((mem_len, 2 * d), lambda i: (i // per_b, 0)),
            win(COL_MQ), win(COL_GATE), win(COL_GATE + d), win(COL_GATE + 2 * d),
            vec(N_BRANCH * d),
            wspec(), wspec(), wspec(), wspec(),
            vec(d), vec(d),
        ],
        out_specs=[pl.BlockSpec((tm, d), row), pl.BlockSpec((tm, d), row)],
        out_shape=[jax.ShapeDtypeStruct((t, d), F32), jax.ShapeDtypeStruct((t, d), BF16)],
        scratch_shapes=[pltpu.VMEM((tm, d), BF16)],
        compiler_params=_params("parallel"),
        name="merge",
    )(x, xb, z, att, mkv, w_in, w_in, w_in, w_in, b_gate,
      w_conv_out, w_diff_out, w_mem_out, w_o, ln_g, ln_b)


def _mlp_kernel(x_ref, xb_ref, w1_ref, w2_ref, g_ref, b_ref, y_ref, yb_ref, *, alpha, fc):
    xb = xb_ref[...]
    f = jnp.zeros(x_ref.shape, F32)
    for c in range(w1_ref.shape[1] // fc):
        cs = slice(c * fc, (c + 1) * fc)
        hid = jnp.maximum(jnp.dot(xb, w1_ref[:, cs], preferred_element_type=F32), 0.0)
        f = f + jnp.dot((hid * hid).astype(BF16), w2_ref[cs, :], preferred_element_type=F32)
    y = _layer_norm(alpha * x_ref[...] + f, g_ref[...], b_ref[...])
    y_ref[...] = y
    yb_ref[...] = y.astype(BF16)


def _mlp(x, xb, w1, w2, ln_g, ln_b, layer, alpha, tm=1024, fc=1024):
    t, d = x.shape
    dff = w1.shape[2]
    row = lambda i: (i, 0)
    vec = lambda n: pl.BlockSpec((None, 1, n), lambda i: (layer, 0, 0))
    kern = functools.partial(_mlp_kernel, alpha=alpha, fc=fc)
    return pl.pallas_call(
        kern,
        grid=(t // tm,),
        in_specs=[
            pl.BlockSpec((tm, d), row),
            pl.BlockSpec((tm, d), row),
            pl.BlockSpec((None, d, dff), lambda i: (layer, 0, 0), pipeline_mode=pl.Buffered(1)),
            pl.BlockSpec((None, dff, d), lambda i: (layer, 0, 0), pipeline_mode=pl.Buffered(1)),
            vec(d), vec(d),
        ],
        out_specs=[pl.BlockSpec((tm, d), row), pl.BlockSpec((tm, d), row)],
        out_shape=[jax.ShapeDtypeStruct((t, d), F32), jax.ShapeDtypeStruct((t, d), BF16)],
        compiler_params=_params("parallel"),
        name="mlp",
    )(x, xb, w1, w2, ln_g, ln_b)


def kernel(x, mem, w_in, b_gate, conv_w, w_conv_out, diff_lambda, subln_g, w_diff_out, rel_bias,
           w_mem_kv, w_mem_out, w_o, ln1_g, ln1_b, w_mlp1, w_mlp2, ln2_g, ln2_b):
    batch, seq, d = x.shape
    mem_len = mem.shape[1]
    depth = w_in.shape[0]
    alpha = (2 * depth) ** 0.25

    w_in_b = w_in.astype(BF16)
    w_conv_out_b = w_conv_out.astype(BF16)
    w_diff_out_b = w_diff_out.astype(BF16)
    w_mem_kv_b = w_mem_kv.astype(BF16)
    w_mem_out_b = w_mem_out.astype(BF16)
    w_o_b = w_o.astype(BF16)
    w_mlp1_b = w_mlp1.astype(BF16)
    w_mlp2_b = w_mlp2.astype(BF16)
    mem_b = mem.reshape(batch * mem_len, d).astype(BF16)
    wq_t = jnp.swapaxes(w_in_b[:, :, COL_Q:COL_K], 1, 2)
    wv_t = jnp.swapaxes(w_in_b[:, :, COL_V:COL_MQ], 1, 2)
    vec3 = lambda a: a.reshape(depth, 1, a.shape[-1])

    pos = jnp.arange(seq, dtype=jnp.int32)
    bucket_kt = _t5_bucket(pos[:, None] - pos[None, :])
    bias_kt = _rel_bias_kt(rel_bias, bucket_kt)

    xf = x.reshape(batch * seq, d)
    xb = xf.astype(BF16)
    for l in range(depth):
        lam_init = 0.8 - 0.6 * math.exp(-0.3 * l)
        qt = _matmul_t(wq_t, xb, l, 1024, seq, DIFF_DK ** -0.5 * LOG2E, BF16)
        k = _matmul_heads(xb, w_in_b, l, COL_K, DIFF_HEADS, 2 * DIFF_DK, seq, BF16)
        vt = _matmul_t(wv_t, xb, l, 1024, seq, 1.0, BF16)
        att = _diff_attention(qt, k, vt, bias_kt, diff_lambda, vec3(subln_g), l, batch, seq, lam_init)
        z = _conv_branch(xb, w_in_b, conv_w, l, batch, seq)
        mkv = _matmul(mem_b, w_mem_kv_b, l, 0, 2 * d, 1024, 1024, BF16)
        xf, xb = _merge(xf, xb, z, att, mkv, w_in_b, vec3(b_gate), w_conv_out_b, w_diff_out_b,
                        w_mem_out_b, w_o_b, vec3(ln1_g), vec3(ln1_b), l, seq, mem_len, alpha)
        xf, xb = _mlp(xf, xb, w_mlp1_b, w_mlp2_b, vec3(ln2_g), vec3(ln2_b), l, alpha)
    return xf.reshape(batch, seq, d)
```

```python
import functools
import math

import jax
import jax.numpy as jnp
from jax import lax
from jax.experimental import pallas as pl
from jax.experimental.pallas import tpu as pltpu

F32 = jnp.float32
BF16 = jnp.bfloat16

D_MODEL = 1024
CONV_K = 3
DIFF_HEADS = 8
DIFF_DK = 64
DIFF_DV = 2 * DIFF_DK
MEM_HEADS = 4
MEM_DH = 256
N_BRANCH = 3
REL_BUCKETS = 32
REL_MAX_DIST = 128
LN_EPS = 1e-5
LOG2E = math.log2(math.e)
SUM_ROWS = 16

COL_CH, COL_CB, COL_CC = 0, D_MODEL, 2 * D_MODEL
COL_Q, COL_K, COL_V = 3 * D_MODEL, 4 * D_MODEL, 5 * D_MODEL
COL_MQ = 6 * D_MODEL
COL_GATE = 7 * D_MODEL

VMEM_LIMIT = 56 * 1024 * 1024

NT_DIMS = (((1,), (1,)), ((), ()))


def _params(*sem):
    return pltpu.CompilerParams(dimension_semantics=sem, vmem_limit_bytes=VMEM_LIMIT)


def _layer_norm(y, g, b):
    mu = jnp.mean(y, axis=-1, keepdims=True)
    yc = y - mu
    var = jnp.mean(yc * yc, axis=-1, keepdims=True)
    return yc * lax.rsqrt(var + LN_EPS) * g + b


def _mm_kernel(x_ref, w_ref, o_ref):
    o_ref[...] = jnp.dot(x_ref[...], w_ref[...], preferred_element_type=F32).astype(o_ref.dtype)


def _matmul(x, w_stack, layer, col0, ncols, tm, tn, out_dtype):
    m, k = x.shape
    cb0 = col0 // tn
    return pl.pallas_call(
        _mm_kernel,
        grid=(m // tm, ncols // tn),
        in_specs=[
            pl.BlockSpec((tm, k), lambda i, j: (i, 0)),
            pl.BlockSpec((None, k, tn), lambda i, j: (layer, 0, cb0 + j)),
        ],
        out_specs=pl.BlockSpec((tm, tn), lambda i, j: (i, j)),
        out_shape=jax.ShapeDtypeStruct((m, ncols), out_dtype),
        compiler_params=_params("parallel", "arbitrary"),
        name="matmul",
    )(x, w_stack)


def _mm_heads_kernel(x_ref, w_ref, o_ref):
    heads, _, dh = o_ref.shape
    res = jnp.dot(x_ref[...], w_ref[...], preferred_element_type=F32).astype(o_ref.dtype)
    for h in range(heads):
        o_ref[h] = res[:, h * dh:(h + 1) * dh]


def _matmul_heads(x, w_stack, layer, col0, heads, dh, tm, out_dtype):
    m, k = x.shape
    n = heads * dh
    return pl.pallas_call(
        _mm_heads_kernel,
        grid=(m // tm,),
        in_specs=[
            pl.BlockSpec((tm, k), lambda i: (i, 0)),
            pl.BlockSpec((None, k, n), lambda i: (layer, 0, col0 // n)),
        ],
        out_specs=pl.BlockSpec((heads, tm, dh), lambda i: (0, i, 0)),
        out_shape=jax.ShapeDtypeStruct((heads, m, dh), out_dtype),
        compiler_params=_params("parallel"),
        name="matmul_heads",
    )(x, w_stack)


def _mm_t_heads_kernel(wt_ref, x_ref, o_ref, *, scale):
    heads, dh, _ = o_ref.shape
    acc = lax.dot_general(wt_ref[...], x_ref[...], NT_DIMS, preferred_element_type=F32)
    res = (acc * scale).astype(o_ref.dtype)
    for h in range(heads):
        o_ref[h] = res[h * dh:(h + 1) * dh, :]


def _matmul_t_heads(wt_stack, x, layer, heads, dh, batch, seq, scale, out_dtype):
    _, k = x.shape
    return pl.pallas_call(
        functools.partial(_mm_t_heads_kernel, scale=scale),
        grid=(batch,),
        in_specs=[
            pl.BlockSpec((None, heads * dh, k), lambda b: (layer, 0, 0)),
            pl.BlockSpec((seq, k), lambda b: (b, 0)),
        ],
        out_specs=pl.BlockSpec((heads, None, dh, seq), lambda b: (0, b, 0, 0)),
        out_shape=jax.ShapeDtypeStruct((heads, batch, dh, seq), out_dtype),
        compiler_params=_params("parallel"),
        name="matmul_t_heads",
    )(wt_stack, x)


def _conv_kernel(x_ref, wh_ref, wb_ref, wc_ref, cw_ref, z_ref, u_ref):
    s = x_ref.shape[0]
    x = x_ref[...]
    pad = jnp.zeros((8, u_ref.shape[1]), F32)
    u_ref[0:8, :] = pad
    u_ref[s + 8:s + 16, :] = pad
    u_ref[8:s + 8, :] = (jnp.dot(x, wc_ref[...], preferred_element_type=F32)
                         * jnp.dot(x, wh_ref[...], preferred_element_type=F32))
    cb = jnp.dot(x, wb_ref[...], preferred_element_type=F32)
    cw = cw_ref[...]
    y = (cw[0:1, :] * u_ref[7:s + 7, :] + cw[1:2, :] * u_ref[8:s + 8, :]
         + cw[2:3, :] * u_ref[9:s + 9, :])
    z_ref[...] = (cb * y).astype(z_ref.dtype)


def _conv_branch(xb, w_in, conv_w, layer, batch, seq, tc=512):
    t, d = xb.shape
    nct = D_MODEL // tc
    return pl.pallas_call(
        _conv_kernel,
        grid=(batch, nct),
        in_specs=[
            pl.BlockSpec((seq, d), lambda b, j: (b, 0)),
            pl.BlockSpec((None, d, tc), lambda b, j: (layer, 0, COL_CH // tc + j)),
            pl.BlockSpec((None, d, tc), lambda b, j: (layer, 0, COL_CB // tc + j)),
            pl.BlockSpec((None, d, tc), lambda b, j: (layer, 0, COL_CC // tc + j)),
            pl.BlockSpec((None, CONV_K, tc), lambda b, j: (layer, 0, j)),
        ],
        out_specs=pl.BlockSpec((seq, tc), lambda b, j: (b, j)),
        out_shape=jax.ShapeDtypeStruct((t, D_MODEL), BF16),
        scratch_shapes=[pltpu.VMEM((seq + 16, tc), F32)],
        compiler_params=_params("parallel", "arbitrary"),
        name="conv_branch",
    )(xb, w_in, w_in, w_in, conv_w)


def _bias_kernel(rel_ref, lo_ref, hi_ref, bkt_ref, o_ref, *, rows):
    j = pl.program_id(0)
    h = pl.program_id(1)

    def chunk(c, carry):
        r0 = pl.multiple_of(c * rows, rows)
        lo = lo_ref[c, j]

        @pl.when(lo == hi_ref[c, j])
        def _():
            o_ref[pl.ds(r0, rows), :] = jnp.full((rows, o_ref.shape[1]), rel_ref[lo, h], F32) * LOG2E

        @pl.when(lo != hi_ref[c, j])
        def _():
            bkt = bkt_ref[pl.ds(r0, rows), :]
            acc = jnp.zeros(bkt.shape, F32)
            for b in range(REL_BUCKETS):
                acc = jnp.where(bkt == b, rel_ref[b, h], acc)
            o_ref[pl.ds(r0, rows), :] = acc * LOG2E

        return carry

    lax.fori_loop(0, bkt_ref.shape[0] // rows, chunk, 0)


def _rel_bias_kt(rel_bias, bucket_kt, tq=256, rows=128):
    s = bucket_kt.shape[0]
    blocks = bucket_kt.reshape(s // rows, rows, s // tq, tq)
    smem = pl.BlockSpec(memory_space=pltpu.SMEM)
    return pl.pallas_call(
        functools.partial(_bias_kernel, rows=rows),
        grid=(s // tq, DIFF_HEADS),
        in_specs=[smem, smem, smem, pl.BlockSpec((s, tq), lambda j, h: (0, j))],
        out_specs=pl.BlockSpec((None, s, tq), lambda j, h: (h, 0, j)),
        out_shape=jax.ShapeDtypeStruct((DIFF_HEADS, s, s), F32),
        compiler_params=_params("parallel", "arbitrary"),
        name="rel_bias",
    )(rel_bias, blocks.min(axis=(1, 3)), blocks.max(axis=(1, 3)), bucket_kt)


def _t5_bucket(rel):
    nb = REL_BUCKETS // 2
    max_exact = nb // 2
    ret = (rel > 0).astype(jnp.int32) * nb
    n = jnp.abs(rel)
    nf = jnp.maximum(n, 1).astype(F32)
    large = max_exact + (jnp.log(nf / max_exact) / math.log(REL_MAX_DIST / max_exact)
                         * (nb - max_exact)).astype(jnp.int32)
    large = jnp.minimum(large, nb - 1)
    return ret + jnp.where(n < max_exact, n, large)


def _diff_attn_kernel(lam_ref, qt_ref, k_ref, bias_ref, vt_ref, vtp_ref, g_ref, o_ref,
                      s0_ref, s1_ref, m0_ref, m1_ref, e_ref, a0_ref, a1_ref, fin_ref, *, lam_init, kc):
    group, _, tq = qt_ref.shape
    s = k_ref.shape[1]
    n = 2 * tq
    step = pl.program_id(0)

    @pl.when(step == 0)
    def _():
        s1_ref[...] = jnp.zeros(s1_ref.shape, F32)
        m1_ref[...] = jnp.zeros(m1_ref.shape, F32)
        a0_ref[...] = jnp.ones(a0_ref.shape, F32)
        fin_ref[...] = jnp.zeros(fin_ref.shape, BF16)

    for j in range(group - 2):
        o_ref[j] = fin_ref[j]

    def sub_tile(j):
        sw_ref, mw_ref, sr_ref, mr_ref = (s0_ref, m0_ref, s1_ref, m1_ref) if j % 2 == 0 else (s1_ref, m1_ref, s0_ref, m0_ref)
        aw_ref, ar_ref = (a1_ref, a0_ref) if j % 2 == 0 else (a0_ref, a1_ref)

        qt = qt_ref[j]
        row = lax.broadcasted_iota(jnp.int32, qt.shape, 0)
        zero = jnp.zeros_like(qt)
        q12 = jnp.concatenate([jnp.where(row < DIFF_DK, qt, zero), jnp.where(row >= DIFF_DK, qt, zero)], axis=1)
        b = bias_ref[...]
        sc = jnp.dot(k_ref[j], q12, preferred_element_type=F32) + jnp.concatenate([b, b], axis=1)
        sw_ref[...] = sc
        mw_ref[...] = jnp.max(sc.reshape(s // 8, 8, n), axis=0)

        m = jnp.max(mr_ref[...], axis=0, keepdims=True)
        for c in range(s // kc):
            sl = slice(c * kc, (c + 1) * kc)
            e_ref[sl, :] = jnp.exp2(sr_ref[sl, :] - m).astype(BF16)
        orow = lax.broadcasted_iota(jnp.int32, (SUM_ROWS, s), 0)
        ones = jnp.where(orow == 0, 1.0, 0.0).astype(BF16)
        vt_prev = vtp_ref[...] if j == 0 else vt_ref[j - 1]
        vt = jnp.concatenate([vt_prev, ones], axis=0)
        aw_ref[...] = jnp.dot(vt, e_ref[...], preferred_element_type=F32)

        acc = ar_ref[...]
        lf = lam_ref[...]
        lam = (jnp.exp(jnp.sum(lf[0:1, :] * lf[1:2, :], axis=-1, keepdims=True))
               - jnp.exp(jnp.sum(lf[2:3, :] * lf[3:4, :], axis=-1, keepdims=True)) + lam_init)
        r = 1.0 / acc[DIFF_DV:DIFF_DV + 1, :]
        ot = acc[:DIFF_DV, :tq] * r[:, :tq] - acc[:DIFF_DV, tq:] * (lam * r[:, tq:])
        o = ot.T
        ms = jnp.mean(o * o, axis=-1, keepdims=True)
        res = (o * lax.rsqrt(ms + LN_EPS) * g_ref[...] * (1.0 - lam_init)).astype(o_ref.dtype)
        if j < 2:
            o_ref[group - 2 + j] = res
        else:
            fin_ref[j - 2] = res

    for j in range(group):
        pl.when(step >= j - group)(functools.partial(sub_tile, j))


def _diff_attention(qt, k, vt, bias_kt, diff_lambda, subln_g, layer, lam_init, tq=512, kc=256, group=4):
    heads, batch, seq, _ = k.shape
    nqt = seq // tq
    nbg = batch // group
    nreal = heads * nqt * nbg

    def dec(gs):
        return gs // (nqt * nbg), (gs // nbg) % nqt, gs % nbg

    cur = lambda gs: dec(jnp.minimum(gs, nreal - 1))
    out = lambda gs: dec(jnp.maximum(gs - 1, 0))

    def qt_map(gs):
        h, i, bg = cur(gs)
        return (h, bg, 0, i)

    def kv_map(gs):
        h, i, bg = cur(gs)
        return (h, bg, 0, 0)

    def bias_map(gs):
        h, i, bg = cur(gs)
        return (h, 0, i)

    def vtp_map(gs):
        h, i, bg = dec(jnp.maximum(gs - 1, 0))
        return (h, bg * group + group - 1, 0, 0)

    def o_map(gs):
        h, i, bg = out(gs)
        return (h, bg, i, 0)

    kern = functools.partial(_diff_attn_kernel, lam_init=lam_init, kc=kc)
    return pl.pallas_call(
        kern,
        grid=(nreal + 1,),
        in_specs=[
            pl.BlockSpec((None, 4, DIFF_DK), lambda gs: (layer, 0, 0)),
            pl.BlockSpec((None, group, DIFF_DV, tq), qt_map),
            pl.BlockSpec((None, group, seq, DIFF_DV), kv_map),
            pl.BlockSpec((None, seq, tq), bias_map),
            pl.BlockSpec((None, group, DIFF_DV, seq), kv_map),
            pl.BlockSpec((None, None, DIFF_DV, seq), vtp_map),
            pl.BlockSpec((None, 1, DIFF_DV), lambda gs: (layer, 0, 0)),
        ],
        out_specs=pl.BlockSpec((None, group, tq, DIFF_DV), o_map),
        out_shape=jax.ShapeDtypeStruct((heads, batch, seq, DIFF_DV), BF16),
        scratch_shapes=[
            pltpu.VMEM((seq, 2 * tq), F32), pltpu.VMEM((seq, 2 * tq), F32),
            pltpu.VMEM((8, 2 * tq), F32), pltpu.VMEM((8, 2 * tq), F32),
            pltpu.VMEM((seq, 2 * tq), BF16),
            pltpu.VMEM((DIFF_DV + SUM_ROWS, 2 * tq), F32), pltpu.VMEM((DIFF_DV + SUM_ROWS, 2 * tq), F32),
            pltpu.VMEM((group - 2, tq, DIFF_DV), BF16),
        ],
        compiler_params=_params("arbitrary"),
        name="diff_attn",
    )(diff_lambda, qt, k, bias_kt, vt, vt, subln_g)


def _merge_kernel(x_ref, xb_ref, z_ref, att_ref, mkv_ref,
                  wmq_ref, wga_ref, wgb_ref, wgc_ref, bg_ref,
                  wca_ref, wdo_ref, wmo_ref, wo_ref, g_ref, b_ref,
                  y_ref, yb_ref, o_scr, *, alpha):
    xb = xb_ref[...]
    mq = (jnp.dot(xb, wmq_ref[...], preferred_element_type=F32) * (MEM_DH ** -0.5)).astype(BF16)
    w = MEM_HEADS * MEM_DH
    for h in range(MEM_HEADS):
        hs = slice(h * MEM_DH, (h + 1) * MEM_DH)
        sc = lax.dot_general(mq[:, hs], mkv_ref[:, hs], NT_DIMS, preferred_element_type=F32)
        e = jnp.exp(sc - jnp.max(sc, axis=-1, keepdims=True))
        p = e * (1.0 / jnp.sum(e, axis=-1, keepdims=True))
        o_scr[:, hs] = jnp.dot(p.astype(BF16), mkv_ref[:, w + h * MEM_DH:w + (h + 1) * MEM_DH],
                               preferred_element_type=F32).astype(BF16)
    out_c = jnp.dot(o_scr[...], wmo_ref[...], preferred_element_type=F32)
    out_a = jnp.dot(z_ref[...], wca_ref[...], preferred_element_type=F32)
    att = jnp.concatenate([att_ref[h] for h in range(att_ref.shape[0])], axis=1)
    out_b = jnp.dot(att, wdo_ref[...], preferred_element_type=F32)
    d = D_MODEL

    def gate(wg_ref, j):
        gl = jnp.dot(xb, wg_ref[...], preferred_element_type=F32) + bg_ref[:, j * d:(j + 1) * d]
        return 1.0 / (1.0 + jnp.exp(-gl))

    merged = gate(wga_ref, 0) * out_a + gate(wgb_ref, 1) * out_b + gate(wgc_ref, 2) * out_c
    hmix = jnp.dot(merged.astype(BF16), wo_ref[...], preferred_element_type=F32)
    y = _layer_norm(alpha * x_ref[...] + hmix, g_ref[...], b_ref[...])
    y_ref[...] = y
    yb_ref[...] = y.astype(BF16)


def _merge(x, xb, z, att, mkv, w_in, b_gate, w_conv_out, w_diff_out, w_mem_out, w_o, ln_g, ln_b,
           layer, seq, mem_len, alpha, tm=512):
    t, d = x.shape
    per_b = seq // tm
    row = lambda i: (i, 0)
    wspec = lambda: pl.BlockSpec((None, d, d), lambda i: (layer, 0, 0), pipeline_mode=pl.Buffered(1))
    win = lambda col: pl.BlockSpec((None, d, d), lambda i: (layer, 0, col // d), pipeline_mode=pl.Buffered(1))
    vec = lambda n: pl.BlockSpec((None, 1, n), lambda i: (layer, 0, 0))
    kern = functools.partial(_merge_kernel, alpha=alpha)
    return pl.pallas_call(
        kern,
        grid=(t // tm,),
        in_specs=[
            pl.BlockSpec((tm, d), row),
            pl.BlockSpec((tm, d), row),
            pl.BlockSpec((tm, d), row),
            pl.BlockSpec((DIFF_HEADS, tm, DIFF_DV), lambda i: (0, i, 0)),
            pl.BlockSpec((mem_len, 2 * d), lambda i: (i // per_b, 0)),
            win(COL_MQ), win(COL_GATE), win(COL_GATE + d), win(COL_GATE + 2 * d),
            vec(N_BRANCH * d),
            wspec(), wspec(), wspec(), wspec(),
            vec(d), vec(d),
        ],
        out_specs=[pl.BlockSpec((tm, d), row), pl.BlockSpec((tm, d), row)],
        out_shape=[jax.ShapeDtypeStruct((t, d), F32), jax.ShapeDtypeStruct((t, d), BF16)],
        scratch_shapes=[pltpu.VMEM((tm, d), BF16)],
        compiler_params=_params("parallel"),
        name="merge",
    )(x, xb, z, att, mkv, w_in, w_in, w_in, w_in, b_gate,
      w_conv_out, w_diff_out, w_mem_out, w_o, ln_g, ln_b)


def _mlp_kernel(x_ref, xb_ref, w1_ref, w2_ref, g_ref, b_ref, y_ref, yb_ref, *, alpha, fc):
    xb = xb_ref[...]
    f = jnp.zeros(x_ref.shape, F32)
    for c in range(w1_ref.shape[1] // fc):
        cs = slice(c * fc, (c + 1) * fc)
        hid = jnp.maximum(jnp.dot(xb, w1_ref[:, cs], preferred_element_type=F32), 0.0)
        f = f + jnp.dot((hid * hid).astype(BF16), w2_ref[cs, :], preferred_element_type=F32)
    y = _layer_norm(alpha * x_ref[...] + f, g_ref[...], b_ref[...])
    y_ref[...] = y
    yb_ref[...] = y.astype(BF16)


def _mlp(x, xb, w1, w2, ln_g, ln_b, layer, alpha, tm=1024, fc=1024):
    t, d = x.shape
    dff = w1.shape[2]
    row = lambda i: (i, 0)
    vec = lambda n: pl.BlockSpec((None, 1, n), lambda i: (layer, 0, 0))
    kern = functools.partial(_mlp_kernel, alpha=alpha, fc=fc)
    return pl.pallas_call(
        kern,
        grid=(t // tm,),
        in_specs=[
            pl.BlockSpec((tm, d), row),
            pl.BlockSpec((tm, d), row),
            pl.BlockSpec((None, d, dff), lambda i: (layer, 0, 0), pipeline_mode=pl.Buffered(1)),
            pl.BlockSpec((None, dff, d), lambda i: (layer, 0, 0), pipeline_mode=pl.Buffered(1)),
            vec(d), vec(d),
        ],
        out_specs=[pl.BlockSpec((tm, d), row), pl.BlockSpec((tm, d), row)],
        out_shape=[jax.ShapeDtypeStruct((t, d), F32), jax.ShapeDtypeStruct((t, d), BF16)],
        compiler_params=_params("parallel"),
        name="mlp",
    )(x, xb, w1, w2, ln_g, ln_b)


def kernel(x, mem, w_in, b_gate, conv_w, w_conv_out, diff_lambda, subln_g, w_diff_out, rel_bias,
           w_mem_kv, w_mem_out, w_o, ln1_g, ln1_b, w_mlp1, w_mlp2, ln2_g, ln2_b):
    batch, seq, d = x.shape
    mem_len = mem.shape[1]
    depth = w_in.shape[0]
    alpha = (2 * depth) ** 0.25

    w_in_b = w_in.astype(BF16)
    w_conv_out_b = w_conv_out.astype(BF16)
    w_diff_out_b = w_diff_out.astype(BF16)
    w_mem_kv_b = w_mem_kv.astype(BF16)
    w_mem_out_b = w_mem_out.astype(BF16)
    w_o_b = w_o.astype(BF16)
    w_mlp1_b = w_mlp1.astype(BF16)
    w_mlp2_b = w_mlp2.astype(BF16)
    mem_b = mem.reshape(batch * mem_len, d).astype(BF16)
    wq_t = jnp.swapaxes(w_in_b[:, :, COL_Q:COL_K], 1, 2)
    wv_t = jnp.swapaxes(w_in_b[:, :, COL_V:COL_MQ], 1, 2)
    vec3 = lambda a: a.reshape(depth, 1, a.shape[-1])

    pos = jnp.arange(seq, dtype=jnp.int32)
    bucket_kt = _t5_bucket(pos[:, None] - pos[None, :])
    bias_kt = _rel_bias_kt(rel_bias, bucket_kt)

    xf = x.reshape(batch * seq, d)
    xb = xf.astype(BF16)
    for l in range(depth):
        lam_init = 0.8 - 0.6 * math.exp(-0.3 * l)
        qt = _matmul_t_heads(wq_t, xb, l, DIFF_HEADS, 2 * DIFF_DK, batch, seq,
                             DIFF_DK ** -0.5 * LOG2E, BF16)
        k = _matmul_heads(xb, w_in_b, l, COL_K, DIFF_HEADS, 2 * DIFF_DK, seq, BF16)
        vt = _matmul_t_heads(wv_t, xb, l, DIFF_HEADS, DIFF_DV, batch, seq, 1.0, BF16)
        att = _diff_attention(qt, k.reshape(DIFF_HEADS, batch, seq, 2 * DIFF_DK), vt, bias_kt, diff_lambda,
                              vec3(subln_g), l, lam_init).reshape(DIFF_HEADS, batch * seq, DIFF_DV)
        z = _conv_branch(xb, w_in_b, conv_w, l, batch, seq)
        mkv = _matmul(mem_b, w_mem_kv_b, l, 0, 2 * d, 1024, 1024, BF16)
        xf, xb = _merge(xf, xb, z, att, mkv, w_in_b, vec3(b_gate), w_conv_out_b, w_diff_out_b,
                        w_mem_out_b, w_o_b, vec3(ln1_g), vec3(ln1_b), l, seq, mem_len, alpha)
        xf, xb = _mlp(xf, xb, w_mlp1_b, w_mlp2_b, vec3(ln2_g), vec3(ln2_b), l, alpha)
    return xf.reshape(batch, seq, d)
```

```python
import functools
import math

import jax
import jax.numpy as jnp
from jax import lax
from jax.experimental import pallas as pl
from jax.experimental.pallas import tpu as pltpu

F32 = jnp.float32
BF16 = jnp.bfloat16

D_MODEL = 1024
CONV_K = 3
DIFF_HEADS = 8
DIFF_DK = 64
DIFF_DV = 2 * DIFF_DK
MEM_HEADS = 4
MEM_DH = 256
N_BRANCH = 3
REL_BUCKETS = 32
REL_MAX_DIST = 128
LN_EPS = 1e-5
LOG2E = math.log2(math.e)
SUM_ROWS = 16

COL_CH, COL_CB, COL_CC = 0, D_MODEL, 2 * D_MODEL
COL_Q, COL_K, COL_V = 3 * D_MODEL, 4 * D_MODEL, 5 * D_MODEL
COL_MQ = 6 * D_MODEL
COL_GATE = 7 * D_MODEL

VMEM_LIMIT = 56 * 1024 * 1024

NT_DIMS = (((1,), (1,)), ((), ()))
TN_DIMS = (((0,), (1,)), ((), ()))


def _params(*sem):
    return pltpu.CompilerParams(dimension_semantics=sem, vmem_limit_bytes=VMEM_LIMIT)


def _layer_norm(y, g, b):
    mu = jnp.mean(y, axis=-1, keepdims=True)
    yc = y - mu
    var = jnp.mean(yc * yc, axis=-1, keepdims=True)
    return yc * lax.rsqrt(var + LN_EPS) * g + b


def _mm_kernel(x_ref, w_ref, o_ref):
    o_ref[...] = jnp.dot(x_ref[...], w_ref[...], preferred_element_type=F32).astype(o_ref.dtype)


def _matmul_layers(x, w_stack, tm, tn, out_dtype):
    m, k = x.shape
    layers, _, n = w_stack.shape
    return pl.pallas_call(
        _mm_kernel,
        grid=(m // tm, layers, n // tn),
        in_specs=[
            pl.BlockSpec((tm, k), lambda i, l, j: (i, 0)),
            pl.BlockSpec((None, k, tn), lambda i, l, j: (l, 0, j)),
        ],
        out_specs=pl.BlockSpec((None, tm, tn), lambda i, l, j: (l, i, j)),
        out_shape=jax.ShapeDtypeStruct((layers, m, n), out_dtype),
        compiler_params=_params("parallel", "arbitrary", "arbitrary"),
        name="matmul_layers",
    )(x, w_stack)


def _mm_heads_kernel(x_ref, w_ref, o_ref):
    heads, _, dh = o_ref.shape
    res = jnp.dot(x_ref[...], w_ref[...], preferred_element_type=F32).astype(o_ref.dtype)
    for h in range(heads):
        o_ref[h] = res[:, h * dh:(h + 1) * dh]


def _matmul_heads(x, w_stack, layer, col0, heads, dh, tm, out_dtype):
    m, k = x.shape
    n = heads * dh
    return pl.pallas_call(
        _mm_heads_kernel,
        grid=(m // tm,),
        in_specs=[
            pl.BlockSpec((tm, k), lambda i: (i, 0)),
            pl.BlockSpec((None, k, n), lambda i: (layer, 0, col0 // n)),
        ],
        out_specs=pl.BlockSpec((heads, tm, dh), lambda i: (0, i, 0)),
        out_shape=jax.ShapeDtypeStruct((heads, m, dh), out_dtype),
        compiler_params=_params("parallel"),
        name="matmul_heads",
    )(x, w_stack)


def _mm_t_heads_kernel(w_ref, x_ref, o_ref, *, scale):
    heads, dh, _ = o_ref.shape
    acc = lax.dot_general(w_ref[...], x_ref[...], TN_DIMS, preferred_element_type=F32)
    res = (acc * scale).astype(o_ref.dtype)
    for h in range(heads):
        o_ref[h] = res[h * dh:(h + 1) * dh, :]


def _matmul_t_heads(x, w_stack, layer, col0, heads, dh, batch, seq, scale, out_dtype):
    _, k = x.shape
    n = heads * dh
    return pl.pallas_call(
        functools.partial(_mm_t_heads_kernel, scale=scale),
        grid=(batch,),
        in_specs=[
            pl.BlockSpec((None, k, n), lambda b: (layer, 0, col0 // n)),
            pl.BlockSpec((seq, k), lambda b: (b, 0)),
        ],
        out_specs=pl.BlockSpec((heads, None, dh, seq), lambda b: (0, b, 0, 0)),
        out_shape=jax.ShapeDtypeStruct((heads, batch, dh, seq), out_dtype),
        compiler_params=_params("parallel"),
        name="matmul_t_heads",
    )(w_stack, x)


def _conv_kernel(x_ref, wh_ref, wb_ref, wc_ref, cw_ref, z_ref, u_ref):
    s = x_ref.shape[0]
    x = x_ref[...]
    pad = jnp.zeros((8, u_ref.shape[1]), F32)
    u_ref[0:8, :] = pad
    u_ref[s + 8:s + 16, :] = pad
    u_ref[8:s + 8, :] = (jnp.dot(x, wc_ref[...], preferred_element_type=F32)
                         * jnp.dot(x, wh_ref[...], preferred_element_type=F32))
    cb = jnp.dot(x, wb_ref[...], preferred_element_type=F32)
    cw = cw_ref[...]
    y = (cw[0:1, :] * u_ref[7:s + 7, :] + cw[1:2, :] * u_ref[8:s + 8, :]
         + cw[2:3, :] * u_ref[9:s + 9, :])
    z_ref[...] = (cb * y).astype(z_ref.dtype)


def _conv_branch(xb, w_in, conv_w, layer, batch, seq, tc=512):
    t, d = xb.shape
    nct = D_MODEL // tc
    return pl.pallas_call(
        _conv_kernel,
        grid=(batch, nct),
        in_specs=[
            pl.BlockSpec((seq, d), lambda b, j: (b, 0)),
            pl.BlockSpec((None, d, tc), lambda b, j: (layer, 0, COL_CH // tc + j)),
            pl.BlockSpec((None, d, tc), lambda b, j: (layer, 0, COL_CB // tc + j)),
            pl.BlockSpec((None, d, tc), lambda b, j: (layer, 0, COL_CC // tc + j)),
            pl.BlockSpec((None, CONV_K, tc), lambda b, j: (layer, 0, j)),
        ],
        out_specs=pl.BlockSpec((seq, tc), lambda b, j: (b, j)),
        out_shape=jax.ShapeDtypeStruct((t, D_MODEL), BF16),
        scratch_shapes=[pltpu.VMEM((seq + 16, tc), F32)],
        compiler_params=_params("parallel", "arbitrary"),
        name="conv_branch",
    )(xb, w_in, w_in, w_in, conv_w)


def _bias_kernel(rel_ref, lo_ref, hi_ref, bkt_ref, o_ref, *, rows):
    j = pl.program_id(0)
    h = pl.program_id(1)

    def chunk(c, carry):
        r0 = pl.multiple_of(c * rows, rows)
        lo = lo_ref[c, j]

        @pl.when(lo == hi_ref[c, j])
        def _():
            o_ref[pl.ds(r0, rows), :] = jnp.full((rows, o_ref.shape[1]), rel_ref[lo, h], F32) * LOG2E

        @pl.when(lo != hi_ref[c, j])
        def _():
            bkt = bkt_ref[pl.ds(r0, rows), :]
            acc = jnp.zeros(bkt.shape, F32)
            for b in range(REL_BUCKETS):
                acc = jnp.where(bkt == b, rel_ref[b, h], acc)
            o_ref[pl.ds(r0, rows), :] = acc * LOG2E

        return carry

    lax.fori_loop(0, bkt_ref.shape[0] // rows, chunk, 0)


def _rel_bias_kt(rel_bias, bucket_kt, tq=256, rows=128):
    s = bucket_kt.shape[0]
    blocks = bucket_kt.reshape(s // rows, rows, s // tq, tq)
    smem = pl.BlockSpec(memory_space=pltpu.SMEM)
    return pl.pallas_call(
        functools.partial(_bias_kernel, rows=rows),
        grid=(s // tq, DIFF_HEADS),
        in_specs=[smem, smem, smem, pl.BlockSpec((s, tq), lambda j, h: (0, j))],
        out_specs=pl.BlockSpec((None, s, tq), lambda j, h: (h, 0, j)),
        out_shape=jax.ShapeDtypeStruct((DIFF_HEADS, s, s), F32),
        compiler_params=_params("parallel", "arbitrary"),
        name="rel_bias",
    )(rel_bias, blocks.min(axis=(1, 3)), blocks.max(axis=(1, 3)), bucket_kt)


def _t5_bucket(rel):
    nb = REL_BUCKETS // 2
    max_exact = nb // 2
    ret = (rel > 0).astype(jnp.int32) * nb
    n = jnp.abs(rel)
    nf = jnp.maximum(n, 1).astype(F32)
    large = max_exact + (jnp.log(nf / max_exact) / math.log(REL_MAX_DIST / max_exact)
                         * (nb - max_exact)).astype(jnp.int32)
    large = jnp.minimum(large, nb - 1)
    return ret + jnp.where(n < max_exact, n, large)


def _diff_attn_kernel(lam_ref, qt_ref, k_ref, bias_ref, vt_ref, vtp_ref, g_ref, o_ref,
                      s0_ref, s1_ref, m0_ref, m1_ref, e_ref, a0_ref, a1_ref, fin_ref, *, lam_init, kc, nreal):
    group, _, tq = qt_ref.shape
    s = k_ref.shape[1]
    n = 2 * tq
    step = pl.program_id(0)

    @pl.when(step == 0)
    def _():
        s1_ref[...] = jnp.zeros(s1_ref.shape, F32)
        m1_ref[...] = jnp.zeros(m1_ref.shape, F32)
        a0_ref[...] = jnp.ones(a0_ref.shape, F32)
        fin_ref[...] = jnp.zeros(fin_ref.shape, BF16)

    for j in range(group - 2):
        o_ref[j] = fin_ref[j]

    def sub_tile(j):
        sw_ref, mw_ref, sr_ref, mr_ref = (s0_ref, m0_ref, s1_ref, m1_ref) if j % 2 == 0 else (s1_ref, m1_ref, s0_ref, m0_ref)
        aw_ref, ar_ref = (a1_ref, a0_ref) if j % 2 == 0 else (a0_ref, a1_ref)

        qt = qt_ref[j]
        row = lax.broadcasted_iota(jnp.int32, qt.shape, 0)
        zero = jnp.zeros_like(qt)
        q12 = jnp.concatenate([jnp.where(row < DIFF_DK, qt, zero), jnp.where(row >= DIFF_DK, qt, zero)], axis=1)
        b = bias_ref[...]
        sc = jnp.dot(k_ref[j], q12, preferred_element_type=F32) + jnp.concatenate([b, b], axis=1)
        sw_ref[...] = sc
        mw_ref[...] = jnp.max(sc.reshape(s // 8, 8, n), axis=0)

        m = jnp.max(mr_ref[...], axis=0, keepdims=True)
        for c in range(s // kc):
            sl = slice(c * kc, (c + 1) * kc)
            e_ref[sl, :] = jnp.exp2(sr_ref[sl, :] - m).astype(BF16)
        orow = lax.broadcasted_iota(jnp.int32, (SUM_ROWS, s), 0)
        ones = jnp.where(orow == 0, 1.0, 0.0).astype(BF16)
        vt_prev = vtp_ref[...] if j == 0 else vt_ref[j - 1]
        vt = jnp.concatenate([vt_prev, ones], axis=0)
        aw_ref[...] = jnp.dot(vt, e_ref[...], preferred_element_type=F32)

        acc = ar_ref[...]
        lf = lam_ref[...]
        lam = (jnp.exp(jnp.sum(lf[0:1, :] * lf[1:2, :], axis=-1, keepdims=True))
               - jnp.exp(jnp.sum(lf[2:3, :] * lf[3:4, :], axis=-1, keepdims=True)) + lam_init)
        r = 1.0 / acc[DIFF_DV:DIFF_DV + 1, :]
        ot = acc[:DIFF_DV, :tq] * r[:, :tq] - acc[:DIFF_DV, tq:] * (lam * r[:, tq:])
        o = ot.T
        ms = jnp.mean(o * o, axis=-1, keepdims=True)
        res = (o * lax.rsqrt(ms + LN_EPS) * g_ref[...] * (1.0 - lam_init)).astype(o_ref.dtype)
        if j < 2:
            o_ref[group - 2 + j] = res
        else:
            fin_ref[j - 2] = res

    for j in range(group):
        pl.when(step >= j - group if j < 2 else step < nreal)(functools.partial(sub_tile, j))


def _diff_attention(qt, k, vt, bias_kt, diff_lambda, subln_g, layer, lam_init, tq=512, kc=256, group=4):
    heads, batch, seq, _ = k.shape
    nqt = seq // tq
    nbg = batch // group
    nreal = heads * nqt * nbg

    def dec(gs):
        return gs // (nqt * nbg), (gs // nbg) % nqt, gs % nbg

    cur = lambda gs: dec(jnp.minimum(gs, nreal - 1))
    out = lambda gs: dec(jnp.maximum(gs - 1, 0))

    def qt_map(gs):
        h, i, bg = cur(gs)
        return (h, bg, 0, i)

    def kv_map(gs):
        h, i, bg = cur(gs)
        return (h, bg, 0, 0)

    def bias_map(gs):
        h, i, bg = cur(gs)
        return (h, 0, i)

    def vtp_map(gs):
        h, i, bg = dec(jnp.maximum(gs - 1, 0))
        return (h, bg * group + group - 1, 0, 0)

    def o_map(gs):
        h, i, bg = out(gs)
        return (h, bg, i, 0)

    kern = functools.partial(_diff_attn_kernel, lam_init=lam_init, kc=kc, nreal=nreal)
    return pl.pallas_call(
        kern,
        grid=(nreal + 1,),
        in_specs=[
            pl.BlockSpec((None, 4, DIFF_DK), lambda gs: (layer, 0, 0)),
            pl.BlockSpec((None, group, DIFF_DV, tq), qt_map),
            pl.BlockSpec((None, group, seq, DIFF_DV), kv_map),
            pl.BlockSpec((None, seq, tq), bias_map),
            pl.BlockSpec((None, group, DIFF_DV, seq), kv_map),
            pl.BlockSpec((None, None, DIFF_DV, seq), vtp_map),
            pl.BlockSpec((None, 1, DIFF_DV), lambda gs: (layer, 0, 0)),
        ],
        out_specs=pl.BlockSpec((None, group, tq, DIFF_DV), o_map),
        out_shape=jax.ShapeDtypeStruct((heads, batch, seq, DIFF_DV), BF16),
        scratch_shapes=[
            pltpu.VMEM((seq, 2 * tq), F32), pltpu.VMEM((seq, 2 * tq), F32),
            pltpu.VMEM((8, 2 * tq), F32), pltpu.VMEM((8, 2 * tq), F32),
            pltpu.VMEM((seq, 2 * tq), BF16),
            pltpu.VMEM((DIFF_DV + SUM_ROWS, 2 * tq), F32), pltpu.VMEM((DIFF_DV + SUM_ROWS, 2 * tq), F32),
            pltpu.VMEM((group - 2, tq, DIFF_DV), BF16),
        ],
        compiler_params=_params("arbitrary"),
        name="diff_attn",
    )(diff_lambda, qt, k, bias_kt, vt, vt, subln_g)


def _merge_kernel(x_ref, xb_ref, z_ref, att_ref, mkv_ref,
                  wmq_ref, wga_ref, wgb_ref, wgc_ref, bg_ref,
                  wca_ref, wdo_ref, wmo_ref, wo_ref, g_ref, b_ref,
                  y_ref, yb_ref, o_scr, *, alpha):
    xb = xb_ref[...]
    mq = (jnp.dot(xb, wmq_ref[...], preferred_element_type=F32) * (MEM_DH ** -0.5)).astype(BF16)
    w = MEM_HEADS * MEM_DH
    for h in range(MEM_HEADS):
        hs = slice(h * MEM_DH, (h + 1) * MEM_DH)
        sc = lax.dot_general(mq[:, hs], mkv_ref[:, hs], NT_DIMS, preferred_element_type=F32)
        e = jnp.exp(sc - jnp.max(sc, axis=-1, keepdims=True))
        p = e * (1.0 / jnp.sum(e, axis=-1, keepdims=True))
        o_scr[:, hs] = jnp.dot(p.astype(BF16), mkv_ref[:, w + h * MEM_DH:w + (h + 1) * MEM_DH],
                               preferred_element_type=F32).astype(BF16)
    out_c = jnp.dot(o_scr[...], wmo_ref[...], preferred_element_type=F32)
    out_a = jnp.dot(z_ref[...], wca_ref[...], preferred_element_type=F32)
    att = jnp.concatenate([att_ref[h] for h in range(att_ref.shape[0])], axis=1)
    out_b = jnp.dot(att, wdo_ref[...], preferred_element_type=F32)
    d = D_MODEL

    def gate(wg_ref, j):
        gl = jnp.dot(xb, wg_ref[...], preferred_element_type=F32) + bg_ref[:, j * d:(j + 1) * d]
        return 1.0 / (1.0 + jnp.exp(-gl))

    merged = gate(wga_ref, 0) * out_a + gate(wgb_ref, 1) * out_b + gate(wgc_ref, 2) * out_c
    hmix = jnp.dot(merged.astype(BF16), wo_ref[...], preferred_element_type=F32)
    y = _layer_norm(alpha * x_ref[...] + hmix, g_ref[...], b_ref[...])
    y_ref[...] = y
    yb_ref[...] = y.astype(BF16)


def _merge(x, xb, z, att, mkv, w_in, b_gate, w_conv_out, w_diff_out, w_mem_out, w_o, ln_g, ln_b,
           layer, seq, mem_len, alpha, tm=512):
    t, d = x.shape
    per_b = seq // tm
    row = lambda i: (i, 0)
    wspec = lambda: pl.BlockSpec((None, d, d), lambda i: (layer, 0, 0), pipeline_mode=pl.Buffered(1))
    win = lambda col: pl.BlockSpec((None, d, d), lambda i: (layer, 0, col // d), pipeline_mode=pl.Buffered(1))
    vec = lambda n: pl.BlockSpec((None, 1, n), lambda i: (layer, 0, 0))
    kern = functools.partial(_merge_kernel, alpha=alpha)
    return pl.pallas_call(
        kern,
        grid=(t // tm,),
        in_specs=[
            pl.BlockSpec((tm, d), row),
            pl.BlockSpec((tm, d), row),
            pl.BlockSpec((tm, d), row),
            pl.BlockSpec((DIFF_HEADS, tm, DIFF_DV), lambda i: (0, i, 0)),
            pl.BlockSpec((None, mem_len, 2 * d), lambda i: (layer, i // per_b, 0)),
            win(COL_MQ), win(COL_GATE), win(COL_GATE + d), win(COL_GATE + 2 * d),
            vec(N_BRANCH * d),
            wspec(), wspec(), wspec(), wspec(),
            vec(d), vec(d),
        ],
        out_specs=[pl.BlockSpec((tm, d), row), pl.BlockSpec((tm, d), row)],
        out_shape=[jax.ShapeDtypeStruct((t, d), F32), jax.ShapeDtypeStruct((t, d), BF16)],
        scratch_shapes=[pltpu.VMEM((tm, d), BF16)],
        compiler_params=_params("parallel"),
        name="merge",
    )(x, xb, z, att, mkv, w_in, w_in, w_in, w_in, b_gate,
      w_conv_out, w_diff_out, w_mem_out, w_o, ln_g, ln_b)


def _mlp_kernel(x_ref, xb_ref, w1_ref, w2_ref, g_ref, b_ref, y_ref, yb_ref, *, alpha, fc):
    xb = xb_ref[...]
    f = jnp.zeros(x_ref.shape, F32)
    for c in range(w1_ref.shape[1] // fc):
        cs = slice(c * fc, (c + 1) * fc)
        hid = jnp.maximum(jnp.dot(xb, w1_ref[:, cs], preferred_element_type=F32), 0.0)
        f = f + jnp.dot((hid * hid).astype(BF16), w2_ref[cs, :], preferred_element_type=F32)
    y = _layer_norm(alpha * x_ref[...] + f, g_ref[...], b_ref[...])
    y_ref[...] = y
    yb_ref[...] = y.astype(BF16)


def _mlp(x, xb, w1, w2, ln_g, ln_b, layer, alpha, tm=1024, fc=1024):
    t, d = x.shape
    dff = w1.shape[2]
    row = lambda i: (i, 0)
    vec = lambda n: pl.BlockSpec((None, 1, n), lambda i: (layer, 0, 0))
    kern = functools.partial(_mlp_kernel, alpha=alpha, fc=fc)
    return pl.pallas_call(
        kern,
        grid=(t // tm,),
        in_specs=[
            pl.BlockSpec((tm, d), row),
            pl.BlockSpec((tm, d), row),
            pl.BlockSpec((None, d, dff), lambda i: (layer, 0, 0), pipeline_mode=pl.Buffered(1)),
            pl.BlockSpec((None, dff, d), lambda i: (layer, 0, 0), pipeline_mode=pl.Buffered(1)),
            vec(d), vec(d),
        ],
        out_specs=[pl.BlockSpec((tm, d), row), pl.BlockSpec((tm, d), row)],
        out_shape=[jax.ShapeDtypeStruct((t, d), F32), jax.ShapeDtypeStruct((t, d), BF16)],
        compiler_params=_params("parallel"),
        name="mlp",
    )(x, xb, w1, w2, ln_g, ln_b)


def kernel(x, mem, w_in, b_gate, conv_w, w_conv_out, diff_lambda, subln_g, w_diff_out, rel_bias,
           w_mem_kv, w_mem_out, w_o, ln1_g, ln1_b, w_mlp1, w_mlp2, ln2_g, ln2_b):
    batch, seq, d = x.shape
    mem_len = mem.shape[1]
    depth = w_in.shape[0]
    alpha = (2 * depth) ** 0.25

    w_in_b = w_in.astype(BF16)
    w_conv_out_b = w_conv_out.astype(BF16)
    w_diff_out_b = w_diff_out.astype(BF16)
    w_mem_kv_b = w_mem_kv.astype(BF16)
    w_mem_out_b = w_mem_out.astype(BF16)
    w_o_b = w_o.astype(BF16)
    w_mlp1_b = w_mlp1.astype(BF16)
    w_mlp2_b = w_mlp2.astype(BF16)
    mem_b = mem.reshape(batch * mem_len, d).astype(BF16)
    vec3 = lambda a: a.reshape(depth, 1, a.shape[-1])

    pos = jnp.arange(seq, dtype=jnp.int32)
    bucket_kt = _t5_bucket(pos[:, None] - pos[None, :])
    bias_kt = _rel_bias_kt(rel_bias, bucket_kt)

    mkv = _matmul_layers(mem_b, w_mem_kv_b, 1024, 1024, BF16)

    xf = x.reshape(batch * seq, d)
    xb = xf.astype(BF16)
    for l in range(depth):
        lam_init = 0.8 - 0.6 * math.exp(-0.3 * l)
        qt = _matmul_t_heads(xb, w_in_b, l, COL_Q, DIFF_HEADS, 2 * DIFF_DK, batch, seq,
                             DIFF_DK ** -0.5 * LOG2E, BF16)
        k = _matmul_heads(xb, w_in_b, l, COL_K, DIFF_HEADS, 2 * DIFF_DK, seq, BF16)
        vt = _matmul_t_heads(xb, w_in_b, l, COL_V, DIFF_HEADS, DIFF_DV, batch, seq, 1.0, BF16)
        att = _diff_attention(qt, k.reshape(DIFF_HEADS, batch, seq, 2 * DIFF_DK), vt, bias_kt, diff_lambda,
                              vec3(subln_g), l, lam_init).reshape(DIFF_HEADS, batch * seq, DIFF_DV)
        z = _conv_branch(xb, w_in_b, conv_w, l, batch, seq)
        xf, xb = _merge(xf, xb, z, att, mkv, w_in_b, vec3(b_gate), w_conv_out_b, w_diff_out_b,
                        w_mem_out_b, w_o_b, vec3(ln1_g), vec3(ln1_b), l, seq, mem_len, alpha)
        xf, xb = _mlp(xf, xb, w_mlp1_b, w_mlp2_b, vec3(ln2_g), vec3(ln2_b), l, alpha)
    return xf.reshape(batch, seq, d)
```

```python
import functools
import math

import jax
import jax.numpy as jnp
from jax import lax
from jax.experimental import pallas as pl
from jax.experimental.pallas import tpu as pltpu

F32 = jnp.float32
BF16 = jnp.bfloat16

D_MODEL = 1024
CONV_K = 3
DIFF_HEADS = 8
DIFF_DK = 64
DIFF_DV = 2 * DIFF_DK
MEM_HEADS = 4
MEM_DH = 256
N_BRANCH = 3
REL_BUCKETS = 32
REL_MAX_DIST = 128
LN_EPS = 1e-5
LOG2E = math.log2(math.e)
SUM_ROWS = 16

COL_CH, COL_CB, COL_CC = 0, D_MODEL, 2 * D_MODEL
COL_Q, COL_K, COL_V = 3 * D_MODEL, 4 * D_MODEL, 5 * D_MODEL
COL_MQ = 6 * D_MODEL
COL_GATE = 7 * D_MODEL

VMEM_LIMIT = 56 * 1024 * 1024

NT_DIMS = (((1,), (1,)), ((), ()))
TN_DIMS = (((0,), (1,)), ((), ()))


def _params(*sem):
    return pltpu.CompilerParams(dimension_semantics=sem, vmem_limit_bytes=VMEM_LIMIT)


def _layer_norm(y, g, b):
    mu = jnp.mean(y, axis=-1, keepdims=True)
    yc = y - mu
    var = jnp.mean(yc * yc, axis=-1, keepdims=True)
    return yc * lax.rsqrt(var + LN_EPS) * g + b


def _mm_kernel(x_ref, w_ref, o_ref):
    o_ref[...] = jnp.dot(x_ref[...], w_ref[...], preferred_element_type=F32).astype(o_ref.dtype)


def _matmul_layers(x, w_stack, tm, tn, out_dtype):
    m, k = x.shape
    layers, _, n = w_stack.shape
    return pl.pallas_call(
        _mm_kernel,
        grid=(m // tm, layers, n // tn),
        in_specs=[
            pl.BlockSpec((tm, k), lambda i, l, j: (i, 0)),
            pl.BlockSpec((None, k, tn), lambda i, l, j: (l, 0, j)),
        ],
        out_specs=pl.BlockSpec((None, tm, tn), lambda i, l, j: (l, i, j)),
        out_shape=jax.ShapeDtypeStruct((layers, m, n), out_dtype),
        compiler_params=_params("parallel", "arbitrary", "arbitrary"),
        name="matmul_layers",
    )(x, w_stack)


def _mm_heads_kernel(x_ref, w_ref, o_ref):
    heads, _, dh = o_ref.shape
    res = jnp.dot(x_ref[...], w_ref[...], preferred_element_type=F32).astype(o_ref.dtype)
    for h in range(heads):
        o_ref[h] = res[:, h * dh:(h + 1) * dh]


def _matmul_heads(x, w_stack, layer, col0, heads, dh, tm, out_dtype):
    m, k = x.shape
    n = heads * dh
    return pl.pallas_call(
        _mm_heads_kernel,
        grid=(m // tm,),
        in_specs=[
            pl.BlockSpec((tm, k), lambda i: (i, 0)),
            pl.BlockSpec((None, k, n), lambda i: (layer, 0, col0 // n)),
        ],
        out_specs=pl.BlockSpec((heads, tm, dh), lambda i: (0, i, 0)),
        out_shape=jax.ShapeDtypeStruct((heads, m, dh), out_dtype),
        compiler_params=_params("parallel"),
        name="matmul_heads",
    )(x, w_stack)


def _mm_t_heads_kernel(w_ref, x_ref, o_ref, *, scale):
    heads, dh, _ = o_ref.shape
    acc = lax.dot_general(w_ref[...], x_ref[...], TN_DIMS, preferred_element_type=F32)
    res = (acc * scale).astype(o_ref.dtype)
    for h in range(heads):
        o_ref[h] = res[h * dh:(h + 1) * dh, :]


def _matmul_t_heads(x, w_stack, layer, col0, heads, dh, batch, seq, scale, out_dtype):
    _, k = x.shape
    n = heads * dh
    return pl.pallas_call(
        functools.partial(_mm_t_heads_kernel, scale=scale),
        grid=(batch,),
        in_specs=[
            pl.BlockSpec((None, k, n), lambda b: (layer, 0, col0 // n)),
            pl.BlockSpec((seq, k), lambda b: (b, 0)),
        ],
        out_specs=pl.BlockSpec((heads, None, dh, seq), lambda b: (0, b, 0, 0)),
        out_shape=jax.ShapeDtypeStruct((heads, batch, dh, seq), out_dtype),
        compiler_params=_params("parallel"),
        name="matmul_t_heads",
    )(w_stack, x)


def _conv_kernel(x_ref, wh_ref, wb_ref, wc_ref, cw_ref, z_ref, u_ref):
    s = x_ref.shape[0]
    x = x_ref[...]
    pad = jnp.zeros((8, u_ref.shape[1]), F32)
    u_ref[0:8, :] = pad
    u_ref[s + 8:s + 16, :] = pad
    u_ref[8:s + 8, :] = (jnp.dot(x, wc_ref[...], preferred_element_type=F32)
                         * jnp.dot(x, wh_ref[...], preferred_element_type=F32))
    cb = jnp.dot(x, wb_ref[...], preferred_element_type=F32)
    cw = cw_ref[...]
    y = (cw[0:1, :] * u_ref[7:s + 7, :] + cw[1:2, :] * u_ref[8:s + 8, :]
         + cw[2:3, :] * u_ref[9:s + 9, :])
    z_ref[...] = (cb * y).astype(z_ref.dtype)


def _conv_branch(xb, w_in, conv_w, layer, batch, seq, tc=512):
    t, d = xb.shape
    nct = D_MODEL // tc
    return pl.pallas_call(
        _conv_kernel,
        grid=(batch, nct),
        in_specs=[
            pl.BlockSpec((seq, d), lambda b, j: (b, 0)),
            pl.BlockSpec((None, d, tc), lambda b, j: (layer, 0, COL_CH // tc + j)),
            pl.BlockSpec((None, d, tc), lambda b, j: (layer, 0, COL_CB // tc + j)),
            pl.BlockSpec((None, d, tc), lambda b, j: (layer, 0, COL_CC // tc + j)),
            pl.BlockSpec((None, CONV_K, tc), lambda b, j: (layer, 0, j)),
        ],
        out_specs=pl.BlockSpec((seq, tc), lambda b, j: (b, j)),
        out_shape=jax.ShapeDtypeStruct((t, D_MODEL), BF16),
        scratch_shapes=[pltpu.VMEM((seq + 16, tc), F32)],
        compiler_params=_params("parallel", "arbitrary"),
        name="conv_branch",
    )(xb, w_in, w_in, w_in, conv_w)


def _bias_kernel(rel_ref, lo_ref, hi_ref, bkt_ref, o_ref, *, rows):
    j = pl.program_id(0)
    h = pl.program_id(1)

    def chunk(c, carry):
        r0 = pl.multiple_of(c * rows, rows)
        lo = lo_ref[c, j]

        @pl.when(lo == hi_ref[c, j])
        def _():
            o_ref[pl.ds(r0, rows), :] = jnp.full((rows, o_ref.shape[1]), rel_ref[lo, h], F32) * LOG2E

        @pl.when(lo != hi_ref[c, j])
        def _():
            bkt = bkt_ref[pl.ds(r0, rows), :]
            acc = jnp.zeros(bkt.shape, F32)
            for b in range(REL_BUCKETS):
                acc = jnp.where(bkt == b, rel_ref[b, h], acc)
            o_ref[pl.ds(r0, rows), :] = acc * LOG2E

        return carry

    lax.fori_loop(0, bkt_ref.shape[0] // rows, chunk, 0)


def _rel_bias_kt(rel_bias, bucket_kt, tq=256, rows=128):
    s = bucket_kt.shape[0]
    blocks = bucket_kt.reshape(s // rows, rows, s // tq, tq)
    smem = pl.BlockSpec(memory_space=pltpu.SMEM)
    return pl.pallas_call(
        functools.partial(_bias_kernel, rows=rows),
        grid=(s // tq, DIFF_HEADS),
        in_specs=[smem, smem, smem, pl.BlockSpec((s, tq), lambda j, h: (0, j))],
        out_specs=pl.BlockSpec((None, s, tq), lambda j, h: (h, 0, j)),
        out_shape=jax.ShapeDtypeStruct((DIFF_HEADS, s, s), F32),
        compiler_params=_params("parallel", "arbitrary"),
        name="rel_bias",
    )(rel_bias, blocks.min(axis=(1, 3)), blocks.max(axis=(1, 3)), bucket_kt)


def _t5_bucket(rel):
    nb = REL_BUCKETS // 2
    max_exact = nb // 2
    ret = (rel > 0).astype(jnp.int32) * nb
    n = jnp.abs(rel)
    nf = jnp.maximum(n, 1).astype(F32)
    large = max_exact + (jnp.log(nf / max_exact) / math.log(REL_MAX_DIST / max_exact)
                         * (nb - max_exact)).astype(jnp.int32)
    large = jnp.minimum(large, nb - 1)
    return ret + jnp.where(n < max_exact, n, large)


def _diff_attn_kernel(lam_ref, qt_ref, k_ref, bias_ref, vt_ref, vtp_ref, g_ref, o_ref,
                      s0_ref, s1_ref, m0_ref, m1_ref, e_ref, a0_ref, a1_ref, fin_ref, *, lam_init, kc):
    group, _, tq = qt_ref.shape
    s = k_ref.shape[1]
    n = 2 * tq
    step = pl.program_id(0)

    @pl.when(step == 0)
    def _():
        s1_ref[...] = jnp.zeros(s1_ref.shape, F32)
        m1_ref[...] = jnp.zeros(m1_ref.shape, F32)
        a0_ref[...] = jnp.ones(a0_ref.shape, F32)
        fin_ref[...] = jnp.zeros(fin_ref.shape, BF16)

    for j in range(group - 2):
        o_ref[j] = fin_ref[j]

    def sub_tile(j):
        sw_ref, mw_ref, sr_ref, mr_ref = (s0_ref, m0_ref, s1_ref, m1_ref) if j % 2 == 0 else (s1_ref, m1_ref, s0_ref, m0_ref)
        aw_ref, ar_ref = (a1_ref, a0_ref) if j % 2 == 0 else (a0_ref, a1_ref)

        qt = qt_ref[j]
        row = lax.broadcasted_iota(jnp.int32, qt.shape, 0)
        zero = jnp.zeros_like(qt)
        q12 = jnp.concatenate([jnp.where(row < DIFF_DK, qt, zero), jnp.where(row >= DIFF_DK, qt, zero)], axis=1)
        b = bias_ref[...]
        sc = jnp.dot(k_ref[j], q12, preferred_element_type=F32) + jnp.concatenate([b, b], axis=1)
        sw_ref[...] = sc
        mw_ref[...] = jnp.max(sc.reshape(s // 8, 8, n), axis=0)

        m = jnp.max(mr_ref[...], axis=0, keepdims=True)
        for c in range(s // kc):
            sl = slice(c * kc, (c + 1) * kc)
            e_ref[sl, :] = jnp.exp2(sr_ref[sl, :] - m).astype(BF16)
        orow = lax.broadcasted_iota(jnp.int32, (SUM_ROWS, s), 0)
        ones = jnp.where(orow == 0, 1.0, 0.0).astype(BF16)
        vt_prev = vtp_ref[...] if j == 0 else vt_ref[j - 1]
        vt = jnp.concatenate([vt_prev, ones], axis=0)
        aw_ref[...] = jnp.dot(vt, e_ref[...], preferred_element_type=F32)

        acc = ar_ref[...]
        lf = lam_ref[...]
        lam = (jnp.exp(jnp.sum(lf[0:1, :] * lf[1:2, :], axis=-1, keepdims=True))
               - jnp.exp(jnp.sum(lf[2:3, :] * lf[3:4, :], axis=-1, keepdims=True)) + lam_init)
        r = 1.0 / acc[DIFF_DV:DIFF_DV + 1, :]
        ot = acc[:DIFF_DV, :tq] * r[:, :tq] - acc[:DIFF_DV, tq:] * (lam * r[:, tq:])
        o = ot.T
        ms = jnp.mean(o * o, axis=-1, keepdims=True)
        res = (o * lax.rsqrt(ms + LN_EPS) * g_ref[...] * (1.0 - lam_init)).astype(o_ref.dtype)
        if j < 2:
            o_ref[group - 2 + j] = res
        else:
            fin_ref[j - 2] = res

    for j in range(group):
        pl.when(step >= j - group)(functools.partial(sub_tile, j))


def _diff_attention(qt, k, vt, bias_kt, diff_lambda, subln_g, layer, lam_init, tq=512, kc=256, group=4):
    heads, batch, seq, _ = k.shape
    nqt = seq // tq
    nbg = batch // group
    nreal = heads * nqt * nbg

    def dec(gs):
        return gs // (nqt * nbg), (gs // nbg) % nqt, gs % nbg

    cur = lambda gs: dec(jnp.minimum(gs, nreal - 1))
    out = lambda gs: dec(jnp.maximum(gs - 1, 0))

    def qt_map(gs):
        h, i, bg = cur(gs)
        return (h, bg, 0, i)

    def kv_map(gs):
        h, i, bg = cur(gs)
        return (h, bg, 0, 0)

    def bias_map(gs):
        h, i, bg = cur(gs)
        return (h, 0, i)

    def vtp_map(gs):
        h, i, bg = dec(jnp.maximum(gs - 1, 0))
        return (h, bg * group + group - 1, 0, 0)

    def o_map(gs):
        h, i, bg = out(gs)
        return (h, bg, i, 0)

    kern = functools.partial(_diff_attn_kernel, lam_init=lam_init, kc=kc)
    return pl.pallas_call(
        kern,
        grid=(nreal + 1,),
        in_specs=[
            pl.BlockSpec((None, 4, DIFF_DK), lambda gs: (layer, 0, 0)),
            pl.BlockSpec((None, group, DIFF_DV, tq), qt_map),
            pl.BlockSpec((None, group, seq, DIFF_DV), kv_map),
            pl.BlockSpec((None, seq, tq), bias_map),
            pl.BlockSpec((None, group, DIFF_DV, seq), kv_map),
            pl.BlockSpec((None, None, DIFF_DV, seq), vtp_map),
            pl.BlockSpec((None, 1, DIFF_DV), lambda gs: (layer, 0, 0)),
        ],
        out_specs=pl.BlockSpec((None, group, tq, DIFF_DV), o_map),
        out_shape=jax.ShapeDtypeStruct((heads, batch, seq, DIFF_DV), BF16),
        scratch_shapes=[
            pltpu.VMEM((seq, 2 * tq), F32), pltpu.VMEM((seq, 2 * tq), F32),
            pltpu.VMEM((8, 2 * tq), F32), pltpu.VMEM((8, 2 * tq), F32),
            pltpu.VMEM((seq, 2 * tq), BF16),
            pltpu.VMEM((DIFF_DV + SUM_ROWS, 2 * tq), F32), pltpu.VMEM((DIFF_DV + SUM_ROWS, 2 * tq), F32),
            pltpu.VMEM((group - 2, tq, DIFF_DV), BF16),
        ],
        compiler_params=_params("arbitrary"),
        name="diff_attn",
    )(diff_lambda, qt, k, bias_kt, vt, vt, subln_g)


def _merge_kernel(x_ref, xb_ref, z_ref, att_ref, mkv_ref,
                  wmq_ref, wga_ref, wgb_ref, wgc_ref, bg_ref,
                  wca_ref, wdo_ref, wmo_ref, wo_ref, g_ref, b_ref,
                  y_ref, yb_ref, o_scr, *, alpha):
    xb = xb_ref[...]
    mq = (jnp.dot(xb, wmq_ref[...], preferred_element_type=F32) * (MEM_DH ** -0.5)).astype(BF16)
    w = MEM_HEADS * MEM_DH
    for h in range(MEM_HEADS):
        hs = slice(h * MEM_DH, (h + 1) * MEM_DH)
        sc = lax.dot_general(mq[:, hs], mkv_ref[:, hs], NT_DIMS, preferred_element_type=F32)
        e = jnp.exp(sc - jnp.max(sc, axis=-1, keepdims=True))
        p = e * (1.0 / jnp.sum(e, axis=-1, keepdims=True))
        o_scr[:, hs] = jnp.dot(p.astype(BF16), mkv_ref[:, w + h * MEM_DH:w + (h + 1) * MEM_DH],
                               preferred_element_type=F32).astype(BF16)
    out_c = jnp.dot(o_scr[...], wmo_ref[...], preferred_element_type=F32)
    out_a = jnp.dot(z_ref[...], wca_ref[...], preferred_element_type=F32)
    att = jnp.concatenate([att_ref[h] for h in range(att_ref.shape[0])], axis=1)
    out_b = jnp.dot(att, wdo_ref[...], preferred_element_type=F32)
    d = D_MODEL

    def gate(wg_ref, j):
        gl = jnp.dot(xb, wg_ref[...], preferred_element_type=F32) + bg_ref[:, j * d:(j + 1) * d]
        return 1.0 / (1.0 + jnp.exp(-gl))

    merged = gate(wga_ref, 0) * out_a + gate(wgb_ref, 1) * out_b + gate(wgc_ref, 2) * out_c
    hmix = jnp.dot(merged.astype(BF16), wo_ref[...], preferred_element_type=F32)
    y = _layer_norm(alpha * x_ref[...] + hmix, g_ref[...], b_ref[...])
    y_ref[...] = y
    yb_ref[...] = y.astype(BF16)


def _merge(x, xb, z, att, mkv, w_in, b_gate, w_conv_out, w_diff_out, w_mem_out, w_o, ln_g, ln_b,
           layer, seq, mem_len, alpha, tm=512):
    t, d = x.shape
    per_b = seq // tm
    row = lambda i: (i, 0)
    wspec = lambda: pl.BlockSpec((None, d, d), lambda i: (layer, 0, 0), pipeline_mode=pl.Buffered(1))
    win = lambda col: pl.BlockSpec((None, d, d), lambda i: (layer, 0, col // d), pipeline_mode=pl.Buffered(1))
    vec = lambda n: pl.BlockSpec((None, 1, n), lambda i: (layer, 0, 0))
    kern = functools.partial(_merge_kernel, alpha=alpha)
    return pl.pallas_call(
        kern,
        grid=(t // tm,),
        in_specs=[
            pl.BlockSpec((tm, d), row),
            pl.BlockSpec((tm, d), row),
            pl.BlockSpec((tm, d), row),
            pl.BlockSpec((DIFF_HEADS, tm, DIFF_DV), lambda i: (0, i, 0)),
            pl.BlockSpec((None, mem_len, 2 * d), lambda i: (layer, i // per_b, 0)),
            win(COL_MQ), win(COL_GATE), win(COL_GATE + d), win(COL_GATE + 2 * d),
            vec(N_BRANCH * d),
            wspec(), wspec(), wspec(), wspec(),
            vec(d), vec(d),
        ],
        out_specs=[pl.BlockSpec((tm, d), row), pl.BlockSpec((tm, d), row)],
        out_shape=[jax.ShapeDtypeStruct((t, d), F32), jax.ShapeDtypeStruct((t, d), BF16)],
        scratch_shapes=[pltpu.VMEM((tm, d), BF16)],
        compiler_params=_params("parallel"),
        name="merge",
    )(x, xb, z, att, mkv, w_in, w_in, w_in, w_in, b_gate,
      w_conv_out, w_diff_out, w_mem_out, w_o, ln_g, ln_b)


def _mlp_kernel(x_ref, xb_ref, w1_ref, w2_ref, g_ref, b_ref, y_ref, yb_ref, *, alpha, fc):
    xb = xb_ref[...]
    f = jnp.zeros(x_ref.shape, F32)
    for c in range(w1_ref.shape[1] // fc):
        cs = slice(c * fc, (c + 1) * fc)
        hid = jnp.maximum(jnp.dot(xb, w1_ref[:, cs], preferred_element_type=F32), 0.0)
        f = f + jnp.dot((hid * hid).astype(BF16), w2_ref[cs, :], preferred_element_type=F32)
    y = _layer_norm(alpha * x_ref[...] + f, g_ref[...], b_ref[...])
    y_ref[...] = y
    yb_ref[...] = y.astype(BF16)


def _mlp(x, xb, w1, w2, ln_g, ln_b, layer, alpha, tm=1024, fc=1024):
    t, d = x.shape
    dff = w1.shape[2]
    row = lambda i: (i, 0)
    vec = lambda n: pl.BlockSpec((None, 1, n), lambda i: (layer, 0, 0))
    kern = functools.partial(_mlp_kernel, alpha=alpha, fc=fc)
    return pl.pallas_call(
        kern,
        grid=(t // tm,),
        in_specs=[
            pl.BlockSpec((tm, d), row),
            pl.BlockSpec((tm, d), row),
            pl.BlockSpec((None, d, dff), lambda i: (layer, 0, 0), pipeline_mode=pl.Buffered(1)),
            pl.BlockSpec((None, dff, d), lambda i: (layer, 0, 0), pipeline_mode=pl.Buffered(1)),
            vec(d), vec(d),
        ],
        out_specs=[pl.BlockSpec((tm, d), row), pl.BlockSpec((tm, d), row)],
        out_shape=[jax.ShapeDtypeStruct((t, d), F32), jax.ShapeDtypeStruct((t, d), BF16)],
        compiler_params=_params("parallel"),
        name="mlp",
    )(x, xb, w1, w2, ln_g, ln_b)


def kernel(x, mem, w_in, b_gate, conv_w, w_conv_out, diff_lambda, subln_g, w_diff_out, rel_bias,
           w_mem_kv, w_mem_out, w_o, ln1_g, ln1_b, w_mlp1, w_mlp2, ln2_g, ln2_b):
    batch, seq, d = x.shape
    mem_len = mem.shape[1]
    depth = w_in.shape[0]
    alpha = (2 * depth) ** 0.25

    w_in_b = w_in.astype(BF16)
    w_conv_out_b = w_conv_out.astype(BF16)
    w_diff_out_b = w_diff_out.astype(BF16)
    w_mem_kv_b = w_mem_kv.astype(BF16)
    w_mem_out_b = w_mem_out.astype(BF16)
    w_o_b = w_o.astype(BF16)
    w_mlp1_b = w_mlp1.astype(BF16)
    w_mlp2_b = w_mlp2.astype(BF16)
    mem_b = mem.reshape(batch * mem_len, d).astype(BF16)
    vec3 = lambda a: a.reshape(depth, 1, a.shape[-1])

    pos = jnp.arange(seq, dtype=jnp.int32)
    bucket_kt = _t5_bucket(pos[:, None] - pos[None, :])
    bias_kt = _rel_bias_kt(rel_bias, bucket_kt)

    mkv = _matmul_layers(mem_b, w_mem_kv_b, 1024, 1024, BF16)

    xf = x.reshape(batch * seq, d)
    xb = xf.astype(BF16)
    for l in range(depth):
        lam_init = 0.8 - 0.6 * math.exp(-0.3 * l)
        qt = _matmul_t_heads(xb, w_in_b, l, COL_Q, DIFF_HEADS, 2 * DIFF_DK, batch, seq,
                             DIFF_DK ** -0.5 * LOG2E, BF16)
        k = _matmul_heads(xb, w_in_b, l, COL_K, DIFF_HEADS, 2 * DIFF_DK, seq, BF16)
        vt = _matmul_t_heads(xb, w_in_b, l, COL_V, DIFF_HEADS, DIFF_DV, batch, seq, 1.0, BF16)
        att = _diff_attention(qt, k.reshape(DIFF_HEADS, batch, seq, 2 * DIFF_DK), vt, bias_kt, diff_lambda,
                              vec3(subln_g), l, lam_init).reshape(DIFF_HEADS, batch * seq, DIFF_DV)
        z = _conv_branch(xb, w_in_b, conv_w, l, batch, seq)
        xf, xb = _merge(xf, xb, z, att, mkv, w_in_b, vec3(b_gate), w_conv_out_b, w_diff_out_b,
                        w_mem_out_b, w_o_b, vec3(ln1_g), vec3(ln1_b), l, seq, mem_len, alpha)
        xf, xb = _mlp(xf, xb, w_mlp1_b, w_mlp2_b, vec3(ln2_g), vec3(ln2_b), l, alpha)
    return xf.reshape(batch, seq, d)
```

```python
import functools
import math

import jax
import jax.numpy as jnp
from jax import lax
from jax.experimental import pallas as pl
from jax.experimental.pallas import tpu as pltpu

F32 = jnp.float32
BF16 = jnp.bfloat16

D_MODEL = 1024
CONV_K = 3
DIFF_HEADS = 8
DIFF_DK = 64
DIFF_DV = 2 * DIFF_DK
MEM_HEADS = 4
MEM_DH = 256
N_BRANCH = 3
REL_BUCKETS = 32
REL_MAX_DIST = 128
LN_EPS = 1e-5
LOG2E = math.log2(math.e)
SUM_ROWS = 16

COL_CH, COL_CB, COL_CC = 0, D_MODEL, 2 * D_MODEL
COL_Q, COL_K, COL_V = 3 * D_MODEL, 4 * D_MODEL, 5 * D_MODEL
COL_MQ = 6 * D_MODEL
COL_GATE = 7 * D_MODEL

VMEM_LIMIT = 56 * 1024 * 1024

NT_DIMS = (((1,), (1,)), ((), ()))
TN_DIMS = (((0,), (1,)), ((), ()))


def _params(*sem):
    return pltpu.CompilerParams(dimension_semantics=sem, vmem_limit_bytes=VMEM_LIMIT)


def _layer_norm(y, g, b):
    mu = jnp.mean(y, axis=-1, keepdims=True)
    yc = y - mu
    var = jnp.mean(yc * yc, axis=-1, keepdims=True)
    return yc * lax.rsqrt(var + LN_EPS) * g + b


def _mm_kernel(x_ref, w_ref, o_ref):
    o_ref[...] = jnp.dot(x_ref[...], w_ref[...], preferred_element_type=F32).astype(o_ref.dtype)


def _matmul_layers(x, w_stack, tm, tn, out_dtype):
    m, k = x.shape
    layers, _, n = w_stack.shape
    return pl.pallas_call(
        _mm_kernel,
        grid=(m // tm, layers, n // tn),
        in_specs=[
            pl.BlockSpec((tm, k), lambda i, l, j: (i, 0)),
            pl.BlockSpec((None, k, tn), lambda i, l, j: (l, 0, j)),
        ],
        out_specs=pl.BlockSpec((None, tm, tn), lambda i, l, j: (l, i, j)),
        out_shape=jax.ShapeDtypeStruct((layers, m, n), out_dtype),
        compiler_params=_params("parallel", "arbitrary", "arbitrary"),
        name="matmul_layers",
    )(x, w_stack)


def _mm_heads_kernel(x_ref, w_ref, o_ref):
    heads, _, dh = o_ref.shape
    res = jnp.dot(x_ref[...], w_ref[...], preferred_element_type=F32).astype(o_ref.dtype)
    for h in range(heads):
        o_ref[h] = res[:, h * dh:(h + 1) * dh]


def _matmul_heads(x, w_stack, layer, col0, heads, dh, tm, out_dtype):
    m, k = x.shape
    n = heads * dh
    return pl.pallas_call(
        _mm_heads_kernel,
        grid=(m // tm,),
        in_specs=[
            pl.BlockSpec((tm, k), lambda i: (i, 0)),
            pl.BlockSpec((None, k, n), lambda i: (layer, 0, col0 // n)),
        ],
        out_specs=pl.BlockSpec((heads, tm, dh), lambda i: (0, i, 0)),
        out_shape=jax.ShapeDtypeStruct((heads, m, dh), out_dtype),
        compiler_params=_params("parallel"),
        name="matmul_heads",
    )(x, w_stack)


def _mm_t_heads_kernel(w_ref, x_ref, o_ref, *, scale):
    heads, dh, _ = o_ref.shape
    acc = lax.dot_general(w_ref[...], x_ref[...], TN_DIMS, preferred_element_type=F32)
    res = (acc * scale).astype(o_ref.dtype)
    for h in range(heads):
        o_ref[h] = res[h * dh:(h + 1) * dh, :]


def _matmul_t_heads(x, w_stack, layer, col0, heads, dh, batch, seq, scale, out_dtype):
    _, k = x.shape
    n = heads * dh
    return pl.pallas_call(
        functools.partial(_mm_t_heads_kernel, scale=scale),
        grid=(batch,),
        in_specs=[
            pl.BlockSpec((None, k, n), lambda b: (layer, 0, col0 // n)),
            pl.BlockSpec((seq, k), lambda b: (b, 0)),
        ],
        out_specs=pl.BlockSpec((heads, None, dh, seq), lambda b: (0, b, 0, 0)),
        out_shape=jax.ShapeDtypeStruct((heads, batch, dh, seq), out_dtype),
        compiler_params=_params("parallel"),
        name="matmul_t_heads",
    )(w_stack, x)


def _conv_kernel(x_ref, wh_ref, wb_ref, wc_ref, cw_ref, z_ref, u_ref):
    s = x_ref.shape[0]
    x = x_ref[...]
    pad = jnp.zeros((8, u_ref.shape[1]), F32)
    u_ref[0:8, :] = pad
    u_ref[s + 8:s + 16, :] = pad
    u_ref[8:s + 8, :] = (jnp.dot(x, wc_ref[...], preferred_element_type=F32)
                         * jnp.dot(x, wh_ref[...], preferred_element_type=F32))
    cb = jnp.dot(x, wb_ref[...], preferred_element_type=F32)
    cw = cw_ref[...]
    y = (cw[0:1, :] * u_ref[7:s + 7, :] + cw[1:2, :] * u_ref[8:s + 8, :]
         + cw[2:3, :] * u_ref[9:s + 9, :])
    z_ref[...] = (cb * y).astype(z_ref.dtype)


def _conv_branch(xb, w_in, conv_w, layer, batch, seq, tc=512):
    t, d = xb.shape
    nct = D_MODEL // tc
    return pl.pallas_call(
        _conv_kernel,
        grid=(batch, nct),
        in_specs=[
            pl.BlockSpec((seq, d), lambda b, j: (b, 0)),
            pl.BlockSpec((None, d, tc), lambda b, j: (layer, 0, COL_CH // tc + j)),
            pl.BlockSpec((None, d, tc), lambda b, j: (layer, 0, COL_CB // tc + j)),
            pl.BlockSpec((None, d, tc), lambda b, j: (layer, 0, COL_CC // tc + j)),
            pl.BlockSpec((None, CONV_K, tc), lambda b, j: (layer, 0, j)),
        ],
        out_specs=pl.BlockSpec((seq, tc), lambda b, j: (b, j)),
        out_shape=jax.ShapeDtypeStruct((t, D_MODEL), BF16),
        scratch_shapes=[pltpu.VMEM((seq + 16, tc), F32)],
        compiler_params=_params("parallel", "arbitrary"),
        name="conv_branch",
    )(xb, w_in, w_in, w_in, conv_w)


def _bias_kernel(rel_ref, lo_ref, hi_ref, wbkt_ref, o_ref, win_ref, *, rows):
    h = pl.program_id(0)
    j = pl.program_id(1)
    s, tq = o_ref.shape

    @pl.when(j == 0)
    def _():
        bkt = wbkt_ref[...]
        acc = jnp.zeros(bkt.shape, F32)
        for b in range(REL_BUCKETS):
            acc = jnp.where(bkt == b, rel_ref[b, h], acc)
        win_ref[...] = acc * LOG2E

    def chunk(c, carry):
        r0 = pl.multiple_of(c * rows, rows)
        lo = lo_ref[c, j]

        @pl.when(lo == hi_ref[c, j])
        def _():
            o_ref[pl.ds(r0, rows), :] = jnp.full((rows, tq), rel_ref[lo, h], F32) * LOG2E

        @pl.when(lo != hi_ref[c, j])
        def _():
            w0 = jnp.clip(c * rows - j * tq + REL_MAX_DIST, 0, win_ref.shape[0] - rows)
            o_ref[pl.ds(r0, rows), :] = win_ref[pl.ds(pl.multiple_of(w0, rows), rows), :]

        return carry

    lax.fori_loop(0, s // rows, chunk, 0)


def _rel_bias_kt(rel_bias, seq, tq=256, rows=REL_MAX_DIST):
    wrows = tq + 2 * REL_MAX_DIST
    r = jnp.arange(wrows, dtype=jnp.int32)[:, None]
    c = jnp.arange(tq, dtype=jnp.int32)[None, :]
    win_bkt = _t5_bucket(r - REL_MAX_DIST - c)
    k0 = jnp.arange(seq // rows, dtype=jnp.int32)[:, None, None] * rows
    q0 = jnp.arange(seq // tq, dtype=jnp.int32)[None, :, None] * tq
    span = _t5_bucket(k0 - q0 - (tq - 1) + jnp.arange(rows + tq - 1, dtype=jnp.int32)[None, None, :])
    smem = pl.BlockSpec(memory_space=pltpu.SMEM)
    return pl.pallas_call(
        functools.partial(_bias_kernel, rows=rows),
        grid=(DIFF_HEADS, seq // tq),
        in_specs=[smem, smem, smem, pl.BlockSpec((wrows, tq), lambda h, j: (0, 0))],
        out_specs=pl.BlockSpec((None, seq, tq), lambda h, j: (h, 0, j)),
        out_shape=jax.ShapeDtypeStruct((DIFF_HEADS, seq, seq), F32),
        scratch_shapes=[pltpu.VMEM((wrows, tq), F32)],
        compiler_params=_params("parallel", "arbitrary"),
        name="rel_bias",
    )(rel_bias, span.min(axis=-1), span.max(axis=-1), win_bkt)


def _t5_bucket(rel):
    nb = REL_BUCKETS // 2
    max_exact = nb // 2
    ret = (rel > 0).astype(jnp.int32) * nb
    n = jnp.abs(rel)
    nf = jnp.maximum(n, 1).astype(F32)
    large = max_exact + (jnp.log(nf / max_exact) / math.log(REL_MAX_DIST / max_exact)
                         * (nb - max_exact)).astype(jnp.int32)
    large = jnp.minimum(large, nb - 1)
    return ret + jnp.where(n < max_exact, n, large)


def _diff_attn_kernel(lam_ref, qt_ref, k_ref, bias_ref, vt_ref, vtp_ref, g_ref, o_ref,
                      s0_ref, s1_ref, m0_ref, m1_ref, e_ref, a0_ref, a1_ref, fin_ref, *, lam_init, kc):
    group, _, tq = qt_ref.shape
    s = k_ref.shape[1]
    n = 2 * tq
    step = pl.program_id(0)

    @pl.when(step == 0)
    def _():
        s1_ref[...] = jnp.zeros(s1_ref.shape, F32)
        m1_ref[...] = jnp.zeros(m1_ref.shape, F32)
        a0_ref[...] = jnp.ones(a0_ref.shape, F32)
        fin_ref[...] = jnp.zeros(fin_ref.shape, BF16)

    for j in range(group - 2):
        o_ref[j] = fin_ref[j]

    def sub_tile(j):
        sw_ref, mw_ref, sr_ref, mr_ref = (s0_ref, m0_ref, s1_ref, m1_ref) if j % 2 == 0 else (s1_ref, m1_ref, s0_ref, m0_ref)
        aw_ref, ar_ref = (a1_ref, a0_ref) if j % 2 == 0 else (a0_ref, a1_ref)

        qt = qt_ref[j]
        row = lax.broadcasted_iota(jnp.int32, qt.shape, 0)
        zero = jnp.zeros_like(qt)
        q12 = jnp.concatenate([jnp.where(row < DIFF_DK, qt, zero), jnp.where(row >= DIFF_DK, qt, zero)], axis=1)
        b = bias_ref[...]
        sc = jnp.dot(k_ref[j], q12, preferred_element_type=F32) + jnp.concatenate([b, b], axis=1)
        sw_ref[...] = sc
        mw_ref[...] = jnp.max(sc.reshape(s // 8, 8, n), axis=0)

        m = jnp.max(mr_ref[...], axis=0, keepdims=True)
        for c in range(s // kc):
            sl = slice(c * kc, (c + 1) * kc)
            e_ref[sl, :] = jnp.exp2(sr_ref[sl, :] - m).astype(BF16)
        orow = lax.broadcasted_iota(jnp.int32, (SUM_ROWS, s), 0)
        ones = jnp.where(orow == 0, 1.0, 0.0).astype(BF16)
        vt_prev = vtp_ref[...] if j == 0 else vt_ref[j - 1]
        vt = jnp.concatenate([vt_prev, ones], axis=0)
        aw_ref[...] = jnp.dot(vt, e_ref[...], preferred_element_type=F32)

        acc = ar_ref[...]
        lf = lam_ref[...]
        lam = (jnp.exp(jnp.sum(lf[0:1, :] * lf[1:2, :], axis=-1, keepdims=True))
               - jnp.exp(jnp.sum(lf[2:3, :] * lf[3:4, :], axis=-1, keepdims=True)) + lam_init)
        r = 1.0 / acc[DIFF_DV:DIFF_DV + 1, :]
        ot = acc[:DIFF_DV, :tq] * r[:, :tq] - acc[:DIFF_DV, tq:] * (lam * r[:, tq:])
        o = ot.T
        ms = jnp.mean(o * o, axis=-1, keepdims=True)
        res = (o * lax.rsqrt(ms + LN_EPS) * g_ref[...] * (1.0 - lam_init)).astype(o_ref.dtype)
        if j < 2:
            o_ref[group - 2 + j] = res
        else:
            fin_ref[j - 2] = res

    for j in range(group):
        pl.when(step >= j - group)(functools.partial(sub_tile, j))


def _diff_attention(qt, k, vt, bias_kt, diff_lambda, subln_g, layer, lam_init, tq=512, kc=256, group=4):
    heads, batch, seq, _ = k.shape
    nqt = seq // tq
    nbg = batch // group
    nreal = heads * nqt * nbg

    def dec(gs):
        return gs // (nqt * nbg), (gs // nbg) % nqt, gs % nbg

    cur = lambda gs: dec(jnp.minimum(gs, nreal - 1))
    out = lambda gs: dec(jnp.maximum(gs - 1, 0))

    def qt_map(gs):
        h, i, bg = cur(gs)
        return (h, bg, 0, i)

    def kv_map(gs):
        h, i, bg = cur(gs)
        return (h, bg, 0, 0)

    def bias_map(gs):
        h, i, bg = cur(gs)
        return (h, 0, i)

    def vtp_map(gs):
        h, i, bg = dec(jnp.maximum(gs - 1, 0))
        return (h, bg * group + group - 1, 0, 0)

    def o_map(gs):
        h, i, bg = out(gs)
        return (h, bg, i, 0)

    kern = functools.partial(_diff_attn_kernel, lam_init=lam_init, kc=kc)
    return pl.pallas_call(
        kern,
        grid=(nreal + 1,),
        in_specs=[
            pl.BlockSpec((None, 4, DIFF_DK), lambda gs: (layer, 0, 0)),
            pl.BlockSpec((None, group, DIFF_DV, tq), qt_map),
            pl.BlockSpec((None, group, seq, DIFF_DV), kv_map),
            pl.BlockSpec((None, seq, tq), bias_map),
            pl.BlockSpec((None, group, DIFF_DV, seq), kv_map),
            pl.BlockSpec((None, None, DIFF_DV, seq), vtp_map),
            pl.BlockSpec((None, 1, DIFF_DV), lambda gs: (layer, 0, 0)),
        ],
        out_specs=pl.BlockSpec((None, group, tq, DIFF_DV), o_map),
        out_shape=jax.ShapeDtypeStruct((heads, batch, seq, DIFF_DV), BF16),
        scratch_shapes=[
            pltpu.VMEM((seq, 2 * tq), F32), pltpu.VMEM((seq, 2 * tq), F32),
            pltpu.VMEM((8, 2 * tq), F32), pltpu.VMEM((8, 2 * tq), F32),
            pltpu.VMEM((seq, 2 * tq), BF16),
            pltpu.VMEM((DIFF_DV + SUM_ROWS, 2 * tq), F32), pltpu.VMEM((DIFF_DV + SUM_ROWS, 2 * tq), F32),
            pltpu.VMEM((group - 2, tq, DIFF_DV), BF16),
        ],
        compiler_params=_params("arbitrary"),
        name="diff_attn",
    )(diff_lambda, qt, k, bias_kt, vt, vt, subln_g)


def _merge_kernel(x_ref, xb_ref, z_ref, att_ref, mkv_ref,
                  wmq_ref, wga_ref, wgb_ref, wgc_ref, bg_ref,
                  wca_ref, wdo_ref, wmo_ref, wo_ref, g_ref, b_ref,
                  y_ref, yb_ref, o_scr, *, alpha):
    xb = xb_ref[...]
    mq = (jnp.dot(xb, wmq_ref[...], preferred_element_type=F32) * (MEM_DH ** -0.5)).astype(BF16)
    w = MEM_HEADS * MEM_DH
    for h in range(MEM_HEADS):
        hs = slice(h * MEM_DH, (h + 1) * MEM_DH)
        sc = lax.dot_general(mq[:, hs], mkv_ref[:, hs], NT_DIMS, preferred_element_type=F32)
        e = jnp.exp(sc - jnp.max(sc, axis=-1, keepdims=True))
        p = e * (1.0 / jnp.sum(e, axis=-1, keepdims=True))
        o_scr[:, hs] = jnp.dot(p.astype(BF16), mkv_ref[:, w + h * MEM_DH:w + (h + 1) * MEM_DH],
                               preferred_element_type=F32).astype(BF16)
    out_c = jnp.dot(o_scr[...], wmo_ref[...], preferred_element_type=F32)
    out_a = jnp.dot(z_ref[...], wca_ref[...], preferred_element_type=F32)
    att = jnp.concatenate([att_ref[h] for h in range(att_ref.shape[0])], axis=1)
    out_b = jnp.dot(att, wdo_ref[...], preferred_element_type=F32)
    d = D_MODEL

    def gate(wg_ref, j):
        gl = jnp.dot(xb, wg_ref[...], preferred_element_type=F32) + bg_ref[:, j * d:(j + 1) * d]
        return 1.0 / (1.0 + jnp.exp(-gl))

    merged = gate(wga_ref, 0) * out_a + gate(wgb_ref, 1) * out_b + gate(wgc_ref, 2) * out_c
    hmix = jnp.dot(merged.astype(BF16), wo_ref[...], preferred_element_type=F32)
    y = _layer_norm(alpha * x_ref[...] + hmix, g_ref[...], b_ref[...])
    y_ref[...] = y
    yb_ref[...] = y.astype(BF16)


def _merge(x, xb, z, att, mkv, w_in, b_gate, w_conv_out, w_diff_out, w_mem_out, w_o, ln_g, ln_b,
           layer, seq, mem_len, alpha, tm=512):
    t, d = x.shape
    per_b = seq // tm
    row = lambda i: (i, 0)
    wspec = lambda: pl.BlockSpec((None, d, d), lambda i: (layer, 0, 0), pipeline_mode=pl.Buffered(1))
    win = lambda col: pl.BlockSpec((None, d, d), lambda i: (layer, 0, col // d), pipeline_mode=pl.Buffered(1))
    vec = lambda n: pl.BlockSpec((None, 1, n), lambda i: (layer, 0, 0))
    kern = functools.partial(_merge_kernel, alpha=alpha)
    return pl.pallas_call(
        kern,
        grid=(t // tm,),
        in_specs=[
            pl.BlockSpec((tm, d), row),
            pl.BlockSpec((tm, d), row),
            pl.BlockSpec((tm, d), row),
            pl.BlockSpec((DIFF_HEADS, tm, DIFF_DV), lambda i: (0, i, 0)),
            pl.BlockSpec((None, mem_len, 2 * d), lambda i: (layer, i // per_b, 0)),
            win(COL_MQ), win(COL_GATE), win(COL_GATE + d), win(COL_GATE + 2 * d),
            vec(N_BRANCH * d),
            wspec(), wspec(), wspec(), wspec(),
            vec(d), vec(d),
        ],
        out_specs=[pl.BlockSpec((tm, d), row), pl.BlockSpec((tm, d), row)],
        out_shape=[jax.ShapeDtypeStruct((t, d), F32), jax.ShapeDtypeStruct((t, d), BF16)],
        scratch_shapes=[pltpu.VMEM((tm, d), BF16)],
        compiler_params=_params("parallel"),
        name="merge",
    )(x, xb, z, att, mkv, w_in, w_in, w_in, w_in, b_gate,
      w_conv_out, w_diff_out, w_mem_out, w_o, ln_g, ln_b)


def _mlp_kernel(x_ref, xb_ref, w1_ref, w2_ref, g_ref, b_ref, y_ref, yb_ref, *, alpha, fc):
    xb = xb_ref[...]
    f = jnp.zeros(x_ref.shape, F32)
    for c in range(w1_ref.shape[1] // fc):
        cs = slice(c * fc, (c + 1) * fc)
        hid = jnp.maximum(jnp.dot(xb, w1_ref[:, cs], preferred_element_type=F32), 0.0)
        f = f + jnp.dot((hid * hid).astype(BF16), w2_ref[cs, :], preferred_element_type=F32)
    y = _layer_norm(alpha * x_ref[...] + f, g_ref[...], b_ref[...])
    y_ref[...] = y
    yb_ref[...] = y.astype(BF16)


def _mlp(x, xb, w1, w2, ln_g, ln_b, layer, alpha, tm=1024, fc=1024):
    t, d = x.shape
    dff = w1.shape[2]
    row = lambda i: (i, 0)
    vec = lambda n: pl.BlockSpec((None, 1, n), lambda i: (layer, 0, 0))
    kern = functools.partial(_mlp_kernel, alpha=alpha, fc=fc)
    return pl.pallas_call(
        kern,
        grid=(t // tm,),
        in_specs=[
            pl.BlockSpec((tm, d), row),
            pl.BlockSpec((tm, d), row),
            pl.BlockSpec((None, d, dff), lambda i: (layer, 0, 0), pipeline_mode=pl.Buffered(1)),
            pl.BlockSpec((None, dff, d), lambda i: (layer, 0, 0), pipeline_mode=pl.Buffered(1)),
            vec(d), vec(d),
        ],
        out_specs=[pl.BlockSpec((tm, d), row), pl.BlockSpec((tm, d), row)],
        out_shape=[jax.ShapeDtypeStruct((t, d), F32), jax.ShapeDtypeStruct((t, d), BF16)],
        compiler_params=_params("parallel"),
        name="mlp",
    )(x, xb, w1, w2, ln_g, ln_b)


def kernel(x, mem, w_in, b_gate, conv_w, w_conv_out, diff_lambda, subln_g, w_diff_out, rel_bias,
           w_mem_kv, w_mem_out, w_o, ln1_g, ln1_b, w_mlp1, w_mlp2, ln2_g, ln2_b):
    batch, seq, d = x.shape
    mem_len = mem.shape[1]
    depth = w_in.shape[0]
    alpha = (2 * depth) ** 0.25

    w_in_b = w_in.astype(BF16)
    w_conv_out_b = w_conv_out.astype(BF16)
    w_diff_out_b = w_diff_out.astype(BF16)
    w_mem_kv_b = w_mem_kv.astype(BF16)
    w_mem_out_b = w_mem_out.astype(BF16)
    w_o_b = w_o.astype(BF16)
    w_mlp1_b = w_mlp1.astype(BF16)
    w_mlp2_b = w_mlp2.astype(BF16)
    mem_b = mem.reshape(batch * mem_len, d).astype(BF16)
    vec3 = lambda a: a.reshape(depth, 1, a.shape[-1])

    bias_kt = _rel_bias_kt(rel_bias, seq)

    mkv = _matmul_layers(mem_b, w_mem_kv_b, 1024, 1024, BF16)

    xf = x.reshape(batch * seq, d)
    xb = xf.astype(BF16)
    for l in range(depth):
        lam_init = 0.8 - 0.6 * math.exp(-0.3 * l)
        qt = _matmul_t_heads(xb, w_in_b, l, COL_Q, DIFF_HEADS, 2 * DIFF_DK, batch, seq,
                             DIFF_DK ** -0.5 * LOG2E, BF16)
        k = _matmul_heads(xb, w_in_b, l, COL_K, DIFF_HEADS, 2 * DIFF_DK, seq, BF16)
        vt = _matmul_t_heads(xb, w_in_b, l, COL_V, DIFF_HEADS, DIFF_DV, batch, seq, 1.0, BF16)
        att = _diff_attention(qt, k.reshape(DIFF_HEADS, batch, seq, 2 * DIFF_DK), vt, bias_kt, diff_lambda,
                              vec3(subln_g), l, lam_init).reshape(DIFF_HEADS, batch * seq, DIFF_DV)
        z = _conv_branch(xb, w_in_b, conv_w, l, batch, seq)
        xf, xb = _merge(xf, xb, z, att, mkv, w_in_b, vec3(b_gate), w_conv_out_b, w_diff_out_b,
                        w_mem_out_b, w_o_b, vec3(ln1_g), vec3(ln1_b), l, seq, mem_len, alpha)
        xf, xb = _mlp(xf, xb, w_mlp1_b, w_mlp2_b, vec3(ln2_g), vec3(ln2_b), l, alpha)
    return xf.reshape(batch, seq, d)
```

```python
import functools
import math

import jax
import jax.numpy as jnp
from jax import lax
from jax.experimental import pallas as pl
from jax.experimental.pallas import tpu as pltpu

F32 = jnp.float32
BF16 = jnp.bfloat16

D_MODEL = 1024
CONV_K = 3
DIFF_HEADS = 8
DIFF_DK = 64
DIFF_DV = 2 * DIFF_DK
MEM_HEADS = 4
MEM_DH = 256
N_BRANCH = 3
REL_BUCKETS = 32
REL_MAX_DIST = 128
LN_EPS = 1e-5
LOG2E = math.log2(math.e)
SUM_ROWS = 16

COL_CH, COL_CB, COL_CC = 0, D_MODEL, 2 * D_MODEL
COL_Q, COL_K, COL_V = 3 * D_MODEL, 4 * D_MODEL, 5 * D_MODEL
COL_MQ = 6 * D_MODEL
COL_GATE = 7 * D_MODEL

VMEM_LIMIT = 56 * 1024 * 1024

NT_DIMS = (((1,), (1,)), ((), ()))
TN_DIMS = (((0,), (1,)), ((), ()))


def _params(*sem):
    return pltpu.CompilerParams(dimension_semantics=sem, vmem_limit_bytes=VMEM_LIMIT)


def _layer_norm(y, g, b):
    mu = jnp.mean(y, axis=-1, keepdims=True)
    yc = y - mu
    var = jnp.mean(yc * yc, axis=-1, keepdims=True)
    return yc * lax.rsqrt(var + LN_EPS) * g + b


def _mm_kernel(x_ref, w_ref, o_ref):
    o_ref[...] = jnp.dot(x_ref[...], w_ref[...], preferred_element_type=F32).astype(o_ref.dtype)


def _matmul_layers(x, w_stack, tm, tn, out_dtype):
    m, k = x.shape
    layers, _, n = w_stack.shape
    return pl.pallas_call(
        _mm_kernel,
        grid=(m // tm, layers, n // tn),
        in_specs=[
            pl.BlockSpec((tm, k), lambda i, l, j: (i, 0)),
            pl.BlockSpec((None, k, tn), lambda i, l, j: (l, 0, j)),
        ],
        out_specs=pl.BlockSpec((None, tm, tn), lambda i, l, j: (l, i, j)),
        out_shape=jax.ShapeDtypeStruct((layers, m, n), out_dtype),
        compiler_params=_params("parallel", "arbitrary", "arbitrary"),
        name="matmul_layers",
    )(x, w_stack)


def _mm_heads_kernel(x_ref, w_ref, o_ref):
    heads, _, dh = o_ref.shape
    res = jnp.dot(x_ref[...], w_ref[...], preferred_element_type=F32).astype(o_ref.dtype)
    for h in range(heads):
        o_ref[h] = res[:, h * dh:(h + 1) * dh]


def _matmul_heads(x, w_stack, layer, col0, heads, dh, tm, out_dtype):
    m, k = x.shape
    n = heads * dh
    return pl.pallas_call(
        _mm_heads_kernel,
        grid=(m // tm,),
        in_specs=[
            pl.BlockSpec((tm, k), lambda i: (i, 0)),
            pl.BlockSpec((None, k, n), lambda i: (layer, 0, col0 // n)),
        ],
        out_specs=pl.BlockSpec((heads, tm, dh), lambda i: (0, i, 0)),
        out_shape=jax.ShapeDtypeStruct((heads, m, dh), out_dtype),
        compiler_params=_params("parallel"),
        name="matmul_heads",
    )(x, w_stack)


def _mm_t_heads_kernel(w_ref, x_ref, o_ref, *, scale):
    heads, dh, _ = o_ref.shape
    acc = lax.dot_general(w_ref[...], x_ref[...], TN_DIMS, preferred_element_type=F32)
    res = (acc * scale).astype(o_ref.dtype)
    for h in range(heads):
        o_ref[h] = res[h * dh:(h + 1) * dh, :]


def _matmul_t_heads(x, w_stack, layer, col0, heads, dh, batch, seq, scale, out_dtype):
    _, k = x.shape
    n = heads * dh
    return pl.pallas_call(
        functools.partial(_mm_t_heads_kernel, scale=scale),
        grid=(batch,),
        in_specs=[
            pl.BlockSpec((None, k, n), lambda b: (layer, 0, col0 // n)),
            pl.BlockSpec((seq, k), lambda b: (b, 0)),
        ],
        out_specs=pl.BlockSpec((heads, None, dh, seq), lambda b: (0, b, 0, 0)),
        out_shape=jax.ShapeDtypeStruct((heads, batch, dh, seq), out_dtype),
        compiler_params=_params("parallel"),
        name="matmul_t_heads",
    )(w_stack, x)


def _conv_kernel(x_ref, wh_ref, wb_ref, wc_ref, cw_ref, z_ref, u_ref, *, nsplit):
    s = x_ref.shape[0]
    rows = s // nsplit
    pad = jnp.zeros((8, u_ref.shape[1]), F32)
    u_ref[0:8, :] = pad
    u_ref[s + 8:s + 16, :] = pad
    for r in range(nsplit):
        x = x_ref[r * rows:(r + 1) * rows, :]
        u_ref[8 + r * rows:8 + (r + 1) * rows, :] = (jnp.dot(x, wc_ref[...], preferred_element_type=F32)
                                                     * jnp.dot(x, wh_ref[...], preferred_element_type=F32))
    cw = cw_ref[...]
    for r in range(nsplit):
        lo = r * rows
        cb = jnp.dot(x_ref[lo:lo + rows, :], wb_ref[...], preferred_element_type=F32)
        y = (cw[0:1, :] * u_ref[lo + 7:lo + rows + 7, :] + cw[1:2, :] * u_ref[lo + 8:lo + rows + 8, :]
             + cw[2:3, :] * u_ref[lo + 9:lo + rows + 9, :])
        z_ref[lo:lo + rows, :] = (cb * y).astype(z_ref.dtype)


def _conv_branch(xb, w_in, conv_w, layer, batch, seq, tc=512, nsplit=2):
    t, d = xb.shape
    nct = D_MODEL // tc
    return pl.pallas_call(
        functools.partial(_conv_kernel, nsplit=nsplit),
        grid=(batch, nct),
        in_specs=[
            pl.BlockSpec((seq, d), lambda b, j: (b, 0)),
            pl.BlockSpec((None, d, tc), lambda b, j: (layer, 0, COL_CH // tc + j)),
            pl.BlockSpec((None, d, tc), lambda b, j: (layer, 0, COL_CB // tc + j)),
            pl.BlockSpec((None, d, tc), lambda b, j: (layer, 0, COL_CC // tc + j)),
            pl.BlockSpec((None, CONV_K, tc), lambda b, j: (layer, 0, j)),
        ],
        out_specs=pl.BlockSpec((seq, tc), lambda b, j: (b, j)),
        out_shape=jax.ShapeDtypeStruct((t, D_MODEL), BF16),
        scratch_shapes=[pltpu.VMEM((seq + 16, tc), F32)],
        compiler_params=_params("parallel", "arbitrary"),
        name="conv_branch",
    )(xb, w_in, w_in, w_in, conv_w)


def _bias_kernel(rel_ref, lo_ref, hi_ref, wbkt_ref, o_ref, win_ref, *, rows):
    h = pl.program_id(0)
    j = pl.program_id(1)
    s, tq = o_ref.shape

    @pl.when(j == 0)
    def _():
        bkt = wbkt_ref[...]
        acc = jnp.zeros(bkt.shape, F32)
        for b in range(REL_BUCKETS):
            acc = jnp.where(bkt == b, rel_ref[b, h], acc)
        win_ref[...] = acc * LOG2E

    def chunk(c, carry):
        r0 = pl.multiple_of(c * rows, rows)
        lo = lo_ref[c, j]

        @pl.when(lo == hi_ref[c, j])
        def _():
            o_ref[pl.ds(r0, rows), :] = jnp.full((rows, tq), rel_ref[lo, h], F32) * LOG2E

        @pl.when(lo != hi_ref[c, j])
        def _():
            w0 = jnp.clip(c * rows - j * tq + REL_MAX_DIST, 0, win_ref.shape[0] - rows)
            o_ref[pl.ds(r0, rows), :] = win_ref[pl.ds(pl.multiple_of(w0, rows), rows), :]

        return carry

    lax.fori_loop(0, s // rows, chunk, 0)


def _rel_bias_kt(rel_bias, seq, tq=256, rows=REL_MAX_DIST):
    wrows = tq + 2 * REL_MAX_DIST
    r = jnp.arange(wrows, dtype=jnp.int32)[:, None]
    c = jnp.arange(tq, dtype=jnp.int32)[None, :]
    win_bkt = _t5_bucket(r - REL_MAX_DIST - c)
    k0 = jnp.arange(seq // rows, dtype=jnp.int32)[:, None, None] * rows
    q0 = jnp.arange(seq // tq, dtype=jnp.int32)[None, :, None] * tq
    span = _t5_bucket(k0 - q0 - (tq - 1) + jnp.arange(rows + tq - 1, dtype=jnp.int32)[None, None, :])
    smem = pl.BlockSpec(memory_space=pltpu.SMEM)
    return pl.pallas_call(
        functools.partial(_bias_kernel, rows=rows),
        grid=(DIFF_HEADS, seq // tq),
        in_specs=[smem, smem, smem, pl.BlockSpec((wrows, tq), lambda h, j: (0, 0))],
        out_specs=pl.BlockSpec((None, seq, tq), lambda h, j: (h, 0, j)),
        out_shape=jax.ShapeDtypeStruct((DIFF_HEADS, seq, seq), F32),
        scratch_shapes=[pltpu.VMEM((wrows, tq), F32)],
        compiler_params=_params("parallel", "arbitrary"),
        name="rel_bias",
    )(rel_bias, span.min(axis=-1), span.max(axis=-1), win_bkt)


def _t5_bucket(rel):
    nb = REL_BUCKETS // 2
    max_exact = nb // 2
    ret = (rel > 0).astype(jnp.int32) * nb
    n = jnp.abs(rel)
    nf = jnp.maximum(n, 1).astype(F32)
    large = max_exact + (jnp.log(nf / max_exact) / math.log(REL_MAX_DIST / max_exact)
                         * (nb - max_exact)).astype(jnp.int32)
    large = jnp.minimum(large, nb - 1)
    return ret + jnp.where(n < max_exact, n, large)


def _diff_attn_kernel(lam_ref, qt_ref, k_ref, bias_ref, vt_ref, vtp_ref, g_ref, o_ref,
                      s0_ref, s1_ref, m0_ref, m1_ref, e_ref, a0_ref, a1_ref, fin_ref, *, lam_init, kc):
    group, _, tq = qt_ref.shape
    s = k_ref.shape[1]
    n = 2 * tq
    step = pl.program_id(0)

    @pl.when(step == 0)
    def _():
        s1_ref[...] = jnp.zeros(s1_ref.shape, F32)
        m1_ref[...] = jnp.zeros(m1_ref.shape, F32)
        a0_ref[...] = jnp.ones(a0_ref.shape, F32)
        fin_ref[...] = jnp.zeros(fin_ref.shape, BF16)

    for j in range(group - 2):
        o_ref[j] = fin_ref[j]

    def sub_tile(j):
        sw_ref, mw_ref, sr_ref, mr_ref = (s0_ref, m0_ref, s1_ref, m1_ref) if j % 2 == 0 else (s1_ref, m1_ref, s0_ref, m0_ref)
        aw_ref, ar_ref = (a1_ref, a0_ref) if j % 2 == 0 else (a0_ref, a1_ref)

        qt = qt_ref[j]
        row = lax.broadcasted_iota(jnp.int32, qt.shape, 0)
        zero = jnp.zeros_like(qt)
        q12 = jnp.concatenate([jnp.where(row < DIFF_DK, qt, zero), jnp.where(row >= DIFF_DK, qt, zero)], axis=1)
        b = bias_ref[...]
        sc = jnp.dot(k_ref[j], q12, preferred_element_type=F32) + jnp.concatenate([b, b], axis=1)
        sw_ref[...] = sc
        mw_ref[...] = jnp.max(sc.reshape(s // 8, 8, n), axis=0)

        m = jnp.max(mr_ref[...], axis=0, keepdims=True)
        for c in range(s // kc):
            sl = slice(c * kc, (c + 1) * kc)
            e_ref[sl, :] = jnp.exp2(sr_ref[sl, :] - m).astype(BF16)
        orow = lax.broadcasted_iota(jnp.int32, (SUM_ROWS, s), 0)
        ones = jnp.where(orow == 0, 1.0, 0.0).astype(BF16)
        vt_prev = vtp_ref[...] if j == 0 else vt_ref[j - 1]
        vt = jnp.concatenate([vt_prev, ones], axis=0)
        aw_ref[...] = jnp.dot(vt, e_ref[...], preferred_element_type=F32)

        acc = ar_ref[...]
        lf = lam_ref[...]
        lam = (jnp.exp(jnp.sum(lf[0:1, :] * lf[1:2, :], axis=-1, keepdims=True))
               - jnp.exp(jnp.sum(lf[2:3, :] * lf[3:4, :], axis=-1, keepdims=True)) + lam_init)
        r = 1.0 / acc[DIFF_DV:DIFF_DV + 1, :]
        ot = acc[:DIFF_DV, :tq] * r[:, :tq] - acc[:DIFF_DV, tq:] * (lam * r[:, tq:])
        o = ot.T
        ms = jnp.mean(o * o, axis=-1, keepdims=True)
        res = (o * lax.rsqrt(ms + LN_EPS) * g_ref[...] * (1.0 - lam_init)).astype(o_ref.dtype)
        if j < 2:
            o_ref[group - 2 + j] = res
        else:
            fin_ref[j - 2] = res

    for j in range(group):
        pl.when(step >= j - group)(functools.partial(sub_tile, j))


def _diff_attention(qt, k, vt, bias_kt, diff_lambda, subln_g, layer, lam_init, tq=512, kc=256, group=4):
    heads, batch, seq, _ = k.shape
    nqt = seq // tq
    nbg = batch // group
    nreal = heads * nqt * nbg

    def dec(gs):
        return gs // (nqt * nbg), (gs // nbg) % nqt, gs % nbg

    cur = lambda gs: dec(jnp.minimum(gs, nreal - 1))
    out = lambda gs: dec(jnp.maximum(gs - 1, 0))

    def qt_map(gs):
        h, i, bg = cur(gs)
        return (h, bg, 0, i)

    def kv_map(gs):
        h, i, bg = cur(gs)
        return (h, bg, 0, 0)

    def bias_map(gs):
        h, i, bg = cur(gs)
        return (h, 0, i)

    def vtp_map(gs):
        h, i, bg = dec(jnp.maximum(gs - 1, 0))
        return (h, bg * group + group - 1, 0, 0)

    def o_map(gs):
        h, i, bg = out(gs)
        return (h, bg, i, 0)

    kern = functools.partial(_diff_attn_kernel, lam_init=lam_init, kc=kc)
    return pl.pallas_call(
        kern,
        grid=(nreal + 1,),
        in_specs=[
            pl.BlockSpec((None, 4, DIFF_DK), lambda gs: (layer, 0, 0)),
            pl.BlockSpec((None, group, DIFF_DV, tq), qt_map),
            pl.BlockSpec((None, group, seq, DIFF_DV), kv_map),
            pl.BlockSpec((None, seq, tq), bias_map),
            pl.BlockSpec((None, group, DIFF_DV, seq), kv_map),
            pl.BlockSpec((None, None, DIFF_DV, seq), vtp_map),
            pl.BlockSpec((None, 1, DIFF_DV), lambda gs: (layer, 0, 0)),
        ],
        out_specs=pl.BlockSpec((None, group, tq, DIFF_DV), o_map),
        out_shape=jax.ShapeDtypeStruct((heads, batch, seq, DIFF_DV), BF16),
        scratch_shapes=[
            pltpu.VMEM((seq, 2 * tq), F32), pltpu.VMEM((seq, 2 * tq), F32),
            pltpu.VMEM((8, 2 * tq), F32), pltpu.VMEM((8, 2 * tq), F32),
            pltpu.VMEM((seq, 2 * tq), BF16),
            pltpu.VMEM((DIFF_DV + SUM_ROWS, 2 * tq), F32), pltpu.VMEM((DIFF_DV + SUM_ROWS, 2 * tq), F32),
            pltpu.VMEM((group - 2, tq, DIFF_DV), BF16),
        ],
        compiler_params=_params("arbitrary"),
        name="diff_attn",
    )(diff_lambda, qt, k, bias_kt, vt, vt, subln_g)


def _merge_kernel(x_ref, xb_ref, z_ref, att_ref, mkv_ref,
                  wmq_ref, wga_ref, wgb_ref, wgc_ref, bg_ref,
                  wca_ref, wdo_ref, wmo_ref, wo_ref, g_ref, b_ref,
                  y_ref, yb_ref, o_scr, *, alpha, nsplit):
    xb = xb_ref[...]
    d = D_MODEL
    w = MEM_HEADS * MEM_DH

    def gate(wg_ref, j):
        gl = jnp.dot(xb, wg_ref[...], preferred_element_type=F32) + bg_ref[:, j * d:(j + 1) * d]
        return 1.0 / (1.0 + jnp.exp(-gl))

    mq = (jnp.dot(xb, wmq_ref[...], preferred_element_type=F32) * (MEM_DH ** -0.5)).astype(BF16)
    scs = [lax.dot_general(mq[:, h * MEM_DH:(h + 1) * MEM_DH], mkv_ref[:, h * MEM_DH:(h + 1) * MEM_DH], NT_DIMS,
                           preferred_element_type=F32) for h in range(MEM_HEADS)]
    ga_a = gate(wga_ref, 0) * jnp.dot(z_ref[...], wca_ref[...], preferred_element_type=F32)
    ps = []
    for sc in scs:
        e = jnp.exp(sc - jnp.max(sc, axis=-1, keepdims=True))
        ps.append((e * (1.0 / jnp.sum(e, axis=-1, keepdims=True))).astype(BF16))
    att = jnp.concatenate([att_ref[h] for h in range(att_ref.shape[0])], axis=1)
    gb_b = gate(wgb_ref, 1) * jnp.dot(att, wdo_ref[...], preferred_element_type=F32)
    for h in range(MEM_HEADS):
        o_scr[:, h * MEM_DH:(h + 1) * MEM_DH] = jnp.dot(ps[h], mkv_ref[:, w + h * MEM_DH:w + (h + 1) * MEM_DH],
                                                         preferred_element_type=F32).astype(BF16)
    gc = gate(wgc_ref, 2)
    out_c = jnp.dot(o_scr[...], wmo_ref[...], preferred_element_type=F32)
    merged = (ga_a + gb_b + gc * out_c).astype(BF16)
    rows = merged.shape[0] // nsplit
    for r in range(nsplit):
        rs = slice(r * rows, (r + 1) * rows)
        hmix = jnp.dot(merged[rs], wo_ref[...], preferred_element_type=F32)
        y = _layer_norm(alpha * x_ref[rs, :] + hmix, g_ref[...], b_ref[...])
        y_ref[rs, :] = y
        yb_ref[rs, :] = y.astype(BF16)


def _merge(x, xb, z, att, mkv, w_in, b_gate, w_conv_out, w_diff_out, w_mem_out, w_o, ln_g, ln_b,
           layer, seq, mem_len, alpha, tm=512, nsplit=2):
    t, d = x.shape
    per_b = seq // tm
    row = lambda i: (i, 0)
    wspec = lambda: pl.BlockSpec((None, d, d), lambda i: (layer, 0, 0), pipeline_mode=pl.Buffered(1))
    win = lambda col: pl.BlockSpec((None, d, d), lambda i: (layer, 0, col // d), pipeline_mode=pl.Buffered(1))
    vec = lambda n: pl.BlockSpec((None, 1, n), lambda i: (layer, 0, 0))
    kern = functools.partial(_merge_kernel, alpha=alpha, nsplit=nsplit)
    return pl.pallas_call(
        kern,
        grid=(t // tm,),
        in_specs=[
            pl.BlockSpec((tm, d), row),
            pl.BlockSpec((tm, d), row),
            pl.BlockSpec((tm, d), row),
            pl.BlockSpec((DIFF_HEADS, tm, DIFF_DV), lambda i: (0, i, 0)),
            pl.BlockSpec((None, mem_len, 2 * d), lambda i: (layer, i // per_b, 0)),
            win(COL_MQ), win(COL_GATE), win(COL_GATE + d), win(COL_GATE + 2 * d),
            vec(N_BRANCH * d),
            wspec(), wspec(), wspec(), wspec(),
            vec(d), vec(d),
        ],
        out_specs=[pl.BlockSpec((tm, d), row), pl.BlockSpec((tm, d), row)],
        out_shape=[jax.ShapeDtypeStruct((t, d), F32), jax.ShapeDtypeStruct((t, d), BF16)],
        scratch_shapes=[pltpu.VMEM((tm, d), BF16)],
        compiler_params=_params("parallel"),
        name="merge",
    )(x, xb, z, att, mkv, w_in, w_in, w_in, w_in, b_gate,
      w_conv_out, w_diff_out, w_mem_out, w_o, ln_g, ln_b)


def _mlp_kernel(x_ref, xb_ref, w1_ref, w2_ref, g_ref, b_ref, y_ref, yb_ref, *, alpha, fc, nsplit):
    rows = x_ref.shape[0] // nsplit
    for r in range(nsplit):
        rs = slice(r * rows, (r + 1) * rows)
        xb = xb_ref[rs, :]
        f = jnp.zeros((rows, x_ref.shape[1]), F32)
        for c in range(w1_ref.shape[1] // fc):
            cs = slice(c * fc, (c + 1) * fc)
            hid = jnp.maximum(jnp.dot(xb, w1_ref[:, cs], preferred_element_type=F32), 0.0)
            f = f + jnp.dot((hid * hid).astype(BF16), w2_ref[cs, :], preferred_element_type=F32)
        y = _layer_norm(alpha * x_ref[rs, :] + f, g_ref[...], b_ref[...])
        y_ref[rs, :] = y
        yb_ref[rs, :] = y.astype(BF16)


def _mlp(x, xb, w1, w2, ln_g, ln_b, layer, alpha, tm=1024, fc=2048, nsplit=2):
    t, d = x.shape
    dff = w1.shape[2]
    row = lambda i: (i, 0)
    vec = lambda n: pl.BlockSpec((None, 1, n), lambda i: (layer, 0, 0))
    kern = functools.partial(_mlp_kernel, alpha=alpha, fc=fc, nsplit=nsplit)
    return pl.pallas_call(
        kern,
        grid=(t // tm,),
        in_specs=[
            pl.BlockSpec((tm, d), row),
            pl.BlockSpec((tm, d), row),
            pl.BlockSpec((None, d, dff), lambda i: (layer, 0, 0), pipeline_mode=pl.Buffered(1)),
            pl.BlockSpec((None, dff, d), lambda i: (layer, 0, 0), pipeline_mode=pl.Buffered(1)),
            vec(d), vec(d),
        ],
        out_specs=[pl.BlockSpec((tm, d), row), pl.BlockSpec((tm, d), row)],
        out_shape=[jax.ShapeDtypeStruct((t, d), F32), jax.ShapeDtypeStruct((t, d), BF16)],
        compiler_params=_params("parallel"),
        name="mlp",
    )(x, xb, w1, w2, ln_g, ln_b)


def kernel(x, mem, w_in, b_gate, conv_w, w_conv_out, diff_lambda, subln_g, w_diff_out, rel_bias,
           w_mem_kv, w_mem_out, w_o, ln1_g, ln1_b, w_mlp1, w_mlp2, ln2_g, ln2_b):
    batch, seq, d = x.shape
    mem_len = mem.shape[1]
    depth = w_in.shape[0]
    alpha = (2 * depth) ** 0.25

    w_in_b = w_in.astype(BF16)
    w_conv_out_b = w_conv_out.astype(BF16)
    w_diff_out_b = w_diff_out.astype(BF16)
    w_mem_kv_b = w_mem_kv.astype(BF16)
    w_mem_out_b = w_mem_out.astype(BF16)
    w_o_b = w_o.astype(BF16)
    w_mlp1_b = w_mlp1.astype(BF16)
    w_mlp2_b = w_mlp2.astype(BF16)
    mem_b = mem.reshape(batch * mem_len, d).astype(BF16)
    vec3 = lambda a: a.reshape(depth, 1, a.shape[-1])

    bias_kt = _rel_bias_kt(rel_bias, seq)

    mkv = _matmul_layers(mem_b, w_mem_kv_b, 1024, 1024, BF16)

    xf = x.reshape(batch * seq, d)
    xb = xf.astype(BF16)
    for l in range(depth):
        lam_init = 0.8 - 0.6 * math.exp(-0.3 * l)
        qt = _matmul_t_heads(xb, w_in_b, l, COL_Q, DIFF_HEADS, 2 * DIFF_DK, batch, seq,
                             DIFF_DK ** -0.5 * LOG2E, BF16)
        k = _matmul_heads(xb, w_in_b, l, COL_K, DIFF_HEADS, 2 * DIFF_DK, seq, BF16)
        vt = _matmul_t_heads(xb, w_in_b, l, COL_V, DIFF_HEADS, DIFF_DV, batch, seq, 1.0, BF16)
        att = _diff_attention(qt, k.reshape(DIFF_HEADS, batch, seq, 2 * DIFF_DK), vt, bias_kt, diff_lambda,
                              vec3(subln_g), l, lam_init).reshape(DIFF_HEADS, batch * seq, DIFF_DV)
        z = _conv_branch(xb, w_in_b, conv_w, l, batch, seq)
        xf, xb = _merge(xf, xb, z, att, mkv, w_in_b, vec3(b_gate), w_conv_out_b, w_diff_out_b,
                        w_mem_out_b, w_o_b, vec3(ln1_g), vec3(ln1_b), l, seq, mem_len, alpha)
        xf, xb = _mlp(xf, xb, w_mlp1_b, w_mlp2_b, vec3(ln2_g), vec3(ln2_b), l, alpha)
    return xf.reshape(batch, seq, d)
```

```python
import functools
import math

import jax
import jax.numpy as jnp
from jax import lax
from jax.experimental import pallas as pl
from jax.experimental.pallas import tpu as pltpu

F32 = jnp.float32
BF16 = jnp.bfloat16

D_MODEL = 1024
CONV_K = 3
DIFF_HEADS = 8
DIFF_DK = 64
DIFF_DV = 2 * DIFF_DK
MEM_HEADS = 4
MEM_DH = 256
N_BRANCH = 3
REL_BUCKETS = 32
REL_MAX_DIST = 128
LN_EPS = 1e-5
LOG2E = math.log2(math.e)
SUM_ROWS = 16

COL_CH, COL_CB, COL_CC = 0, D_MODEL, 2 * D_MODEL
COL_Q, COL_K, COL_V = 3 * D_MODEL, 4 * D_MODEL, 5 * D_MODEL
COL_MQ = 6 * D_MODEL
COL_GATE = 7 * D_MODEL

VMEM_LIMIT = 56 * 1024 * 1024

NT_DIMS = (((1,), (1,)), ((), ()))
TN_DIMS = (((0,), (1,)), ((), ()))


def _params(*sem):
    return pltpu.CompilerParams(dimension_semantics=sem, vmem_limit_bytes=VMEM_LIMIT)


def _layer_norm(y, g, b):
    mu = jnp.mean(y, axis=-1, keepdims=True)
    yc = y - mu
    var = jnp.mean(yc * yc, axis=-1, keepdims=True)
    return yc * lax.rsqrt(var + LN_EPS) * g + b


def _mm_kernel(x_ref, w_ref, o_ref):
    o_ref[...] = jnp.dot(x_ref[...], w_ref[...], preferred_element_type=F32).astype(o_ref.dtype)


def _matmul_layers(x, w_stack, tm, tn, out_dtype):
    m, k = x.shape
    layers, _, n = w_stack.shape
    return pl.pallas_call(
        _mm_kernel,
        grid=(m // tm, layers, n // tn),
        in_specs=[
            pl.BlockSpec((tm, k), lambda i, l, j: (i, 0)),
            pl.BlockSpec((None, k, tn), lambda i, l, j: (l, 0, j)),
        ],
        out_specs=pl.BlockSpec((None, tm, tn), lambda i, l, j: (l, i, j)),
        out_shape=jax.ShapeDtypeStruct((layers, m, n), out_dtype),
        compiler_params=_params("parallel", "arbitrary", "arbitrary"),
        name="matmul_layers",
    )(x, w_stack)


def _qkv_kernel(x_ref, wq_ref, wk_ref, wv_ref, qt_ref, k_ref, vt_ref, *, q_scale):
    heads, dh, _ = qt_ref.shape
    x = x_ref[...]
    qt = (lax.dot_general(wq_ref[...], x, TN_DIMS, preferred_element_type=F32) * q_scale).astype(qt_ref.dtype)
    kk = jnp.dot(x, wk_ref[...], preferred_element_type=F32).astype(k_ref.dtype)
    vt = lax.dot_general(wv_ref[...], x, TN_DIMS, preferred_element_type=F32).astype(vt_ref.dtype)
    for h in range(heads):
        qt_ref[h] = qt[h * dh:(h + 1) * dh, :]
        k_ref[h] = kk[:, h * dh:(h + 1) * dh]
        vt_ref[h] = vt[h * dh:(h + 1) * dh, :]


def _qkv_projection(x, w_stack, layer, heads, dh, batch, seq, q_scale, tm=2048):
    _, k = x.shape
    n = heads * dh
    per_b = seq // tm
    wspec = lambda col0: pl.BlockSpec((None, k, n), lambda i: (layer, 0, col0 // n), pipeline_mode=pl.Buffered(1))
    fm = pl.BlockSpec((heads, None, dh, tm), lambda i: (0, i // per_b, 0, i % per_b))
    tmaj = pl.BlockSpec((heads, None, tm, dh), lambda i: (0, i // per_b, i % per_b, 0))
    return pl.pallas_call(
        functools.partial(_qkv_kernel, q_scale=q_scale),
        grid=(batch * per_b,),
        in_specs=[pl.BlockSpec((tm, k), lambda i: (i, 0)), wspec(COL_Q), wspec(COL_K), wspec(COL_V)],
        out_specs=[fm, tmaj, fm],
        out_shape=[jax.ShapeDtypeStruct((heads, batch, dh, seq), BF16),
                   jax.ShapeDtypeStruct((heads, batch, seq, dh), BF16),
                   jax.ShapeDtypeStruct((heads, batch, dh, seq), BF16)],
        compiler_params=_params("parallel"),
        name="qkv_projection",
    )(x, w_stack, w_stack, w_stack)


def _conv_kernel(x_ref, wh_ref, wb_ref, wc_ref, cw_ref, z_ref, u_ref, *, nsplit):
    s = x_ref.shape[0]
    rows = s // nsplit
    pad = jnp.zeros((8, u_ref.shape[1]), F32)
    u_ref[0:8, :] = pad
    u_ref[s + 8:s + 16, :] = pad
    for r in range(nsplit):
        x = x_ref[r * rows:(r + 1) * rows, :]
        u_ref[8 + r * rows:8 + (r + 1) * rows, :] = (jnp.dot(x, wc_ref[...], preferred_element_type=F32)
                                                     * jnp.dot(x, wh_ref[...], preferred_element_type=F32))
    cw = cw_ref[...]
    for r in range(nsplit):
        lo = r * rows
        cb = jnp.dot(x_ref[lo:lo + rows, :], wb_ref[...], preferred_element_type=F32)
        y = (cw[0:1, :] * u_ref[lo + 7:lo + rows + 7, :] + cw[1:2, :] * u_ref[lo + 8:lo + rows + 8, :]
             + cw[2:3, :] * u_ref[lo + 9:lo + rows + 9, :])
        z_ref[lo:lo + rows, :] = (cb * y).astype(z_ref.dtype)


def _conv_branch(xb, w_in, conv_w, layer, batch, seq, tc=512, nsplit=2):
    t, d = xb.shape
    nct = D_MODEL // tc
    return pl.pallas_call(
        functools.partial(_conv_kernel, nsplit=nsplit),
        grid=(batch, nct),
        in_specs=[
            pl.BlockSpec((seq, d), lambda b, j: (b, 0)),
            pl.BlockSpec((None, d, tc), lambda b, j: (layer, 0, COL_CH // tc + j)),
            pl.BlockSpec((None, d, tc), lambda b, j: (layer, 0, COL_CB // tc + j)),
            pl.BlockSpec((None, d, tc), lambda b, j: (layer, 0, COL_CC // tc + j)),
            pl.BlockSpec((None, CONV_K, tc), lambda b, j: (layer, 0, j)),
        ],
        out_specs=pl.BlockSpec((seq, tc), lambda b, j: (b, j)),
        out_shape=jax.ShapeDtypeStruct((t, D_MODEL), BF16),
        scratch_shapes=[pltpu.VMEM((seq + 16, tc), F32)],
        compiler_params=_params("parallel", "arbitrary"),
        name="conv_branch",
    )(xb, w_in, w_in, w_in, conv_w)


def _bias_kernel(rel_ref, lo_ref, hi_ref, wbkt_ref, o_ref, win_ref, *, rows):
    h = pl.program_id(0)
    j = pl.program_id(1)
    s, tq = o_ref.shape

    @pl.when(j == 0)
    def _():
        bkt = wbkt_ref[...]
        acc = jnp.zeros(bkt.shape, F32)
        for b in range(REL_BUCKETS):
            acc = jnp.where(bkt == b, rel_ref[b, h], acc)
        win_ref[...] = acc * LOG2E

    def chunk(c, carry):
        r0 = pl.multiple_of(c * rows, rows)
        lo = lo_ref[c, j]

        @pl.when(lo == hi_ref[c, j])
        def _():
            o_ref[pl.ds(r0, rows), :] = jnp.full((rows, tq), rel_ref[lo, h], F32) * LOG2E

        @pl.when(lo != hi_ref[c, j])
        def _():
            w0 = jnp.clip(c * rows - j * tq + REL_MAX_DIST, 0, win_ref.shape[0] - rows)
            o_ref[pl.ds(r0, rows), :] = win_ref[pl.ds(pl.multiple_of(w0, rows), rows), :]

        return carry

    lax.fori_loop(0, s // rows, chunk, 0)


def _rel_bias_kt(rel_bias, seq, tq=256, rows=REL_MAX_DIST):
    wrows = tq + 2 * REL_MAX_DIST
    r = jnp.arange(wrows, dtype=jnp.int32)[:, None]
    c = jnp.arange(tq, dtype=jnp.int32)[None, :]
    win_bkt = _t5_bucket(r - REL_MAX_DIST - c)
    k0 = jnp.arange(seq // rows, dtype=jnp.int32)[:, None, None] * rows
    q0 = jnp.arange(seq // tq, dtype=jnp.int32)[None, :, None] * tq
    span = _t5_bucket(k0 - q0 - (tq - 1) + jnp.arange(rows + tq - 1, dtype=jnp.int32)[None, None, :])
    smem = pl.BlockSpec(memory_space=pltpu.SMEM)
    return pl.pallas_call(
        functools.partial(_bias_kernel, rows=rows),
        grid=(DIFF_HEADS, seq // tq),
        in_specs=[smem, smem, smem, pl.BlockSpec((wrows, tq), lambda h, j: (0, 0))],
        out_specs=pl.BlockSpec((None, seq, tq), lambda h, j: (h, 0, j)),
        out_shape=jax.ShapeDtypeStruct((DIFF_HEADS, seq, seq), F32),
        scratch_shapes=[pltpu.VMEM((wrows, tq), F32)],
        compiler_params=_params("parallel", "arbitrary"),
        name="rel_bias",
    )(rel_bias, span.min(axis=-1), span.max(axis=-1), win_bkt)


def _t5_bucket(rel):
    nb = REL_BUCKETS // 2
    max_exact = nb // 2
    ret = (rel > 0).astype(jnp.int32) * nb
    n = jnp.abs(rel)
    nf = jnp.maximum(n, 1).astype(F32)
    large = max_exact + (jnp.log(nf / max_exact) / math.log(REL_MAX_DIST / max_exact)
                         * (nb - max_exact)).astype(jnp.int32)
    large = jnp.minimum(large, nb - 1)
    return ret + jnp.where(n < max_exact, n, large)


def _diff_attn_kernel(lam_ref, qt_ref, k_ref, bias_ref, vt_ref, vtp_ref, g_ref, o_ref,
                      s0_ref, s1_ref, m0_ref, m1_ref, e_ref, a0_ref, a1_ref, fin_ref, *, lam_init, kc):
    group, _, tq = qt_ref.shape
    s = k_ref.shape[1]
    n = 2 * tq
    step = pl.program_id(0)

    @pl.when(step == 0)
    def _():
        s1_ref[...] = jnp.zeros(s1_ref.shape, F32)
        m1_ref[...] = jnp.zeros(m1_ref.shape, F32)
        a0_ref[...] = jnp.ones(a0_ref.shape, F32)
        fin_ref[...] = jnp.zeros(fin_ref.shape, BF16)

    for j in range(group - 2):
        o_ref[j] = fin_ref[j]

    def sub_tile(j):
        sw_ref, mw_ref, sr_ref, mr_ref = (s0_ref, m0_ref, s1_ref, m1_ref) if j % 2 == 0 else (s1_ref, m1_ref, s0_ref, m0_ref)
        aw_ref, ar_ref = (a1_ref, a0_ref) if j % 2 == 0 else (a0_ref, a1_ref)

        qt = qt_ref[j]
        row = lax.broadcasted_iota(jnp.int32, qt.shape, 0)
        zero = jnp.zeros_like(qt)
        q12 = jnp.concatenate([jnp.where(row < DIFF_DK, qt, zero), jnp.where(row >= DIFF_DK, qt, zero)], axis=1)
        b = bias_ref[...]
        sc = jnp.dot(k_ref[j], q12, preferred_element_type=F32) + jnp.concatenate([b, b], axis=1)
        sw_ref[...] = sc
        mw_ref[...] = jnp.max(sc.reshape(s // 8, 8, n), axis=0)

        m = jnp.max(mr_ref[...], axis=0, keepdims=True)
        for c in range(s // kc):
            sl = slice(c * kc, (c + 1) * kc)
            e_ref[sl, :] = jnp.exp2(sr_ref[sl, :] - m).astype(BF16)
        orow = lax.broadcasted_iota(jnp.int32, (SUM_ROWS, s), 0)
        ones = jnp.where(orow == 0, 1.0, 0.0).astype(BF16)
        vt_prev = vtp_ref[...] if j == 0 else vt_ref[j - 1]
        vt = jnp.concatenate([vt_prev, ones], axis=0)
        aw_ref[...] = jnp.dot(vt, e_ref[...], preferred_element_type=F32)

        acc = ar_ref[...]
        lf = lam_ref[...]
        lam = (jnp.exp(jnp.sum(lf[0:1, :] * lf[1:2, :], axis=-1, keepdims=True))
               - jnp.exp(jnp.sum(lf[2:3, :] * lf[3:4, :], axis=-1, keepdims=True)) + lam_init)
        r = 1.0 / acc[DIFF_DV:DIFF_DV + 1, :]
        ot = acc[:DIFF_DV, :tq] * r[:, :tq] - acc[:DIFF_DV, tq:] * (lam * r[:, tq:])
        o = ot.T
        ms = jnp.mean(o * o, axis=-1, keepdims=True)
        res = (o * lax.rsqrt(ms + LN_EPS) * g_ref[...] * (1.0 - lam_init)).astype(o_ref.dtype)
        if j < 2:
            o_ref[group - 2 + j] = res
        else:
            fin_ref[j - 2] = res

    for j in range(group):
        pl.when(step >= j - group)(functools.partial(sub_tile, j))


def _diff_attention(qt, k, vt, bias_kt, diff_lambda, subln_g, layer, lam_init, tq=512, kc=256, group=4):
    heads, batch, seq, _ = k.shape
    nqt = seq // tq
    nbg = batch // group
    nreal = heads * nqt * nbg

    def dec(gs):
        return gs // (nqt * nbg), (gs // nbg) % nqt, gs % nbg

    cur = lambda gs: dec(jnp.minimum(gs, nreal - 1))
    out = lambda gs: dec(jnp.maximum(gs - 1, 0))

    def qt_map(gs):
        h, i, bg = cur(gs)
        return (h, bg, 0, i)

    def kv_map(gs):
        h, i, bg = cur(gs)
        return (h, bg, 0, 0)

    def bias_map(gs):
        h, i, bg = cur(gs)
        return (h, 0, i)

    def vtp_map(gs):
        h, i, bg = dec(jnp.maximum(gs - 1, 0))
        return (h, bg * group + group - 1, 0, 0)

    def o_map(gs):
        h, i, bg = out(gs)
        return (h, bg, i, 0)

    kern = functools.partial(_diff_attn_kernel, lam_init=lam_init, kc=kc)
    return pl.pallas_call(
        kern,
        grid=(nreal + 1,),
        in_specs=[
            pl.BlockSpec((None, 4, DIFF_DK), lambda gs: (layer, 0, 0)),
            pl.BlockSpec((None, group, DIFF_DV, tq), qt_map),
            pl.BlockSpec((None, group, seq, DIFF_DV), kv_map),
            pl.BlockSpec((None, seq, tq), bias_map),
            pl.BlockSpec((None, group, DIFF_DV, seq), kv_map),
            pl.BlockSpec((None, None, DIFF_DV, seq), vtp_map),
            pl.BlockSpec((None, 1, DIFF_DV), lambda gs: (layer, 0, 0)),
        ],
        out_specs=pl.BlockSpec((None, group, tq, DIFF_DV), o_map),
        out_shape=jax.ShapeDtypeStruct((heads, batch, seq, DIFF_DV), BF16),
        scratch_shapes=[
            pltpu.VMEM((seq, 2 * tq), F32), pltpu.VMEM((seq, 2 * tq), F32),
            pltpu.VMEM((8, 2 * tq), F32), pltpu.VMEM((8, 2 * tq), F32),
            pltpu.VMEM((seq, 2 * tq), BF16),
            pltpu.VMEM((DIFF_DV + SUM_ROWS, 2 * tq), F32), pltpu.VMEM((DIFF_DV + SUM_ROWS, 2 * tq), F32),
            pltpu.VMEM((group - 2, tq, DIFF_DV), BF16),
        ],
        compiler_params=_params("arbitrary"),
        name="diff_attn",
    )(diff_lambda, qt, k, bias_kt, vt, vt, subln_g)


def _merge_kernel(x_ref, xb_ref, z_ref, att_ref, mkv_ref,
                  wmq_ref, wga_ref, wgb_ref, wgc_ref, bg_ref,
                  wca_ref, wdo_ref, wmo_ref, wo_ref, g_ref, b_ref,
                  y_ref, yb_ref, o_scr, *, alpha, nsplit):
    xb = xb_ref[...]
    d = D_MODEL
    w = MEM_HEADS * MEM_DH

    def gate(wg_ref, j):
        gl = jnp.dot(xb, wg_ref[...], preferred_element_type=F32) + bg_ref[:, j * d:(j + 1) * d]
        return 1.0 / (1.0 + jnp.exp(-gl))

    mq = (jnp.dot(xb, wmq_ref[...], preferred_element_type=F32) * (MEM_DH ** -0.5)).astype(BF16)
    scs = [lax.dot_general(mq[:, h * MEM_DH:(h + 1) * MEM_DH], mkv_ref[:, h * MEM_DH:(h + 1) * MEM_DH], NT_DIMS,
                           preferred_element_type=F32) for h in range(MEM_HEADS)]
    ga_a = gate(wga_ref, 0) * jnp.dot(z_ref[...], wca_ref[...], preferred_element_type=F32)
    ps = []
    for sc in scs:
        e = jnp.exp(sc - jnp.max(sc, axis=-1, keepdims=True))
        ps.append((e * (1.0 / jnp.sum(e, axis=-1, keepdims=True))).astype(BF16))
    att = jnp.concatenate([att_ref[h] for h in range(att_ref.shape[0])], axis=1)
    gb_b = gate(wgb_ref, 1) * jnp.dot(att, wdo_ref[...], preferred_element_type=F32)
    for h in range(MEM_HEADS):
        o_scr[:, h * MEM_DH:(h + 1) * MEM_DH] = jnp.dot(ps[h], mkv_ref[:, w + h * MEM_DH:w + (h + 1) * MEM_DH],
                                                         preferred_element_type=F32).astype(BF16)
    gc = gate(wgc_ref, 2)
    out_c = jnp.dot(o_scr[...], wmo_ref[...], preferred_element_type=F32)
    merged = (ga_a + gb_b + gc * out_c).astype(BF16)
    rows = merged.shape[0] // nsplit
    for r in range(nsplit):
        rs = slice(r * rows, (r + 1) * rows)
        hmix = jnp.dot(merged[rs], wo_ref[...], preferred_element_type=F32)
        y = _layer_norm(alpha * x_ref[rs, :] + hmix, g_ref[...], b_ref[...])
        y_ref[rs, :] = y
        yb_ref[rs, :] = y.astype(BF16)


def _merge(x, xb, z, att, mkv, w_in, b_gate, w_conv_out, w_diff_out, w_mem_out, w_o, ln_g, ln_b,
           layer, seq, mem_len, alpha, tm=512, nsplit=2):
    t, d = x.shape
    per_b = seq // tm
    row = lambda i: (i, 0)
    wspec = lambda: pl.BlockSpec((None, d, d), lambda i: (layer, 0, 0), pipeline_mode=pl.Buffered(1))
    win = lambda col: pl.BlockSpec((None, d, d), lambda i: (layer, 0, col // d), pipeline_mode=pl.Buffered(1))
    vec = lambda n: pl.BlockSpec((None, 1, n), lambda i: (layer, 0, 0))
    kern = functools.partial(_merge_kernel, alpha=alpha, nsplit=nsplit)
    return pl.pallas_call(
        kern,
        grid=(t // tm,),
        in_specs=[
            pl.BlockSpec((tm, d), row),
            pl.BlockSpec((tm, d), row),
            pl.BlockSpec((tm, d), row),
            pl.BlockSpec((DIFF_HEADS, tm, DIFF_DV), lambda i: (0, i, 0)),
            pl.BlockSpec((None, mem_len, 2 * d), lambda i: (layer, i // per_b, 0)),
            win(COL_MQ), win(COL_GATE), win(COL_GATE + d), win(COL_GATE + 2 * d),
            vec(N_BRANCH * d),
            wspec(), wspec(), wspec(), wspec(),
            vec(d), vec(d),
        ],
        out_specs=[pl.BlockSpec((tm, d), row), pl.BlockSpec((tm, d), row)],
        out_shape=[jax.ShapeDtypeStruct((t, d), F32), jax.ShapeDtypeStruct((t, d), BF16)],
        scratch_shapes=[pltpu.VMEM((tm, d), BF16)],
        compiler_params=_params("parallel"),
        name="merge",
    )(x, xb, z, att, mkv, w_in, w_in, w_in, w_in, b_gate,
      w_conv_out, w_diff_out, w_mem_out, w_o, ln_g, ln_b)


def _mlp_kernel(x_ref, xb_ref, w1_ref, w2_ref, g_ref, b_ref, y_ref, yb_ref, *, alpha, fc, nsplit):
    rows = x_ref.shape[0] // nsplit
    for r in range(nsplit):
        rs = slice(r * rows, (r + 1) * rows)
        xb = xb_ref[rs, :]
        f = jnp.zeros((rows, x_ref.shape[1]), F32)
        for c in range(w1_ref.shape[1] // fc):
            cs = slice(c * fc, (c + 1) * fc)
            hid = jnp.maximum(jnp.dot(xb, w1_ref[:, cs], preferred_element_type=F32), 0.0)
            f = f + jnp.dot((hid * hid).astype(BF16), w2_ref[cs, :], preferred_element_type=F32)
        y = _layer_norm(alpha * x_ref[rs, :] + f, g_ref[...], b_ref[...])
        y_ref[rs, :] = y
        yb_ref[rs, :] = y.astype(BF16)


def _mlp(x, xb, w1, w2, ln_g, ln_b, layer, alpha, tm=1024, fc=2048, nsplit=2):
    t, d = x.shape
    dff = w1.shape[2]
    row = lambda i: (i, 0)
    vec = lambda n: pl.BlockSpec((None, 1, n), lambda i: (layer, 0, 0))
    kern = functools.partial(_mlp_kernel, alpha=alpha, fc=fc, nsplit=nsplit)
    return pl.pallas_call(
        kern,
        grid=(t // tm,),
        in_specs=[
            pl.BlockSpec((tm, d), row),
            pl.BlockSpec((tm, d), row),
            pl.BlockSpec((None, d, dff), lambda i: (layer, 0, 0), pipeline_mode=pl.Buffered(1)),
            pl.BlockSpec((None, dff, d), lambda i: (layer, 0, 0), pipeline_mode=pl.Buffered(1)),
            vec(d), vec(d),
        ],
        out_specs=[pl.BlockSpec((tm, d), row), pl.BlockSpec((tm, d), row)],
        out_shape=[jax.ShapeDtypeStruct((t, d), F32), jax.ShapeDtypeStruct((t, d), BF16)],
        compiler_params=_params("parallel"),
        name="mlp",
    )(x, xb, w1, w2, ln_g, ln_b)


def kernel(x, mem, w_in, b_gate, conv_w, w_conv_out, diff_lambda, subln_g, w_diff_out, rel_bias,
           w_mem_kv, w_mem_out, w_o, ln1_g, ln1_b, w_mlp1, w_mlp2, ln2_g, ln2_b):
    batch, seq, d = x.shape
    mem_len = mem.shape[1]
    depth = w_in.shape[0]
    alpha = (2 * depth) ** 0.25

    w_in_b = w_in.astype(BF16)
    w_conv_out_b = w_conv_out.astype(BF16)
    w_diff_out_b = w_diff_out.astype(BF16)
    w_mem_kv_b = w_mem_kv.astype(BF16)
    w_mem_out_b = w_mem_out.astype(BF16)
    w_o_b = w_o.astype(BF16)
    w_mlp1_b = w_mlp1.astype(BF16)
    w_mlp2_b = w_mlp2.astype(BF16)
    mem_b = mem.reshape(batch * mem_len, d).astype(BF16)
    vec3 = lambda a: a.reshape(depth, 1, a.shape[-1])

    bias_kt = _rel_bias_kt(rel_bias, seq)

    mkv = _matmul_layers(mem_b, w_mem_kv_b, 1024, 1024, BF16)

    xf = x.reshape(batch * seq, d)
    xb = xf.astype(BF16)
    for l in range(depth):
        lam_init = 0.8 - 0.6 * math.exp(-0.3 * l)
        qt, k, vt = _qkv_projection(xb, w_in_b, l, DIFF_HEADS, DIFF_DV, batch, seq, DIFF_DK ** -0.5 * LOG2E)
        att = _diff_attention(qt, k, vt, bias_kt, diff_lambda,
                              vec3(subln_g), l, lam_init).reshape(DIFF_HEADS, batch * seq, DIFF_DV)
        z = _conv_branch(xb, w_in_b, conv_w, l, batch, seq)
        xf, xb = _merge(xf, xb, z, att, mkv, w_in_b, vec3(b_gate), w_conv_out_b, w_diff_out_b,
                        w_mem_out_b, w_o_b, vec3(ln1_g), vec3(ln1_b), l, seq, mem_len, alpha)
        xf, xb = _mlp(xf, xb, w_mlp1_b, w_mlp2_b, vec3(ln2_g), vec3(ln2_b), l, alpha)
    return xf.reshape(batch, seq, d)
```

```python
import functools
import math

import jax
import jax.numpy as jnp
from jax import lax
from jax.experimental import pallas as pl
from jax.experimental.pallas import tpu as pltpu

F32 = jnp.float32
BF16 = jnp.bfloat16

D_MODEL = 1024
CONV_K = 3
DIFF_HEADS = 8
DIFF_DK = 64
DIFF_DV = 2 * DIFF_DK
MEM_HEADS = 4
MEM_DH = 256
N_BRANCH = 3
REL_BUCKETS = 32
REL_MAX_DIST = 128
LN_EPS = 1e-5
LOG2E = math.log2(math.e)
SUM_ROWS = 16

COL_CH, COL_CB, COL_CC = 0, D_MODEL, 2 * D_MODEL
COL_Q, COL_K, COL_V = 3 * D_MODEL, 4 * D_MODEL, 5 * D_MODEL
COL_MQ = 6 * D_MODEL
COL_GATE = 7 * D_MODEL

VMEM_LIMIT = 56 * 1024 * 1024

NT_DIMS = (((1,), (1,)), ((), ()))
TN_DIMS = (((0,), (1,)), ((), ()))


def _params(*sem):
    return pltpu.CompilerParams(dimension_semantics=sem, vmem_limit_bytes=VMEM_LIMIT)


def _layer_norm(y, g, b):
    mu = jnp.mean(y, axis=-1, keepdims=True)
    yc = y - mu
    var = jnp.mean(yc * yc, axis=-1, keepdims=True)
    return yc * lax.rsqrt(var + LN_EPS) * g + b


def _mm_kernel(x_ref, w_ref, o_ref):
    o_ref[...] = jnp.dot(x_ref[...], w_ref[...], preferred_element_type=F32).astype(o_ref.dtype)


def _matmul_layers(x, w_stack, tm, tn, out_dtype):
    m, k = x.shape
    layers, _, n = w_stack.shape
    return pl.pallas_call(
        _mm_kernel,
        grid=(m // tm, layers, n // tn),
        in_specs=[
            pl.BlockSpec((tm, k), lambda i, l, j: (i, 0)),
            pl.BlockSpec((None, k, tn), lambda i, l, j: (l, 0, j)),
        ],
        out_specs=pl.BlockSpec((None, tm, tn), lambda i, l, j: (l, i, j)),
        out_shape=jax.ShapeDtypeStruct((layers, m, n), out_dtype),
        compiler_params=_params("parallel", "arbitrary", "arbitrary"),
        name="matmul_layers",
    )(x, w_stack)


def _qkv_kernel(x_ref, wq_ref, wk_ref, wv_ref, qt_ref, k_ref, vt_ref, *, q_scale):
    heads, dh, _ = qt_ref.shape
    x = x_ref[...]
    qt = (lax.dot_general(wq_ref[...], x, TN_DIMS, preferred_element_type=F32) * q_scale).astype(qt_ref.dtype)
    kk = jnp.dot(x, wk_ref[...], preferred_element_type=F32).astype(k_ref.dtype)
    vt = lax.dot_general(wv_ref[...], x, TN_DIMS, preferred_element_type=F32).astype(vt_ref.dtype)
    for h in range(heads):
        qt_ref[h] = qt[h * dh:(h + 1) * dh, :]
        k_ref[h] = kk[:, h * dh:(h + 1) * dh]
        vt_ref[h] = vt[h * dh:(h + 1) * dh, :]


def _qkv_projection(x, w_stack, layer, heads, dh, batch, seq, q_scale, tm=2048):
    _, k = x.shape
    n = heads * dh
    per_b = seq // tm
    wspec = lambda col0: pl.BlockSpec((None, k, n), lambda i: (layer, 0, col0 // n), pipeline_mode=pl.Buffered(1))
    fm = pl.BlockSpec((heads, None, dh, tm), lambda i: (0, i // per_b, 0, i % per_b))
    tmaj = pl.BlockSpec((heads, None, tm, dh), lambda i: (0, i // per_b, i % per_b, 0))
    return pl.pallas_call(
        functools.partial(_qkv_kernel, q_scale=q_scale),
        grid=(batch * per_b,),
        in_specs=[pl.BlockSpec((tm, k), lambda i: (i, 0)), wspec(COL_Q), wspec(COL_K), wspec(COL_V)],
        out_specs=[fm, tmaj, fm],
        out_shape=[jax.ShapeDtypeStruct((heads, batch, dh, seq), BF16),
                   jax.ShapeDtypeStruct((heads, batch, seq, dh), BF16),
                   jax.ShapeDtypeStruct((heads, batch, dh, seq), BF16)],
        compiler_params=_params("parallel"),
        name="qkv_projection",
    )(x, w_stack, w_stack, w_stack)


def _conv_kernel(x_ref, wh_ref, wb_ref, wc_ref, cw_ref, z_ref, u_ref, *, nsplit):
    s = x_ref.shape[0]
    rows = s // nsplit
    pad = jnp.zeros((8, u_ref.shape[1]), F32)
    u_ref[0:8, :] = pad
    u_ref[s + 8:s + 16, :] = pad
    for r in range(nsplit):
        x = x_ref[r * rows:(r + 1) * rows, :]
        u_ref[8 + r * rows:8 + (r + 1) * rows, :] = (jnp.dot(x, wc_ref[...], preferred_element_type=F32)
                                                     * jnp.dot(x, wh_ref[...], preferred_element_type=F32))
    cw = cw_ref[...]
    for r in range(nsplit):
        lo = r * rows
        cb = jnp.dot(x_ref[lo:lo + rows, :], wb_ref[...], preferred_element_type=F32)
        y = (cw[0:1, :] * u_ref[lo + 7:lo + rows + 7, :] + cw[1:2, :] * u_ref[lo + 8:lo + rows + 8, :]
             + cw[2:3, :] * u_ref[lo + 9:lo + rows + 9, :])
        z_ref[lo:lo + rows, :] = (cb * y).astype(z_ref.dtype)


def _conv_branch(xb, w_in, conv_w, layer, batch, seq, tc=512, nsplit=2):
    t, d = xb.shape
    nct = D_MODEL // tc
    return pl.pallas_call(
        functools.partial(_conv_kernel, nsplit=nsplit),
        grid=(batch, nct),
        in_specs=[
            pl.BlockSpec((seq, d), lambda b, j: (b, 0)),
            pl.BlockSpec((None, d, tc), lambda b, j: (layer, 0, COL_CH // tc + j)),
            pl.BlockSpec((None, d, tc), lambda b, j: (layer, 0, COL_CB // tc + j)),
            pl.BlockSpec((None, d, tc), lambda b, j: (layer, 0, COL_CC // tc + j)),
            pl.BlockSpec((None, CONV_K, tc), lambda b, j: (layer, 0, j)),
        ],
        out_specs=pl.BlockSpec((seq, tc), lambda b, j: (b, j)),
        out_shape=jax.ShapeDtypeStruct((t, D_MODEL), BF16),
        scratch_shapes=[pltpu.VMEM((seq + 16, tc), F32)],
        compiler_params=_params("parallel", "arbitrary"),
        name="conv_branch",
    )(xb, w_in, w_in, w_in, conv_w)


def _bias_kernel(rel_ref, lo_ref, hi_ref, wbkt_ref, o_ref, win_ref, *, rows):
    h = pl.program_id(0)
    j = pl.program_id(1)
    s, tq = o_ref.shape

    @pl.when(j == 0)
    def _():
        bkt = wbkt_ref[...]
        acc = jnp.zeros(bkt.shape, F32)
        for b in range(REL_BUCKETS):
            acc = jnp.where(bkt == b, rel_ref[b, h], acc)
        win_ref[...] = acc * LOG2E

    def chunk(c, carry):
        r0 = pl.multiple_of(c * rows, rows)
        lo = lo_ref[c, j]

        @pl.when(lo == hi_ref[c, j])
        def _():
            o_ref[pl.ds(r0, rows), :] = jnp.full((rows, tq), rel_ref[lo, h], F32) * LOG2E

        @pl.when(lo != hi_ref[c, j])
        def _():
            w0 = jnp.clip(c * rows - j * tq + REL_MAX_DIST, 0, win_ref.shape[0] - rows)
            o_ref[pl.ds(r0, rows), :] = win_ref[pl.ds(pl.multiple_of(w0, rows), rows), :]

        return carry

    lax.fori_loop(0, s // rows, chunk, 0)


def _rel_bias_kt(rel_bias, seq, tq=256, rows=REL_MAX_DIST):
    wrows = tq + 2 * REL_MAX_DIST
    r = jnp.arange(wrows, dtype=jnp.int32)[:, None]
    c = jnp.arange(tq, dtype=jnp.int32)[None, :]
    win_bkt = _t5_bucket(r - REL_MAX_DIST - c)
    k0 = jnp.arange(seq // rows, dtype=jnp.int32)[:, None, None] * rows
    q0 = jnp.arange(seq // tq, dtype=jnp.int32)[None, :, None] * tq
    span = _t5_bucket(k0 - q0 - (tq - 1) + jnp.arange(rows + tq - 1, dtype=jnp.int32)[None, None, :])
    smem = pl.BlockSpec(memory_space=pltpu.SMEM)
    return pl.pallas_call(
        functools.partial(_bias_kernel, rows=rows),
        grid=(DIFF_HEADS, seq // tq),
        in_specs=[smem, smem, smem, pl.BlockSpec((wrows, tq), lambda h, j: (0, 0))],
        out_specs=pl.BlockSpec((None, seq, tq), lambda h, j: (h, 0, j)),
        out_shape=jax.ShapeDtypeStruct((DIFF_HEADS, seq, seq), F32),
        scratch_shapes=[pltpu.VMEM((wrows, tq), F32)],
        compiler_params=_params("parallel", "arbitrary"),
        name="rel_bias",
    )(rel_bias, span.min(axis=-1), span.max(axis=-1), win_bkt)


def _t5_bucket(rel):
    nb = REL_BUCKETS // 2
    max_exact = nb // 2
    ret = (rel > 0).astype(jnp.int32) * nb
    n = jnp.abs(rel)
    nf = jnp.maximum(n, 1).astype(F32)
    large = max_exact + (jnp.log(nf / max_exact) / math.log(REL_MAX_DIST / max_exact)
                         * (nb - max_exact)).astype(jnp.int32)
    large = jnp.minimum(large, nb - 1)
    return ret + jnp.where(n < max_exact, n, large)


def _diff_attn_kernel(lam_ref, qt_ref, k_ref, bias_ref, vt_ref, vtp_ref, g_ref, o_ref,
                      s0_ref, s1_ref, m0_ref, m1_ref, a0_ref, a1_ref, fin_ref, *, lam_init, kc):
    group, _, tq = qt_ref.shape
    s = k_ref.shape[1]
    n = 2 * tq
    step = pl.program_id(0)

    @pl.when(step == 0)
    def _():
        s1_ref[...] = jnp.zeros(s1_ref.shape, F32)
        m1_ref[...] = jnp.zeros(m1_ref.shape, F32)
        a0_ref[...] = jnp.ones(a0_ref.shape, F32)
        fin_ref[...] = jnp.zeros(fin_ref.shape, BF16)

    for j in range(group - 2):
        o_ref[j] = fin_ref[j]

    def sub_tile(j):
        sw_ref, mw_ref, sr_ref, mr_ref = (s0_ref, m0_ref, s1_ref, m1_ref) if j % 2 == 0 else (s1_ref, m1_ref, s0_ref, m0_ref)
        aw_ref, ar_ref = (a1_ref, a0_ref) if j % 2 == 0 else (a0_ref, a1_ref)

        qt = qt_ref[j]
        row = lax.broadcasted_iota(jnp.int32, qt.shape, 0)
        zero = jnp.zeros_like(qt)
        q12 = jnp.concatenate([jnp.where(row < DIFF_DK, qt, zero), jnp.where(row >= DIFF_DK, qt, zero)], axis=1)
        m = jnp.max(mr_ref[...], axis=0, keepdims=True)
        orow = lax.broadcasted_iota(jnp.int32, (SUM_ROWS, kc), 0)
        ones = jnp.where(orow == 0, 1.0, 0.0).astype(BF16)
        vt_prev = vtp_ref if j == 0 else vt_ref.at[j - 1]
        mnew = jnp.full((8, n), -jnp.inf, F32)
        pv = jnp.zeros((DIFF_DV + SUM_ROWS, n), F32)
        for c in range(s // kc):
            sl = slice(c * kc, (c + 1) * kc)
            e = jnp.exp2(sr_ref[sl, :] - m).astype(BF16)
            vt = jnp.concatenate([vt_prev[:, sl], ones], axis=0)
            pv = pv + jnp.dot(vt, e, preferred_element_type=F32)
            b = bias_ref[sl, :]
            sc = jnp.dot(k_ref[j, sl, :], q12, preferred_element_type=F32) + jnp.concatenate([b, b], axis=1)
            sw_ref[sl, :] = sc
            mnew = jnp.maximum(mnew, jnp.max(sc.reshape(kc // 8, 8, n), axis=0))
        mw_ref[...] = mnew
        aw_ref[...] = pv

        acc = ar_ref[...]
        lf = lam_ref[...]
        lam = (jnp.exp(jnp.sum(lf[0:1, :] * lf[1:2, :], axis=-1, keepdims=True))
               - jnp.exp(jnp.sum(lf[2:3, :] * lf[3:4, :], axis=-1, keepdims=True)) + lam_init)
        r = 1.0 / acc[DIFF_DV:DIFF_DV + 1, :]
        ot = acc[:DIFF_DV, :tq] * r[:, :tq] - acc[:DIFF_DV, tq:] * (lam * r[:, tq:])
        o = ot.T
        ms = jnp.mean(o * o, axis=-1, keepdims=True)
        res = (o * lax.rsqrt(ms + LN_EPS) * g_ref[...] * (1.0 - lam_init)).astype(o_ref.dtype)
        if j < 2:
            o_ref[group - 2 + j] = res
        else:
            fin_ref[j - 2] = res

    for j in range(group):
        pl.when(step >= j - group)(functools.partial(sub_tile, j))


def _diff_attention(qt, k, vt, bias_kt, diff_lambda, subln_g, layer, lam_init, tq=512, kc=256, group=4):
    heads, batch, seq, _ = k.shape
    nqt = seq // tq
    nbg = batch // group
    nreal = heads * nqt * nbg

    def dec(gs):
        return gs // (nqt * nbg), (gs // nbg) % nqt, gs % nbg

    cur = lambda gs: dec(jnp.minimum(gs, nreal - 1))
    out = lambda gs: dec(jnp.maximum(gs - 1, 0))

    def qt_map(gs):
        h, i, bg = cur(gs)
        return (h, bg, 0, i)

    def kv_map(gs):
        h, i, bg = cur(gs)
        return (h, bg, 0, 0)

    def bias_map(gs):
        h, i, bg = cur(gs)
        return (h, 0, i)

    def vtp_map(gs):
        h, i, bg = dec(jnp.maximum(gs - 1, 0))
        return (h, bg * group + group - 1, 0, 0)

    def o_map(gs):
        h, i, bg = out(gs)
        return (h, bg, i, 0)

    kern = functools.partial(_diff_attn_kernel, lam_init=lam_init, kc=kc)
    return pl.pallas_call(
        kern,
        grid=(nreal + 1,),
        in_specs=[
            pl.BlockSpec((None, 4, DIFF_DK), lambda gs: (layer, 0, 0)),
            pl.BlockSpec((None, group, DIFF_DV, tq), qt_map),
            pl.BlockSpec((None, group, seq, DIFF_DV), kv_map),
            pl.BlockSpec((None, seq, tq), bias_map),
            pl.BlockSpec((None, group, DIFF_DV, seq), kv_map),
            pl.BlockSpec((None, None, DIFF_DV, seq), vtp_map),
            pl.BlockSpec((None, 1, DIFF_DV), lambda gs: (layer, 0, 0)),
        ],
        out_specs=pl.BlockSpec((None, group, tq, DIFF_DV), o_map),
        out_shape=jax.ShapeDtypeStruct((heads, batch, seq, DIFF_DV), BF16),
        scratch_shapes=[
            pltpu.VMEM((seq, 2 * tq), F32), pltpu.VMEM((seq, 2 * tq), F32),
            pltpu.VMEM((8, 2 * tq), F32), pltpu.VMEM((8, 2 * tq), F32),
            pltpu.VMEM((DIFF_DV + SUM_ROWS, 2 * tq), F32), pltpu.VMEM((DIFF_DV + SUM_ROWS, 2 * tq), F32),
            pltpu.VMEM((group - 2, tq, DIFF_DV), BF16),
        ],
        compiler_params=_params("arbitrary"),
        name="diff_attn",
    )(diff_lambda, qt, k, bias_kt, vt, vt, subln_g)


def _merge_kernel(x_ref, xb_ref, z_ref, att_ref, mkv_ref,
                  wmq_ref, wga_ref, wgb_ref, wgc_ref, bg_ref,
                  wca_ref, wdo_ref, wmo_ref, wo_ref, g_ref, b_ref,
                  y_ref, yb_ref, o_scr, *, alpha, nsplit):
    xb = xb_ref[...]
    d = D_MODEL
    w = MEM_HEADS * MEM_DH

    def gate(wg_ref, j):
        gl = jnp.dot(xb, wg_ref[...], preferred_element_type=F32) + bg_ref[:, j * d:(j + 1) * d]
        return 1.0 / (1.0 + jnp.exp(-gl))

    mq = (jnp.dot(xb, wmq_ref[...], preferred_element_type=F32) * (MEM_DH ** -0.5)).astype(BF16)
    scs = [lax.dot_general(mq[:, h * MEM_DH:(h + 1) * MEM_DH], mkv_ref[:, h * MEM_DH:(h + 1) * MEM_DH], NT_DIMS,
                           preferred_element_type=F32) for h in range(MEM_HEADS)]
    ga_a = gate(wga_ref, 0) * jnp.dot(z_ref[...], wca_ref[...], preferred_element_type=F32)
    ps = []
    for sc in scs:
        e = jnp.exp(sc - jnp.max(sc, axis=-1, keepdims=True))
        ps.append((e * (1.0 / jnp.sum(e, axis=-1, keepdims=True))).astype(BF16))
    att = jnp.concatenate([att_ref[h] for h in range(att_ref.shape[0])], axis=1)
    gb_b = gate(wgb_ref, 1) * jnp.dot(att, wdo_ref[...], preferred_element_type=F32)
    for h in range(MEM_HEADS):
        o_scr[:, h * MEM_DH:(h + 1) * MEM_DH] = jnp.dot(ps[h], mkv_ref[:, w + h * MEM_DH:w + (h + 1) * MEM_DH],
                                                         preferred_element_type=F32).astype(BF16)
    gc = gate(wgc_ref, 2)
    out_c = jnp.dot(o_scr[...], wmo_ref[...], preferred_element_type=F32)
    merged = (ga_a + gb_b + gc * out_c).astype(BF16)
    rows = merged.shape[0] // nsplit
    for r in range(nsplit):
        rs = slice(r * rows, (r + 1) * rows)
        hmix = jnp.dot(merged[rs], wo_ref[...], preferred_element_type=F32)
        y = _layer_norm(alpha * x_ref[rs, :] + hmix, g_ref[...], b_ref[...])
        y_ref[rs, :] = y
        yb_ref[rs, :] = y.astype(BF16)


def _merge(x, xb, z, att, mkv, w_in, b_gate, w_conv_out, w_diff_out, w_mem_out, w_o, ln_g, ln_b,
           layer, seq, mem_len, alpha, tm=512, nsplit=2):
    t, d = x.shape
    per_b = seq // tm
    row = lambda i: (i, 0)
    wspec = lambda: pl.BlockSpec((None, d, d), lambda i: (layer, 0, 0), pipeline_mode=pl.Buffered(1))
    win = lambda col: pl.BlockSpec((None, d, d), lambda i: (layer, 0, col // d), pipeline_mode=pl.Buffered(1))
    vec = lambda n: pl.BlockSpec((None, 1, n), lambda i: (layer, 0, 0))
    kern = functools.partial(_merge_kernel, alpha=alpha, nsplit=nsplit)
    return pl.pallas_call(
        kern,
        grid=(t // tm,),
        in_specs=[
            pl.BlockSpec((tm, d), row),
            pl.BlockSpec((tm, d), row),
            pl.BlockSpec((tm, d), row),
            pl.BlockSpec((DIFF_HEADS, tm, DIFF_DV), lambda i: (0, i, 0)),
            pl.BlockSpec((None, mem_len, 2 * d), lambda i: (layer, i // per_b, 0)),
            win(COL_MQ), win(COL_GATE), win(COL_GATE + d), win(COL_GATE + 2 * d),
            vec(N_BRANCH * d),
            wspec(), wspec(), wspec(), wspec(),
            vec(d), vec(d),
        ],
        out_specs=[pl.BlockSpec((tm, d), row), pl.BlockSpec((tm, d), row)],
        out_shape=[jax.ShapeDtypeStruct((t, d), F32), jax.ShapeDtypeStruct((t, d), BF16)],
        scratch_shapes=[pltpu.VMEM((tm, d), BF16)],
        compiler_params=_params("parallel"),
        name="merge",
    )(x, xb, z, att, mkv, w_in, w_in, w_in, w_in, b_gate,
      w_conv_out, w_diff_out, w_mem_out, w_o, ln_g, ln_b)


def _mlp_kernel(x_ref, xb_ref, w1_ref, w2_ref, g_ref, b_ref, y_ref, yb_ref, *, alpha, fc, nsplit):
    rows = x_ref.shape[0] // nsplit
    for r in range(nsplit):
        rs = slice(r * rows, (r + 1) * rows)
        xb = xb_ref[rs, :]
        f = jnp.zeros((rows, x_ref.shape[1]), F32)
        for c in range(w1_ref.shape[1] // fc):
            cs = slice(c * fc, (c + 1) * fc)
            hid = jnp.maximum(jnp.dot(xb, w1_ref[:, cs], preferred_element_type=F32), 0.0)
            f = f + jnp.dot((hid * hid).astype(BF16), w2_ref[cs, :], preferred_element_type=F32)
        y = _layer_norm(alpha * x_ref[rs, :] + f, g_ref[...], b_ref[...])
        y_ref[rs, :] = y
        yb_ref[rs, :] = y.astype(BF16)


def _mlp(x, xb, w1, w2, ln_g, ln_b, layer, alpha, tm=1024, fc=2048, nsplit=2):
    t, d = x.shape
    dff = w1.shape[2]
    row = lambda i: (i, 0)
    vec = lambda n: pl.BlockSpec((None, 1, n), lambda i: (layer, 0, 0))
    kern = functools.partial(_mlp_kernel, alpha=alpha, fc=fc, nsplit=nsplit)
    return pl.pallas_call(
        kern,
        grid=(t // tm,),
        in_specs=[
            pl.BlockSpec((tm, d), row),
            pl.BlockSpec((tm, d), row),
            pl.BlockSpec((None, d, dff), lambda i: (layer, 0, 0), pipeline_mode=pl.Buffered(1)),
            pl.BlockSpec((None, dff, d), lambda i: (layer, 0, 0), pipeline_mode=pl.Buffered(1)),
            vec(d), vec(d),
        ],
        out_specs=[pl.BlockSpec((tm, d), row), pl.BlockSpec((tm, d), row)],
        out_shape=[jax.ShapeDtypeStruct((t, d), F32), jax.ShapeDtypeStruct((t, d), BF16)],
        compiler_params=_params("parallel"),
        name="mlp",
    )(x, xb, w1, w2, ln_g, ln_b)


def kernel(x, mem, w_in, b_gate, conv_w, w_conv_out, diff_lambda, subln_g, w_diff_out, rel_bias,
           w_mem_kv, w_mem_out, w_o, ln1_g, ln1_b, w_mlp1, w_mlp2, ln2_g, ln2_b):
    batch, seq, d = x.shape
    mem_len = mem.shape[1]
    depth = w_in.shape[0]
    alpha = (2 * depth) ** 0.25

    w_in_b = w_in.astype(BF16)
    w_conv_out_b = w_conv_out.astype(BF16)
    w_diff_out_b = w_diff_out.astype(BF16)
    w_mem_kv_b = w_mem_kv.astype(BF16)
    w_mem_out_b = w_mem_out.astype(BF16)
    w_o_b = w_o.astype(BF16)
    w_mlp1_b = w_mlp1.astype(BF16)
    w_mlp2_b = w_mlp2.astype(BF16)
    mem_b = mem.reshape(batch * mem_len, d).astype(BF16)
    vec3 = lambda a: a.reshape(depth, 1, a.shape[-1])

    bias_kt = _rel_bias_kt(rel_bias, seq)

    mkv = _matmul_layers(mem_b, w_mem_kv_b, 1024, 1024, BF16)

    xf = x.reshape(batch * seq, d)
    xb = xf.astype(BF16)
    for l in range(depth):
        lam_init = 0.8 - 0.6 * math.exp(-0.3 * l)
        qt, k, vt = _qkv_projection(xb, w_in_b, l, DIFF_HEADS, DIFF_DV, batch, seq, DIFF_DK ** -0.5 * LOG2E)
        att = _diff_attention(qt, k, vt, bias_kt, diff_lambda,
                              vec3(subln_g), l, lam_init).reshape(DIFF_HEADS, batch * seq, DIFF_DV)
        z = _conv_branch(xb, w_in_b, conv_w, l, batch, seq)
        xf, xb = _merge(xf, xb, z, att, mkv, w_in_b, vec3(b_gate), w_conv_out_b, w_diff_out_b,
                        w_mem_out_b, w_o_b, vec3(ln1_g), vec3(ln1_b), l, seq, mem_len, alpha)
        xf, xb = _mlp(xf, xb, w_mlp1_b, w_mlp2_b, vec3(ln2_g), vec3(ln2_b), l, alpha)
    return xf.reshape(batch, seq, d)
```

```python
import functools
import math

import jax
import jax.numpy as jnp
from jax import lax
from jax.experimental import pallas as pl
from jax.experimental.pallas import tpu as pltpu

F32 = jnp.float32
BF16 = jnp.bfloat16

D_MODEL = 1024
CONV_K = 3
DIFF_HEADS = 8
DIFF_DK = 64
DIFF_DV = 2 * DIFF_DK
MEM_HEADS = 4
MEM_DH = 256
N_BRANCH = 3
REL_BUCKETS = 32
REL_MAX_DIST = 128
LN_EPS = 1e-5
LOG2E = math.log2(math.e)
SUM_ROWS = 16

COL_CH, COL_CB, COL_CC = 0, D_MODEL, 2 * D_MODEL
COL_Q, COL_K, COL_V = 3 * D_MODEL, 4 * D_MODEL, 5 * D_MODEL
COL_MQ = 6 * D_MODEL
COL_GATE = 7 * D_MODEL

VMEM_LIMIT = 56 * 1024 * 1024

NT_DIMS = (((1,), (1,)), ((), ()))
TN_DIMS = (((0,), (1,)), ((), ()))


def _params(*sem):
    return pltpu.CompilerParams(dimension_semantics=sem, vmem_limit_bytes=VMEM_LIMIT)


def _layer_norm(y, g, b):
    mu = jnp.mean(y, axis=-1, keepdims=True)
    yc = y - mu
    var = jnp.mean(yc * yc, axis=-1, keepdims=True)
    return yc * lax.rsqrt(var + LN_EPS) * g + b


def _mm_kernel(x_ref, w_ref, o_ref):
    o_ref[...] = jnp.dot(x_ref[...], w_ref[...], preferred_element_type=F32).astype(o_ref.dtype)


def _matmul_layers(x, w_stack, tm, tn, out_dtype):
    m, k = x.shape
    layers, _, n = w_stack.shape
    return pl.pallas_call(
        _mm_kernel,
        grid=(m // tm, layers, n // tn),
        in_specs=[
            pl.BlockSpec((tm, k), lambda i, l, j: (i, 0)),
            pl.BlockSpec((None, k, tn), lambda i, l, j: (l, 0, j)),
        ],
        out_specs=pl.BlockSpec((None, tm, tn), lambda i, l, j: (l, i, j)),
        out_shape=jax.ShapeDtypeStruct((layers, m, n), out_dtype),
        compiler_params=_params("parallel", "arbitrary", "arbitrary"),
        name="matmul_layers",
    )(x, w_stack)


def _qkv_kernel(x_ref, wq_ref, wk_ref, wv_ref, qt_ref, k_ref, vt_ref, *, q_scale):
    heads, dh, _ = qt_ref.shape
    x = x_ref[...]
    qt = (lax.dot_general(wq_ref[...], x, TN_DIMS, preferred_element_type=F32) * q_scale).astype(qt_ref.dtype)
    kk = jnp.dot(x, wk_ref[...], preferred_element_type=F32).astype(k_ref.dtype)
    vt = lax.dot_general(wv_ref[...], x, TN_DIMS, preferred_element_type=F32).astype(vt_ref.dtype)
    for h in range(heads):
        qt_ref[h] = qt[h * dh:(h + 1) * dh, :]
        k_ref[h] = kk[:, h * dh:(h + 1) * dh]
        vt_ref[h] = vt[h * dh:(h + 1) * dh, :]


def _qkv_projection(x, w_stack, layer, heads, dh, batch, seq, q_scale, tm=2048):
    _, k = x.shape
    n = heads * dh
    per_b = seq // tm
    wspec = lambda col0: pl.BlockSpec((None, k, n), lambda i: (layer, 0, col0 // n), pipeline_mode=pl.Buffered(1))
    fm = pl.BlockSpec((heads, None, dh, tm), lambda i: (0, i // per_b, 0, i % per_b))
    tmaj = pl.BlockSpec((heads, None, tm, dh), lambda i: (0, i // per_b, i % per_b, 0))
    return pl.pallas_call(
        functools.partial(_qkv_kernel, q_scale=q_scale),
        grid=(batch * per_b,),
        in_specs=[pl.BlockSpec((tm, k), lambda i: (i, 0)), wspec(COL_Q), wspec(COL_K), wspec(COL_V)],
        out_specs=[fm, tmaj, fm],
        out_shape=[jax.ShapeDtypeStruct((heads, batch, dh, seq), BF16),
                   jax.ShapeDtypeStruct((heads, batch, seq, dh), BF16),
                   jax.ShapeDtypeStruct((heads, batch, dh, seq), BF16)],
        compiler_params=_params("parallel"),
        name="qkv_projection",
    )(x, w_stack, w_stack, w_stack)


def _conv_kernel(x_ref, wh_ref, wb_ref, wc_ref, cw_ref, z_ref, u_ref, *, nsplit):
    s = x_ref.shape[0]
    rows = s // nsplit
    pad = jnp.zeros((8, u_ref.shape[1]), F32)
    u_ref[0:8, :] = pad
    u_ref[s + 8:s + 16, :] = pad
    for r in range(nsplit):
        x = x_ref[r * rows:(r + 1) * rows, :]
        u_ref[8 + r * rows:8 + (r + 1) * rows, :] = (jnp.dot(x, wc_ref[...], preferred_element_type=F32)
                                                     * jnp.dot(x, wh_ref[...], preferred_element_type=F32))
    cw = cw_ref[...]
    for r in range(nsplit):
        lo = r * rows
        cb = jnp.dot(x_ref[lo:lo + rows, :], wb_ref[...], preferred_element_type=F32)
        y = (cw[0:1, :] * u_ref[lo + 7:lo + rows + 7, :] + cw[1:2, :] * u_ref[lo + 8:lo + rows + 8, :]
             + cw[2:3, :] * u_ref[lo + 9:lo + rows + 9, :])
        z_ref[lo:lo + rows, :] = (cb * y).astype(z_ref.dtype)


def _conv_branch(xb, w_in, conv_w, layer, batch, seq, tc=512, nsplit=2):
    t, d = xb.shape
    nct = D_MODEL // tc
    return pl.pallas_call(
        functools.partial(_conv_kernel, nsplit=nsplit),
        grid=(batch, nct),
        in_specs=[
            pl.BlockSpec((seq, d), lambda b, j: (b, 0)),
            pl.BlockSpec((None, d, tc), lambda b, j: (layer, 0, COL_CH // tc + j)),
            pl.BlockSpec((None, d, tc), lambda b, j: (layer, 0, COL_CB // tc + j)),
            pl.BlockSpec((None, d, tc), lambda b, j: (layer, 0, COL_CC // tc + j)),
            pl.BlockSpec((None, CONV_K, tc), lambda b, j: (layer, 0, j)),
        ],
        out_specs=pl.BlockSpec((seq, tc), lambda b, j: (b, j)),
        out_shape=jax.ShapeDtypeStruct((t, D_MODEL), BF16),
        scratch_shapes=[pltpu.VMEM((seq + 16, tc), F32)],
        compiler_params=_params("parallel", "arbitrary"),
        name="conv_branch",
    )(xb, w_in, w_in, w_in, conv_w)


def _bias_kernel(rel_ref, lo_ref, hi_ref, wbkt_ref, o_ref, win_ref, *, rows):
    h = pl.program_id(0)
    j = pl.program_id(1)
    s, tq = o_ref.shape

    @pl.when(j == 0)
    def _():
        bkt = wbkt_ref[...]
        acc = jnp.zeros(bkt.shape, F32)
        for b in range(REL_BUCKETS):
            acc = jnp.where(bkt == b, rel_ref[b, h], acc)
        win_ref[...] = acc * LOG2E

    def chunk(c, carry):
        r0 = pl.multiple_of(c * rows, rows)
        lo = lo_ref[c, j]

        @pl.when(lo == hi_ref[c, j])
        def _():
            o_ref[pl.ds(r0, rows), :] = jnp.full((rows, tq), rel_ref[lo, h], F32) * LOG2E

        @pl.when(lo != hi_ref[c, j])
        def _():
            w0 = jnp.clip(c * rows - j * tq + REL_MAX_DIST, 0, win_ref.shape[0] - rows)
            o_ref[pl.ds(r0, rows), :] = win_ref[pl.ds(pl.multiple_of(w0, rows), rows), :]

        return carry

    lax.fori_loop(0, s // rows, chunk, 0)


def _rel_bias_kt(rel_bias, seq, tq=256, rows=REL_MAX_DIST):
    wrows = tq + 2 * REL_MAX_DIST
    r = jnp.arange(wrows, dtype=jnp.int32)[:, None]
    c = jnp.arange(tq, dtype=jnp.int32)[None, :]
    win_bkt = _t5_bucket(r - REL_MAX_DIST - c)
    k0 = jnp.arange(seq // rows, dtype=jnp.int32)[:, None, None] * rows
    q0 = jnp.arange(seq // tq, dtype=jnp.int32)[None, :, None] * tq
    span = _t5_bucket(k0 - q0 - (tq - 1) + jnp.arange(rows + tq - 1, dtype=jnp.int32)[None, None, :])
    smem = pl.BlockSpec(memory_space=pltpu.SMEM)
    return pl.pallas_call(
        functools.partial(_bias_kernel, rows=rows),
        grid=(DIFF_HEADS, seq // tq),
        in_specs=[smem, smem, smem, pl.BlockSpec((wrows, tq), lambda h, j: (0, 0))],
        out_specs=pl.BlockSpec((None, seq, tq), lambda h, j: (h, 0, j)),
        out_shape=jax.ShapeDtypeStruct((DIFF_HEADS, seq, seq), F32),
        scratch_shapes=[pltpu.VMEM((wrows, tq), F32)],
        compiler_params=_params("parallel", "arbitrary"),
        name="rel_bias",
    )(rel_bias, span.min(axis=-1), span.max(axis=-1), win_bkt)


def _t5_bucket(rel):
    nb = REL_BUCKETS // 2
    max_exact = nb // 2
    ret = (rel > 0).astype(jnp.int32) * nb
    n = jnp.abs(rel)
    nf = jnp.maximum(n, 1).astype(F32)
    large = max_exact + (jnp.log(nf / max_exact) / math.log(REL_MAX_DIST / max_exact)
                         * (nb - max_exact)).astype(jnp.int32)
    large = jnp.minimum(large, nb - 1)
    return ret + jnp.where(n < max_exact, n, large)


def _diff_attn_kernel(lam_ref, qt_ref, k_ref, bias_ref, vt_ref, vtp_ref, g_ref, o_ref,
                      s0_ref, s1_ref, m0_ref, m1_ref, a0_ref, a1_ref, fin_ref, *, lam_init, kc):
    group, _, tq = qt_ref.shape
    s = k_ref.shape[1]
    n = 2 * tq
    step = pl.program_id(0)

    @pl.when(step == 0)
    def _():
        s1_ref[...] = jnp.zeros(s1_ref.shape, F32)
        m1_ref[...] = jnp.zeros(m1_ref.shape, F32)
        a0_ref[...] = jnp.ones(a0_ref.shape, F32)
        fin_ref[...] = jnp.zeros(fin_ref.shape, BF16)

    for j in range(group - 2):
        o_ref[j] = fin_ref[j]

    def sub_tile(j):
        sw_ref, mw_ref, sr_ref, mr_ref = (s0_ref, m0_ref, s1_ref, m1_ref) if j % 2 == 0 else (s1_ref, m1_ref, s0_ref, m0_ref)
        aw_ref, ar_ref = (a1_ref, a0_ref) if j % 2 == 0 else (a0_ref, a1_ref)

        qt = qt_ref[j]
        row = lax.broadcasted_iota(jnp.int32, qt.shape, 0)
        zero = jnp.zeros_like(qt)
        q12 = jnp.concatenate([jnp.where(row < DIFF_DK, qt, zero), jnp.where(row >= DIFF_DK, qt, zero)], axis=1)
        m = jnp.max(mr_ref[...], axis=0, keepdims=True)
        orow = lax.broadcasted_iota(jnp.int32, (SUM_ROWS, kc), 0)
        ones = jnp.where(orow == 0, 1.0, 0.0).astype(BF16)
        vt_prev = vtp_ref if j == 0 else vt_ref.at[j - 1]
        mnew = jnp.full((8, n), -jnp.inf, F32)
        pv = jnp.zeros((DIFF_DV + SUM_ROWS, n), F32)
        for c in range(s // kc):
            sl = slice(c * kc, (c + 1) * kc)
            e = jnp.exp2(sr_ref[sl, :] - m).astype(BF16)
            vt = jnp.concatenate([vt_prev[:, sl], ones], axis=0)
            pv = pv + jnp.dot(vt, e, preferred_element_type=F32)
            b = bias_ref[sl, :]
            sc = jnp.dot(k_ref[j, sl, :], q12, preferred_element_type=F32) + jnp.concatenate([b, b], axis=1)
            sw_ref[sl, :] = sc
            mnew = jnp.maximum(mnew, jnp.max(sc.reshape(kc // 8, 8, n), axis=0))
        mw_ref[...] = mnew
        aw_ref[...] = pv

        acc = ar_ref[...]
        lf = lam_ref[...]
        lam = (jnp.exp(jnp.sum(lf[0:1, :] * lf[1:2, :], axis=-1, keepdims=True))
               - jnp.exp(jnp.sum(lf[2:3, :] * lf[3:4, :], axis=-1, keepdims=True)) + lam_init)
        r = 1.0 / acc[DIFF_DV:DIFF_DV + 1, :]
        ot = acc[:DIFF_DV, :tq] * r[:, :tq] - acc[:DIFF_DV, tq:] * (lam * r[:, tq:])
        o = ot.T
        ms = jnp.mean(o * o, axis=-1, keepdims=True)
        res = (o * lax.rsqrt(ms + LN_EPS) * g_ref[...] * (1.0 - lam_init)).astype(o_ref.dtype)
        if j < 2:
            o_ref[group - 2 + j] = res
        else:
            fin_ref[j - 2] = res

    def pair(j0):
        sub_tile(j0)
        sub_tile(j0 + 1)

    for j0 in range(0, group, 2):
        pl.when(step >= j0 - group)(functools.partial(pair, j0))


def _diff_attention(qt, k, vt, bias_kt, diff_lambda, subln_g, layer, lam_init, tq=512, kc=256, group=4):
    heads, batch, seq, _ = k.shape
    nqt = seq // tq
    nbg = batch // group
    nreal = heads * nqt * nbg

    def dec(gs):
        return gs // (nqt * nbg), (gs // nbg) % nqt, gs % nbg

    cur = lambda gs: dec(jnp.minimum(gs, nreal - 1))
    out = lambda gs: dec(jnp.maximum(gs - 1, 0))

    def qt_map(gs):
        h, i, bg = cur(gs)
        return (h, bg, 0, i)

    def kv_map(gs):
        h, i, bg = cur(gs)
        return (h, bg, 0, 0)

    def bias_map(gs):
        h, i, bg = cur(gs)
        return (h, 0, i)

    def vtp_map(gs):
        h, i, bg = dec(jnp.maximum(gs - 1, 0))
        return (h, bg * group + group - 1, 0, 0)

    def o_map(gs):
        h, i, bg = out(gs)
        return (h, bg, i, 0)

    kern = functools.partial(_diff_attn_kernel, lam_init=lam_init, kc=kc)
    return pl.pallas_call(
        kern,
        grid=(nreal + 1,),
        in_specs=[
            pl.BlockSpec((None, 4, DIFF_DK), lambda gs: (layer, 0, 0)),
            pl.BlockSpec((None, group, DIFF_DV, tq), qt_map),
            pl.BlockSpec((None, group, seq, DIFF_DV), kv_map),
            pl.BlockSpec((None, seq, tq), bias_map),
            pl.BlockSpec((None, group, DIFF_DV, seq), kv_map),
            pl.BlockSpec((None, None, DIFF_DV, seq), vtp_map),
            pl.BlockSpec((None, 1, DIFF_DV), lambda gs: (layer, 0, 0)),
        ],
        out_specs=pl.BlockSpec((None, group, tq, DIFF_DV), o_map),
        out_shape=jax.ShapeDtypeStruct((heads, batch, seq, DIFF_DV), BF16),
        scratch_shapes=[
            pltpu.VMEM((seq, 2 * tq), F32), pltpu.VMEM((seq, 2 * tq), F32),
            pltpu.VMEM((8, 2 * tq), F32), pltpu.VMEM((8, 2 * tq), F32),
            pltpu.VMEM((DIFF_DV + SUM_ROWS, 2 * tq), F32), pltpu.VMEM((DIFF_DV + SUM_ROWS, 2 * tq), F32),
            pltpu.VMEM((group - 2, tq, DIFF_DV), BF16),
        ],
        compiler_params=_params("arbitrary"),
        name="diff_attn",
    )(diff_lambda, qt, k, bias_kt, vt, vt, subln_g)


def _merge_kernel(x_ref, xb_ref, z_ref, att_ref, mkv_ref,
                  wmq_ref, wga_ref, wgb_ref, wgc_ref, bg_ref,
                  wca_ref, wdo_ref, wmo_ref, wo_ref, g_ref, b_ref,
                  y_ref, yb_ref, o_scr, *, alpha, nsplit):
    xb = xb_ref[...]
    d = D_MODEL
    w = MEM_HEADS * MEM_DH

    def gate(wg_ref, j):
        gl = jnp.dot(xb, wg_ref[...], preferred_element_type=F32) + bg_ref[:, j * d:(j + 1) * d]
        return 1.0 / (1.0 + jnp.exp(-gl))

    mq = (jnp.dot(xb, wmq_ref[...], preferred_element_type=F32) * (MEM_DH ** -0.5)).astype(BF16)
    scs = [lax.dot_general(mq[:, h * MEM_DH:(h + 1) * MEM_DH], mkv_ref[:, h * MEM_DH:(h + 1) * MEM_DH], NT_DIMS,
                           preferred_element_type=F32) for h in range(MEM_HEADS)]
    ga_a = gate(wga_ref, 0) * jnp.dot(z_ref[...], wca_ref[...], preferred_element_type=F32)
    ps = []
    for sc in scs:
        e = jnp.exp(sc - jnp.max(sc, axis=-1, keepdims=True))
        ps.append((e * (1.0 / jnp.sum(e, axis=-1, keepdims=True))).astype(BF16))
    att = jnp.concatenate([att_ref[h] for h in range(att_ref.shape[0])], axis=1)
    gb_b = gate(wgb_ref, 1) * jnp.dot(att, wdo_ref[...], preferred_element_type=F32)
    for h in range(MEM_HEADS):
        o_scr[:, h * MEM_DH:(h + 1) * MEM_DH] = jnp.dot(ps[h], mkv_ref[:, w + h * MEM_DH:w + (h + 1) * MEM_DH],
                                                         preferred_element_type=F32).astype(BF16)
    gc = gate(wgc_ref, 2)
    out_c = jnp.dot(o_scr[...], wmo_ref[...], preferred_element_type=F32)
    merged = (ga_a + gb_b + gc * out_c).astype(BF16)
    rows = merged.shape[0] // nsplit
    for r in range(nsplit):
        rs = slice(r * rows, (r + 1) * rows)
        hmix = jnp.dot(merged[rs], wo_ref[...], preferred_element_type=F32)
        y = _layer_norm(alpha * x_ref[rs, :] + hmix, g_ref[...], b_ref[...])
        y_ref[rs, :] = y
        yb_ref[rs, :] = y.astype(BF16)


def _merge(x, xb, z, att, mkv, w_in, b_gate, w_conv_out, w_diff_out, w_mem_out, w_o, ln_g, ln_b,
           layer, seq, mem_len, alpha, tm=512, nsplit=2):
    t, d = x.shape
    per_b = seq // tm
    row = lambda i: (i, 0)
    wspec = lambda: pl.BlockSpec((None, d, d), lambda i: (layer, 0, 0), pipeline_mode=pl.Buffered(1))
    win = lambda col: pl.BlockSpec((None, d, d), lambda i: (layer, 0, col // d), pipeline_mode=pl.Buffered(1))
    vec = lambda n: pl.BlockSpec((None, 1, n), lambda i: (layer, 0, 0))
    kern = functools.partial(_merge_kernel, alpha=alpha, nsplit=nsplit)
    return pl.pallas_call(
        kern,
        grid=(t // tm,),
        in_specs=[
            pl.BlockSpec((tm, d), row),
            pl.BlockSpec((tm, d), row),
            pl.BlockSpec((tm, d), row),
            pl.BlockSpec((DIFF_HEADS, tm, DIFF_DV), lambda i: (0, i, 0)),
            pl.BlockSpec((None, mem_len, 2 * d), lambda i: (layer, i // per_b, 0)),
            win(COL_MQ), win(COL_GATE), win(COL_GATE + d), win(COL_GATE + 2 * d),
            vec(N_BRANCH * d),
            wspec(), wspec(), wspec(), wspec(),
            vec(d), vec(d),
        ],
        out_specs=[pl.BlockSpec((tm, d), row), pl.BlockSpec((tm, d), row)],
        out_shape=[jax.ShapeDtypeStruct((t, d), F32), jax.ShapeDtypeStruct((t, d), BF16)],
        scratch_shapes=[pltpu.VMEM((tm, d), BF16)],
        compiler_params=_params("parallel"),
        name="merge",
    )(x, xb, z, att, mkv, w_in, w_in, w_in, w_in, b_gate,
      w_conv_out, w_diff_out, w_mem_out, w_o, ln_g, ln_b)


def _mlp_kernel(x_ref, xb_ref, w1_ref, w2_ref, g_ref, b_ref, y_ref, yb_ref, *, alpha, fc, nsplit):
    rows = x_ref.shape[0] // nsplit
    for r in range(nsplit):
        rs = slice(r * rows, (r + 1) * rows)
        xb = xb_ref[rs, :]
        f = jnp.zeros((rows, x_ref.shape[1]), F32)
        for c in range(w1_ref.shape[1] // fc):
            cs = slice(c * fc, (c + 1) * fc)
            hid = jnp.maximum(jnp.dot(xb, w1_ref[:, cs], preferred_element_type=F32), 0.0)
            f = f + jnp.dot((hid * hid).astype(BF16), w2_ref[cs, :], preferred_element_type=F32)
        y = _layer_norm(alpha * x_ref[rs, :] + f, g_ref[...], b_ref[...])
        y_ref[rs, :] = y
        yb_ref[rs, :] = y.astype(BF16)


def _mlp(x, xb, w1, w2, ln_g, ln_b, layer, alpha, tm=1024, fc=2048, nsplit=2):
    t, d = x.shape
    dff = w1.shape[2]
    row = lambda i: (i, 0)
    vec = lambda n: pl.BlockSpec((None, 1, n), lambda i: (layer, 0, 0))
    kern = functools.partial(_mlp_kernel, alpha=alpha, fc=fc, nsplit=nsplit)
    return pl.pallas_call(
        kern,
        grid=(t // tm,),
        in_specs=[
            pl.BlockSpec((tm, d), row),
            pl.BlockSpec((tm, d), row),
            pl.BlockSpec((None, d, dff), lambda i: (layer, 0, 0), pipeline_mode=pl.Buffered(1)),
            pl.BlockSpec((None, dff, d), lambda i: (layer, 0, 0), pipeline_mode=pl.Buffered(1)),
            vec(d), vec(d),
        ],
        out_specs=[pl.BlockSpec((tm, d), row), pl.BlockSpec((tm, d), row)],
        out_shape=[jax.ShapeDtypeStruct((t, d), F32), jax.ShapeDtypeStruct((t, d), BF16)],
        compiler_params=_params("parallel"),
        name="mlp",
    )(x, xb, w1, w2, ln_g, ln_b)


def kernel(x, mem, w_in, b_gate, conv_w, w_conv_out, diff_lambda, subln_g, w_diff_out, rel_bias,
           w_mem_kv, w_mem_out, w_o, ln1_g, ln1_b, w_mlp1, w_mlp2, ln2_g, ln2_b):
    batch, seq, d = x.shape
    mem_len = mem.shape[1]
    depth = w_in.shape[0]
    alpha = (2 * depth) ** 0.25

    w_in_b = w_in.astype(BF16)
    w_conv_out_b = w_conv_out.astype(BF16)
    w_diff_out_b = w_diff_out.astype(BF16)
    w_mem_kv_b = w_mem_kv.astype(BF16)
    w_mem_out_b = w_mem_out.astype(BF16)
    w_o_b = w_o.astype(BF16)
    w_mlp1_b = w_mlp1.astype(BF16)
    w_mlp2_b = w_mlp2.astype(BF16)
    mem_b = mem.reshape(batch * mem_len, d).astype(BF16)
    vec3 = lambda a: a.reshape(depth, 1, a.shape[-1])

    bias_kt = _rel_bias_kt(rel_bias, seq)

    mkv = _matmul_layers(mem_b, w_mem_kv_b, 1024, 1024, BF16)

    xf = x.reshape(batch * seq, d)
    xb = xf.astype(BF16)
    for l in range(depth):
        lam_init = 0.8 - 0.6 * math.exp(-0.3 * l)
        qt, k, vt = _qkv_projection(xb, w_in_b, l, DIFF_HEADS, DIFF_DV, batch, seq, DIFF_DK ** -0.5 * LOG2E)
        att = _diff_attention(qt, k, vt, bias_kt, diff_lambda,
                              vec3(subln_g), l, lam_init).reshape(DIFF_HEADS, batch * seq, DIFF_DV)
        z = _conv_branch(xb, w_in_b, conv_w, l, batch, seq)
        xf, xb = _merge(xf, xb, z, att, mkv, w_in_b, vec3(b_gate), w_conv_out_b, w_diff_out_b,
                        w_mem_out_b, w_o_b, vec3(ln1_g), vec3(ln1_b), l, seq, mem_len, alpha)
        xf, xb = _mlp(xf, xb, w_mlp1_b, w_mlp2_b, vec3(ln2_g), vec3(ln2_b), l, alpha)
    return xf.reshape(batch, seq, d)
```

```python
import functools
import math

import jax
import jax.numpy as jnp
from jax import lax
from jax.experimental import pallas as pl
from jax.experimental.pallas import tpu as pltpu

F32 = jnp.float32
BF16 = jnp.bfloat16

D_MODEL = 1024
CONV_K = 3
DIFF_HEADS = 8
DIFF_DK = 64
DIFF_DV = 2 * DIFF_DK
MEM_HEADS = 4
MEM_DH = 256
N_BRANCH = 3
REL_BUCKETS = 32
REL_MAX_DIST = 128
LN_EPS = 1e-5
LOG2E = math.log2(math.e)
SUM_ROWS = 16

COL_CH, COL_CB, COL_CC = 0, D_MODEL, 2 * D_MODEL
COL_Q, COL_K, COL_V = 3 * D_MODEL, 4 * D_MODEL, 5 * D_MODEL
COL_MQ = 6 * D_MODEL
COL_GATE = 7 * D_MODEL

VMEM_LIMIT = 56 * 1024 * 1024

NT_DIMS = (((1,), (1,)), ((), ()))
TN_DIMS = (((0,), (1,)), ((), ()))


def _params(*sem):
    return pltpu.CompilerParams(dimension_semantics=sem, vmem_limit_bytes=VMEM_LIMIT)


def _layer_norm(y, g, b):
    mu = jnp.mean(y, axis=-1, keepdims=True)
    yc = y - mu
    var = jnp.mean(yc * yc, axis=-1, keepdims=True)
    return yc * lax.rsqrt(var + LN_EPS) * g + b


def _mm_kernel(x_ref, w_ref, o_ref):
    o_ref[...] = jnp.dot(x_ref[...], w_ref[...], preferred_element_type=F32).astype(o_ref.dtype)


def _matmul_layers(x, w_stack, tm, tn, out_dtype):
    m, k = x.shape
    layers, _, n = w_stack.shape
    return pl.pallas_call(
        _mm_kernel,
        grid=(m // tm, layers, n // tn),
        in_specs=[
            pl.BlockSpec((tm, k), lambda i, l, j: (i, 0)),
            pl.BlockSpec((None, k, tn), lambda i, l, j: (l, 0, j)),
        ],
        out_specs=pl.BlockSpec((None, tm, tn), lambda i, l, j: (l, i, j)),
        out_shape=jax.ShapeDtypeStruct((layers, m, n), out_dtype),
        compiler_params=_params("parallel", "arbitrary", "arbitrary"),
        name="matmul_layers",
    )(x, w_stack)


def _qkv_kernel(x_ref, wq_ref, wk_ref, wv_ref, qt_ref, k_ref, vt_ref, *, q_scale):
    heads, dh, _ = qt_ref.shape
    x = x_ref[...]
    qt = (lax.dot_general(wq_ref[...], x, TN_DIMS, preferred_element_type=F32) * q_scale).astype(qt_ref.dtype)
    kk = jnp.dot(x, wk_ref[...], preferred_element_type=F32).astype(k_ref.dtype)
    vt = lax.dot_general(wv_ref[...], x, TN_DIMS, preferred_element_type=F32).astype(vt_ref.dtype)
    for h in range(heads):
        qt_ref[h] = qt[h * dh:(h + 1) * dh, :]
        k_ref[h] = kk[:, h * dh:(h + 1) * dh]
        vt_ref[h] = vt[h * dh:(h + 1) * dh, :]


def _qkv_projection(x, w_stack, layer, heads, dh, batch, seq, q_scale, tm=2048):
    _, k = x.shape
    n = heads * dh
    per_b = seq // tm
    wspec = lambda col0: pl.BlockSpec((None, k, n), lambda i: (layer, 0, col0 // n), pipeline_mode=pl.Buffered(1))
    fm = pl.BlockSpec((heads, None, dh, tm), lambda i: (0, i // per_b, 0, i % per_b))
    tmaj = pl.BlockSpec((heads, None, tm, dh), lambda i: (0, i // per_b, i % per_b, 0))
    return pl.pallas_call(
        functools.partial(_qkv_kernel, q_scale=q_scale),
        grid=(batch * per_b,),
        in_specs=[pl.BlockSpec((tm, k), lambda i: (i, 0)), wspec(COL_Q), wspec(COL_K), wspec(COL_V)],
        out_specs=[fm, tmaj, fm],
        out_shape=[jax.ShapeDtypeStruct((heads, batch, dh, seq), BF16),
                   jax.ShapeDtypeStruct((heads, batch, seq, dh), BF16),
                   jax.ShapeDtypeStruct((heads, batch, dh, seq), BF16)],
        compiler_params=_params("parallel"),
        name="qkv_projection",
    )(x, w_stack, w_stack, w_stack)


def _conv_kernel(x_ref, wh_ref, wb_ref, wc_ref, cw_ref, z_ref, u_ref, *, nsplit):
    s = x_ref.shape[0]
    rows = s // nsplit
    pad = jnp.zeros((8, u_ref.shape[1]), F32)
    u_ref[0:8, :] = pad
    u_ref[s + 8:s + 16, :] = pad
    for r in range(nsplit):
        x = x_ref[r * rows:(r + 1) * rows, :]
        u_ref[8 + r * rows:8 + (r + 1) * rows, :] = (jnp.dot(x, wc_ref[...], preferred_element_type=F32)
                                                     * jnp.dot(x, wh_ref[...], preferred_element_type=F32))
    cw = cw_ref[...]
    for r in range(nsplit):
        lo = r * rows
        cb = jnp.dot(x_ref[lo:lo + rows, :], wb_ref[...], preferred_element_type=F32)
        y = (cw[0:1, :] * u_ref[lo + 7:lo + rows + 7, :] + cw[1:2, :] * u_ref[lo + 8:lo + rows + 8, :]
             + cw[2:3, :] * u_ref[lo + 9:lo + rows + 9, :])
        z_ref[lo:lo + rows, :] = (cb * y).astype(z_ref.dtype)


def _conv_branch(xb, w_in, conv_w, layer, batch, seq, tc=1024, nsplit=2):
    t, d = xb.shape
    nct = D_MODEL // tc
    return pl.pallas_call(
        functools.partial(_conv_kernel, nsplit=nsplit),
        grid=(batch, nct),
        in_specs=[
            pl.BlockSpec((seq, d), lambda b, j: (b, 0)),
            pl.BlockSpec((None, d, tc), lambda b, j: (layer, 0, COL_CH // tc + j)),
            pl.BlockSpec((None, d, tc), lambda b, j: (layer, 0, COL_CB // tc + j)),
            pl.BlockSpec((None, d, tc), lambda b, j: (layer, 0, COL_CC // tc + j)),
            pl.BlockSpec((None, CONV_K, tc), lambda b, j: (layer, 0, j)),
        ],
        out_specs=pl.BlockSpec((seq, tc), lambda b, j: (b, j)),
        out_shape=jax.ShapeDtypeStruct((t, D_MODEL), BF16),
        scratch_shapes=[pltpu.VMEM((seq + 16, tc), F32)],
        compiler_params=_params("parallel", "arbitrary"),
        name="conv_branch",
    )(xb, w_in, w_in, w_in, conv_w)


def _bias_kernel(rel_ref, lo_ref, hi_ref, wbkt_ref, o_ref, win_ref, *, rows):
    h = pl.program_id(0)
    j = pl.program_id(1)
    s, tq = o_ref.shape

    @pl.when(j == 0)
    def _():
        bkt = wbkt_ref[...]
        acc = jnp.zeros(bkt.shape, F32)
        for b in range(REL_BUCKETS):
            acc = jnp.where(bkt == b, rel_ref[b, h], acc)
        win_ref[...] = acc * LOG2E

    def chunk(c, carry):
        r0 = pl.multiple_of(c * rows, rows)
        lo = lo_ref[c, j]

        @pl.when(lo == hi_ref[c, j])
        def _():
            o_ref[pl.ds(r0, rows), :] = jnp.full((rows, tq), rel_ref[lo, h], F32) * LOG2E

        @pl.when(lo != hi_ref[c, j])
        def _():
            w0 = jnp.clip(c * rows - j * tq + REL_MAX_DIST, 0, win_ref.shape[0] - rows)
            o_ref[pl.ds(r0, rows), :] = win_ref[pl.ds(pl.multiple_of(w0, rows), rows), :]

        return carry

    lax.fori_loop(0, s // rows, chunk, 0)


def _rel_bias_kt(rel_bias, seq, tq=256, rows=REL_MAX_DIST):
    wrows = tq + 2 * REL_MAX_DIST
    r = jnp.arange(wrows, dtype=jnp.int32)[:, None]
    c = jnp.arange(tq, dtype=jnp.int32)[None, :]
    win_bkt = _t5_bucket(r - REL_MAX_DIST - c)
    k0 = jnp.arange(seq // rows, dtype=jnp.int32)[:, None, None] * rows
    q0 = jnp.arange(seq // tq, dtype=jnp.int32)[None, :, None] * tq
    span = _t5_bucket(k0 - q0 - (tq - 1) + jnp.arange(rows + tq - 1, dtype=jnp.int32)[None, None, :])
    smem = pl.BlockSpec(memory_space=pltpu.SMEM)
    return pl.pallas_call(
        functools.partial(_bias_kernel, rows=rows),
        grid=(DIFF_HEADS, seq // tq),
        in_specs=[smem, smem, smem, pl.BlockSpec((wrows, tq), lambda h, j: (0, 0))],
        out_specs=pl.BlockSpec((None, seq, tq), lambda h, j: (h, 0, j)),
        out_shape=jax.ShapeDtypeStruct((DIFF_HEADS, seq, seq), F32),
        scratch_shapes=[pltpu.VMEM((wrows, tq), F32)],
        compiler_params=_params("parallel", "arbitrary"),
        name="rel_bias",
    )(rel_bias, span.min(axis=-1), span.max(axis=-1), win_bkt)


def _t5_bucket(rel):
    nb = REL_BUCKETS // 2
    max_exact = nb // 2
    ret = (rel > 0).astype(jnp.int32) * nb
    n = jnp.abs(rel)
    nf = jnp.maximum(n, 1).astype(F32)
    large = max_exact + (jnp.log(nf / max_exact) / math.log(REL_MAX_DIST / max_exact)
                         * (nb - max_exact)).astype(jnp.int32)
    large = jnp.minimum(large, nb - 1)
    return ret + jnp.where(n < max_exact, n, large)


def _diff_attn_kernel(lam_ref, qt_ref, k_ref, bias_ref, vt_ref, vtp_ref, g_ref, o_ref,
                      s0_ref, s1_ref, m0_ref, m1_ref, a0_ref, a1_ref, fin_ref, *, lam_init, kc):
    group, _, tq = qt_ref.shape
    s = k_ref.shape[1]
    n = 2 * tq
    step = pl.program_id(0)

    @pl.when(step == 0)
    def _():
        s1_ref[...] = jnp.zeros(s1_ref.shape, F32)
        m1_ref[...] = jnp.zeros(m1_ref.shape, F32)
        a0_ref[...] = jnp.ones(a0_ref.shape, F32)
        fin_ref[...] = jnp.zeros(fin_ref.shape, BF16)

    for j in range(group - 2):
        o_ref[j] = fin_ref[j]

    def sub_tile(j):
        sw_ref, mw_ref, sr_ref, mr_ref = (s0_ref, m0_ref, s1_ref, m1_ref) if j % 2 == 0 else (s1_ref, m1_ref, s0_ref, m0_ref)
        aw_ref, ar_ref = (a1_ref, a0_ref) if j % 2 == 0 else (a0_ref, a1_ref)

        qt = qt_ref[j]
        row = lax.broadcasted_iota(jnp.int32, qt.shape, 0)
        zero = jnp.zeros_like(qt)
        q12 = jnp.concatenate([jnp.where(row < DIFF_DK, qt, zero), jnp.where(row >= DIFF_DK, qt, zero)], axis=1)
        m = jnp.max(mr_ref[...], axis=0, keepdims=True)
        orow = lax.broadcasted_iota(jnp.int32, (SUM_ROWS, kc), 0)
        ones = jnp.where(orow == 0, 1.0, 0.0).astype(BF16)
        vt_prev = vtp_ref if j == 0 else vt_ref.at[j - 1]
        mnew = jnp.full((8, n), -jnp.inf, F32)
        pv = jnp.zeros((DIFF_DV + SUM_ROWS, n), F32)
        for c in range(s // kc):
            sl = slice(c * kc, (c + 1) * kc)
            e = jnp.exp2(sr_ref[sl, :] - m).astype(BF16)
            vt = jnp.concatenate([vt_prev[:, sl], ones], axis=0)
            pv = pv + jnp.dot(vt, e, preferred_element_type=F32)
            b = bias_ref[sl, :]
            sc = jnp.dot(k_ref[j, sl, :], q12, preferred_element_type=F32) + jnp.concatenate([b, b], axis=1)
            sw_ref[sl, :] = sc
            mnew = jnp.maximum(mnew, jnp.max(sc.reshape(kc // 8, 8, n), axis=0))
        mw_ref[...] = mnew
        aw_ref[...] = pv

        acc = ar_ref[...]
        lf = lam_ref[...]
        lam = (jnp.exp(jnp.sum(lf[0:1, :] * lf[1:2, :], axis=-1, keepdims=True))
               - jnp.exp(jnp.sum(lf[2:3, :] * lf[3:4, :], axis=-1, keepdims=True)) + lam_init)
        r = 1.0 / acc[DIFF_DV:DIFF_DV + 1, :]
        ot = acc[:DIFF_DV, :tq] * r[:, :tq] - acc[:DIFF_DV, tq:] * (lam * r[:, tq:])
        o = ot.T
        ms = jnp.mean(o * o, axis=-1, keepdims=True)
        res = (o * lax.rsqrt(ms + LN_EPS) * g_ref[...] * (1.0 - lam_init)).astype(o_ref.dtype)
        if j < 2:
            o_ref[group - 2 + j] = res
        else:
            fin_ref[j - 2] = res

    def pair(j0):
        sub_tile(j0)
        sub_tile(j0 + 1)

    for j0 in range(0, group, 2):
        pl.when(step >= j0 - group)(functools.partial(pair, j0))


def _diff_attention(qt, k, vt, bias_kt, diff_lambda, subln_g, layer, lam_init, tq=512, kc=256, group=4):
    heads, batch, seq, _ = k.shape
    nqt = seq // tq
    nbg = batch // group
    nreal = heads * nqt * nbg

    def dec(gs):
        return gs // (nqt * nbg), (gs // nbg) % nqt, gs % nbg

    cur = lambda gs: dec(jnp.minimum(gs, nreal - 1))
    out = lambda gs: dec(jnp.maximum(gs - 1, 0))

    def qt_map(gs):
        h, i, bg = cur(gs)
        return (h, bg, 0, i)

    def kv_map(gs):
        h, i, bg = cur(gs)
        return (h, bg, 0, 0)

    def bias_map(gs):
        h, i, bg = cur(gs)
        return (h, 0, i)

    def vtp_map(gs):
        h, i, bg = dec(jnp.maximum(gs - 1, 0))
        return (h, bg * group + group - 1, 0, 0)

    def o_map(gs):
        h, i, bg = out(gs)
        return (h, bg, i, 0)

    kern = functools.partial(_diff_attn_kernel, lam_init=lam_init, kc=kc)
    return pl.pallas_call(
        kern,
        grid=(nreal + 1,),
        in_specs=[
            pl.BlockSpec((None, 4, DIFF_DK), lambda gs: (layer, 0, 0)),
            pl.BlockSpec((None, group, DIFF_DV, tq), qt_map),
            pl.BlockSpec((None, group, seq, DIFF_DV), kv_map),
            pl.BlockSpec((None, seq, tq), bias_map),
            pl.BlockSpec((None, group, DIFF_DV, seq), kv_map),
            pl.BlockSpec((None, None, DIFF_DV, seq), vtp_map),
            pl.BlockSpec((None, 1, DIFF_DV), lambda gs: (layer, 0, 0)),
        ],
        out_specs=pl.BlockSpec((None, group, tq, DIFF_DV), o_map),
        out_shape=jax.ShapeDtypeStruct((heads, batch, seq, DIFF_DV), BF16),
        scratch_shapes=[
            pltpu.VMEM((seq, 2 * tq), F32), pltpu.VMEM((seq, 2 * tq), F32),
            pltpu.VMEM((8, 2 * tq), F32), pltpu.VMEM((8, 2 * tq), F32),
            pltpu.VMEM((DIFF_DV + SUM_ROWS, 2 * tq), F32), pltpu.VMEM((DIFF_DV + SUM_ROWS, 2 * tq), F32),
            pltpu.VMEM((group - 2, tq, DIFF_DV), BF16),
        ],
        compiler_params=_params("arbitrary"),
        name="diff_attn",
    )(diff_lambda, qt, k, bias_kt, vt, vt, subln_g)


def _merge_kernel(x_ref, xb_ref, z_ref, att_ref, mkv_ref,
                  wmq_ref, wga_ref, wgb_ref, wgc_ref, bg_ref,
                  wca_ref, wdo_ref, wmo_ref, wo_ref, g_ref, b_ref,
                  y_ref, yb_ref, o_scr, *, alpha, nsplit):
    xb = xb_ref[...]
    d = D_MODEL
    w = MEM_HEADS * MEM_DH

    def gate(wg_ref, j):
        gl = jnp.dot(xb, wg_ref[...], preferred_element_type=F32) + bg_ref[:, j * d:(j + 1) * d]
        return 1.0 / (1.0 + jnp.exp(-gl))

    mq = (jnp.dot(xb, wmq_ref[...], preferred_element_type=F32) * (MEM_DH ** -0.5)).astype(BF16)
    scs = [lax.dot_general(mq[:, h * MEM_DH:(h + 1) * MEM_DH], mkv_ref[:, h * MEM_DH:(h + 1) * MEM_DH], NT_DIMS,
                           preferred_element_type=F32) for h in range(MEM_HEADS)]
    ga_a = gate(wga_ref, 0) * jnp.dot(z_ref[...], wca_ref[...], preferred_element_type=F32)
    ps = []
    for sc in scs:
        e = jnp.exp(sc - jnp.max(sc, axis=-1, keepdims=True))
        ps.append((e * (1.0 / jnp.sum(e, axis=-1, keepdims=True))).astype(BF16))
    att = jnp.concatenate([att_ref[h] for h in range(att_ref.shape[0])], axis=1)
    gb_b = gate(wgb_ref, 1) * jnp.dot(att, wdo_ref[...], preferred_element_type=F32)
    for h in range(MEM_HEADS):
        o_scr[:, h * MEM_DH:(h + 1) * MEM_DH] = jnp.dot(ps[h], mkv_ref[:, w + h * MEM_DH:w + (h + 1) * MEM_DH],
                                                         preferred_element_type=F32).astype(BF16)
    gc = gate(wgc_ref, 2)
    out_c = jnp.dot(o_scr[...], wmo_ref[...], preferred_element_type=F32)
    merged = (ga_a + gb_b + gc * out_c).astype(BF16)
    rows = merged.shape[0] // nsplit
    for r in range(nsplit):
        rs = slice(r * rows, (r + 1) * rows)
        hmix = jnp.dot(merged[rs], wo_ref[...], preferred_element_type=F32)
        y = _layer_norm(alpha * x_ref[rs, :] + hmix, g_ref[...], b_ref[...])
        y_ref[rs, :] = y
        yb_ref[rs, :] = y.astype(BF16)


def _merge(x, xb, z, att, mkv, w_in, b_gate, w_conv_out, w_diff_out, w_mem_out, w_o, ln_g, ln_b,
           layer, seq, mem_len, alpha, tm=512, nsplit=2):
    t, d = x.shape
    per_b = seq // tm
    row = lambda i: (i, 0)
    wspec = lambda: pl.BlockSpec((None, d, d), lambda i: (layer, 0, 0), pipeline_mode=pl.Buffered(1))
    win = lambda col: pl.BlockSpec((None, d, d), lambda i: (layer, 0, col // d), pipeline_mode=pl.Buffered(1))
    vec = lambda n: pl.BlockSpec((None, 1, n), lambda i: (layer, 0, 0))
    kern = functools.partial(_merge_kernel, alpha=alpha, nsplit=nsplit)
    return pl.pallas_call(
        kern,
        grid=(t // tm,),
        in_specs=[
            pl.BlockSpec((tm, d), row),
            pl.BlockSpec((tm, d), row),
            pl.BlockSpec((tm, d), row),
            pl.BlockSpec((DIFF_HEADS, tm, DIFF_DV), lambda i: (0, i, 0)),
            pl.BlockSpec((None, mem_len, 2 * d), lambda i: (layer, i // per_b, 0)),
            win(COL_MQ), win(COL_GATE), win(COL_GATE + d), win(COL_GATE + 2 * d),
            vec(N_BRANCH * d),
            wspec(), wspec(), wspec(), wspec(),
            vec(d), vec(d),
        ],
        out_specs=[pl.BlockSpec((tm, d), row), pl.BlockSpec((tm, d), row)],
        out_shape=[jax.ShapeDtypeStruct((t, d), F32), jax.ShapeDtypeStruct((t, d), BF16)],
        scratch_shapes=[pltpu.VMEM((tm, d), BF16)],
        compiler_params=_params("parallel"),
        name="merge",
    )(x, xb, z, att, mkv, w_in, w_in, w_in, w_in, b_gate,
      w_conv_out, w_diff_out, w_mem_out, w_o, ln_g, ln_b)


def _mlp_kernel(x_ref, xb_ref, w1_ref, w2_ref, g_ref, b_ref, y_ref, yb_ref, *, alpha, fc, nsplit):
    rows = x_ref.shape[0] // nsplit
    for r in range(nsplit):
        rs = slice(r * rows, (r + 1) * rows)
        xb = xb_ref[rs, :]
        f = jnp.zeros((rows, x_ref.shape[1]), F32)
        for c in range(w1_ref.shape[1] // fc):
            cs = slice(c * fc, (c + 1) * fc)
            hid = jnp.maximum(jnp.dot(xb, w1_ref[:, cs], preferred_element_type=F32), 0.0)
            f = f + jnp.dot((hid * hid).astype(BF16), w2_ref[cs, :], preferred_element_type=F32)
        y = _layer_norm(alpha * x_ref[rs, :] + f, g_ref[...], b_ref[...])
        y_ref[rs, :] = y
        yb_ref[rs, :] = y.astype(BF16)


def _mlp(x, xb, w1, w2, ln_g, ln_b, layer, alpha, tm=1024, fc=2048, nsplit=2):
    t, d = x.shape
    dff = w1.shape[2]
    row = lambda i: (i, 0)
    vec = lambda n: pl.BlockSpec((None, 1, n), lambda i: (layer, 0, 0))
    kern = functools.partial(_mlp_kernel, alpha=alpha, fc=fc, nsplit=nsplit)
    return pl.pallas_call(
        kern,
        grid=(t // tm,),
        in_specs=[
            pl.BlockSpec((tm, d), row),
            pl.BlockSpec((tm, d), row),
            pl.BlockSpec((None, d, dff), lambda i: (layer, 0, 0), pipeline_mode=pl.Buffered(1)),
            pl.BlockSpec((None, dff, d), lambda i: (layer, 0, 0), pipeline_mode=pl.Buffered(1)),
            vec(d), vec(d),
        ],
        out_specs=[pl.BlockSpec((tm, d), row), pl.BlockSpec((tm, d), row)],
        out_shape=[jax.ShapeDtypeStruct((t, d), F32), jax.ShapeDtypeStruct((t, d), BF16)],
        compiler_params=_params("parallel"),
        name="mlp",
    )(x, xb, w1, w2, ln_g, ln_b)


def kernel(x, mem, w_in, b_gate, conv_w, w_conv_out, diff_lambda, subln_g, w_diff_out, rel_bias,
           w_mem_kv, w_mem_out, w_o, ln1_g, ln1_b, w_mlp1, w_mlp2, ln2_g, ln2_b):
    batch, seq, d = x.shape
    mem_len = mem.shape[1]
    depth = w_in.shape[0]
    alpha = (2 * depth) ** 0.25

    w_in_b = w_in.astype(BF16)
    w_conv_out_b = w_conv_out.astype(BF16)
    w_diff_out_b = w_diff_out.astype(BF16)
    w_mem_kv_b = w_mem_kv.astype(BF16)
    w_mem_out_b = w_mem_out.astype(BF16)
    w_o_b = w_o.astype(BF16)
    w_mlp1_b = w_mlp1.astype(BF16)
    w_mlp2_b = w_mlp2.astype(BF16)
    mem_b = mem.reshape(batch * mem_len, d).astype(BF16)
    vec3 = lambda a: a.reshape(depth, 1, a.shape[-1])

    bias_kt = _rel_bias_kt(rel_bias, seq)

    mkv = _matmul_layers(mem_b, w_mem_kv_b, 1024, 1024, BF16)

    xf = x.reshape(batch * seq, d)
    xb = xf.astype(BF16)
    for l in range(depth):
        lam_init = 0.8 - 0.6 * math.exp(-0.3 * l)
        qt, k, vt = _qkv_projection(xb, w_in_b, l, DIFF_HEADS, DIFF_DV, batch, seq, DIFF_DK ** -0.5 * LOG2E)
        att = _diff_attention(qt, k, vt, bias_kt, diff_lambda,
                              vec3(subln_g), l, lam_init).reshape(DIFF_HEADS, batch * seq, DIFF_DV)
        z = _conv_branch(xb, w_in_b, conv_w, l, batch, seq)
        xf, xb = _merge(xf, xb, z, att, mkv, w_in_b, vec3(b_gate), w_conv_out_b, w_diff_out_b,
                        w_mem_out_b, w_o_b, vec3(ln1_g), vec3(ln1_b), l, seq, mem_len, alpha)
        xf, xb = _mlp(xf, xb, w_mlp1_b, w_mlp2_b, vec3(ln2_g), vec3(ln2_b), l, alpha)
    return xf.reshape(batch, seq, d)
```

```python
import functools
import math

import jax
import jax.numpy as jnp
from jax import lax
from jax.experimental import pallas as pl
from jax.experimental.pallas import tpu as pltpu

F32 = jnp.float32
BF16 = jnp.bfloat16

D_MODEL = 1024
CONV_K = 3
DIFF_HEADS = 8
DIFF_DK = 64
DIFF_DV = 2 * DIFF_DK
MEM_HEADS = 4
MEM_DH = 256
N_BRANCH = 3
REL_BUCKETS = 32
REL_MAX_DIST = 128
LN_EPS = 1e-5
LOG2E = math.log2(math.e)
SUM_ROWS = 16

COL_CH, COL_CB, COL_CC = 0, D_MODEL, 2 * D_MODEL
COL_Q, COL_K, COL_V = 3 * D_MODEL, 4 * D_MODEL, 5 * D_MODEL
COL_MQ = 6 * D_MODEL
COL_GATE = 7 * D_MODEL

VMEM_LIMIT = 56 * 1024 * 1024

NT_DIMS = (((1,), (1,)), ((), ()))
TN_DIMS = (((0,), (1,)), ((), ()))


def _params(*sem):
    return pltpu.CompilerParams(dimension_semantics=sem, vmem_limit_bytes=VMEM_LIMIT)


def _layer_norm(y, g, b):
    mu = jnp.mean(y, axis=-1, keepdims=True)
    yc = y - mu
    var = jnp.mean(yc * yc, axis=-1, keepdims=True)
    return yc * lax.rsqrt(var + LN_EPS) * g + b


def _mm_kernel(x_ref, w_ref, o_ref):
    o_ref[...] = jnp.dot(x_ref[...], w_ref[...], preferred_element_type=F32).astype(o_ref.dtype)


def _matmul_layers(x, w_stack, tm, tn, out_dtype):
    m, k = x.shape
    layers, _, n = w_stack.shape
    return pl.pallas_call(
        _mm_kernel,
        grid=(m // tm, layers, n // tn),
        in_specs=[
            pl.BlockSpec((tm, k), lambda i, l, j: (i, 0)),
            pl.BlockSpec((None, k, tn), lambda i, l, j: (l, 0, j)),
        ],
        out_specs=pl.BlockSpec((None, tm, tn), lambda i, l, j: (l, i, j)),
        out_shape=jax.ShapeDtypeStruct((layers, m, n), out_dtype),
        compiler_params=_params("parallel", "arbitrary", "arbitrary"),
        name="matmul_layers",
    )(x, w_stack)


def _qkv_kernel(x_ref, wq_ref, wk_ref, wv_ref, qt_ref, k_ref, vt_ref, *, q_scale):
    heads, dh, _ = qt_ref.shape
    x = x_ref[...]
    qt = (lax.dot_general(wq_ref[...], x, TN_DIMS, preferred_element_type=F32) * q_scale).astype(qt_ref.dtype)
    kk = jnp.dot(x, wk_ref[...], preferred_element_type=F32).astype(k_ref.dtype)
    vt = lax.dot_general(wv_ref[...], x, TN_DIMS, preferred_element_type=F32).astype(vt_ref.dtype)
    for h in range(heads):
        qt_ref[h] = qt[h * dh:(h + 1) * dh, :]
        k_ref[h] = kk[:, h * dh:(h + 1) * dh]
        vt_ref[h] = vt[h * dh:(h + 1) * dh, :]


def _qkv_projection(x, w_stack, layer, heads, dh, batch, seq, q_scale, tm=2048):
    _, k = x.shape
    n = heads * dh
    per_b = seq // tm
    wspec = lambda col0: pl.BlockSpec((None, k, n), lambda i: (layer, 0, col0 // n), pipeline_mode=pl.Buffered(1))
    fm = pl.BlockSpec((heads, None, dh, tm), lambda i: (0, i // per_b, 0, i % per_b))
    tmaj = pl.BlockSpec((heads, None, tm, dh), lambda i: (0, i // per_b, i % per_b, 0))
    return pl.pallas_call(
        functools.partial(_qkv_kernel, q_scale=q_scale),
        grid=(batch * per_b,),
        in_specs=[pl.BlockSpec((tm, k), lambda i: (i, 0)), wspec(COL_Q), wspec(COL_K), wspec(COL_V)],
        out_specs=[fm, tmaj, fm],
        out_shape=[jax.ShapeDtypeStruct((heads, batch, dh, seq), BF16),
                   jax.ShapeDtypeStruct((heads, batch, seq, dh), BF16),
                   jax.ShapeDtypeStruct((heads, batch, dh, seq), BF16)],
        compiler_params=_params("parallel"),
        name="qkv_projection",
    )(x, w_stack, w_stack, w_stack)


def _conv_kernel(x_ref, wh_ref, wb_ref, wc_ref, cw_ref, z_ref, u_ref, *, nsplit):
    s = x_ref.shape[0]
    rows = s // nsplit
    pad = jnp.zeros((8, u_ref.shape[1]), F32)
    u_ref[0:8, :] = pad
    u_ref[s + 8:s + 16, :] = pad
    for r in range(nsplit):
        x = x_ref[r * rows:(r + 1) * rows, :]
        u_ref[8 + r * rows:8 + (r + 1) * rows, :] = (jnp.dot(x, wc_ref[...], preferred_element_type=F32)
                                                     * jnp.dot(x, wh_ref[...], preferred_element_type=F32))
    cw = cw_ref[...]
    for r in range(nsplit):
        lo = r * rows
        cb = jnp.dot(x_ref[lo:lo + rows, :], wb_ref[...], preferred_element_type=F32)
        y = (cw[0:1, :] * u_ref[lo + 7:lo + rows + 7, :] + cw[1:2, :] * u_ref[lo + 8:lo + rows + 8, :]
             + cw[2:3, :] * u_ref[lo + 9:lo + rows + 9, :])
        z_ref[lo:lo + rows, :] = (cb * y).astype(z_ref.dtype)


def _conv_branch(xb, w_in, conv_w, layer, batch, seq, tc=1024, nsplit=2):
    t, d = xb.shape
    nct = D_MODEL // tc
    return pl.pallas_call(
        functools.partial(_conv_kernel, nsplit=nsplit),
        grid=(batch, nct),
        in_specs=[
            pl.BlockSpec((seq, d), lambda b, j: (b, 0)),
            pl.BlockSpec((None, d, tc), lambda b, j: (layer, 0, COL_CH // tc + j)),
            pl.BlockSpec((None, d, tc), lambda b, j: (layer, 0, COL_CB // tc + j)),
            pl.BlockSpec((None, d, tc), lambda b, j: (layer, 0, COL_CC // tc + j)),
            pl.BlockSpec((None, CONV_K, tc), lambda b, j: (layer, 0, j)),
        ],
        out_specs=pl.BlockSpec((seq, tc), lambda b, j: (b, j)),
        out_shape=jax.ShapeDtypeStruct((t, D_MODEL), BF16),
        scratch_shapes=[pltpu.VMEM((seq + 16, tc), F32)],
        compiler_params=_params("parallel", "arbitrary"),
        name="conv_branch",
    )(xb, w_in, w_in, w_in, conv_w)


def _bias_kernel(rel_ref, lo_ref, hi_ref, wbkt_ref, o_ref, win_ref, *, rows):
    h = pl.program_id(0)
    j = pl.program_id(1)
    s, tq = o_ref.shape

    @pl.when(j == 0)
    def _():
        bkt = wbkt_ref[...]
        acc = jnp.zeros(bkt.shape, F32)
        for b in range(REL_BUCKETS):
            acc = jnp.where(bkt == b, rel_ref[b, h], acc)
        win_ref[...] = acc * LOG2E

    def chunk(c, carry):
        r0 = pl.multiple_of(c * rows, rows)
        lo = lo_ref[c, j]

        @pl.when(lo == hi_ref[c, j])
        def _():
            o_ref[pl.ds(r0, rows), :] = jnp.full((rows, tq), rel_ref[lo, h], F32) * LOG2E

        @pl.when(lo != hi_ref[c, j])
        def _():
            w0 = jnp.clip(c * rows - j * tq + REL_MAX_DIST, 0, win_ref.shape[0] - rows)
            o_ref[pl.ds(r0, rows), :] = win_ref[pl.ds(pl.multiple_of(w0, rows), rows), :]

        return carry

    lax.fori_loop(0, s // rows, chunk, 0)


def _rel_bias_kt(rel_bias, seq, tq=256, rows=REL_MAX_DIST):
    wrows = tq + 2 * REL_MAX_DIST
    r = jnp.arange(wrows, dtype=jnp.int32)[:, None]
    c = jnp.arange(tq, dtype=jnp.int32)[None, :]
    win_bkt = _t5_bucket(r - REL_MAX_DIST - c)
    k0 = jnp.arange(seq // rows, dtype=jnp.int32)[:, None, None] * rows
    q0 = jnp.arange(seq // tq, dtype=jnp.int32)[None, :, None] * tq
    span = _t5_bucket(k0 - q0 - (tq - 1) + jnp.arange(rows + tq - 1, dtype=jnp.int32)[None, None, :])
    smem = pl.BlockSpec(memory_space=pltpu.SMEM)
    return pl.pallas_call(
        functools.partial(_bias_kernel, rows=rows),
        grid=(DIFF_HEADS, seq // tq),
        in_specs=[smem, smem, smem, pl.BlockSpec((wrows, tq), lambda h, j: (0, 0))],
        out_specs=pl.BlockSpec((None, seq, tq), lambda h, j: (h, 0, j)),
        out_shape=jax.ShapeDtypeStruct((DIFF_HEADS, seq, seq), F32),
        scratch_shapes=[pltpu.VMEM((wrows, tq), F32)],
        compiler_params=_params("parallel", "arbitrary"),
        name="rel_bias",
    )(rel_bias, span.min(axis=-1), span.max(axis=-1), win_bkt)


def _t5_bucket(rel):
    nb = REL_BUCKETS // 2
    max_exact = nb // 2
    ret = (rel > 0).astype(jnp.int32) * nb
    n = jnp.abs(rel)
    nf = jnp.maximum(n, 1).astype(F32)
    large = max_exact + (jnp.log(nf / max_exact) / math.log(REL_MAX_DIST / max_exact)
                         * (nb - max_exact)).astype(jnp.int32)
    large = jnp.minimum(large, nb - 1)
    return ret + jnp.where(n < max_exact, n, large)


def _diff_attn_kernel(lam_ref, qt_ref, k_ref, bias_ref, vt_ref, vtp_ref, g_ref, o_ref,
                      s0_ref, s1_ref, m0_ref, m1_ref, a0_ref, a1_ref, fin_ref, *, lam_init, kc, pw):
    group, _, tq = qt_ref.shape
    s = k_ref.shape[1]
    n = 2 * tq
    step = pl.program_id(0)

    @pl.when(step == 0)
    def _():
        s1_ref[...] = jnp.zeros(s1_ref.shape, F32)
        m1_ref[...] = jnp.zeros(m1_ref.shape, F32)
        a0_ref[...] = jnp.ones(a0_ref.shape, F32)
        fin_ref[...] = jnp.zeros(fin_ref.shape, BF16)

    for j in range(group - 2):
        o_ref[j] = fin_ref[j]

    def sub_tile(j):
        sw_ref, mw_ref, sr_ref, mr_ref = (s0_ref, m0_ref, s1_ref, m1_ref) if j % 2 == 0 else (s1_ref, m1_ref, s0_ref, m0_ref)
        aw_ref, ar_ref = (a1_ref, a0_ref) if j % 2 == 0 else (a0_ref, a1_ref)

        qt = qt_ref[j]
        row = lax.broadcasted_iota(jnp.int32, qt.shape, 0)
        zero = jnp.zeros_like(qt)
        q12 = jnp.concatenate([jnp.where(row < DIFF_DK, qt, zero), jnp.where(row >= DIFF_DK, qt, zero)], axis=1)
        m = jnp.max(mr_ref[...], axis=0, keepdims=True)
        orow = lax.broadcasted_iota(jnp.int32, (SUM_ROWS, kc), 0)
        ones = jnp.where(orow == 0, 1.0, 0.0).astype(BF16)
        vt_prev = vtp_ref if j == 0 else vt_ref.at[j - 1]
        nparts = n // pw
        pvs = [jnp.zeros((DIFF_DV + SUM_ROWS, pw), F32) for _ in range(nparts)]
        mxs = [jnp.full((8, pw), -jnp.inf, F32) for _ in range(nparts)]
        for c in range(s // kc):
            sl = slice(c * kc, (c + 1) * kc)
            vt = jnp.concatenate([vt_prev[:, sl], ones], axis=0)
            kb = k_ref[j, sl, :]
            for p in range(nparts):
                cs = slice(p * pw, (p + 1) * pw)
                qs = slice(p * pw % tq, p * pw % tq + pw)
                e = jnp.exp2(sr_ref[sl, cs] - m[:, cs]).astype(BF16)
                pvs[p] = pvs[p] + jnp.dot(vt, e, preferred_element_type=F32)
                sc = jnp.dot(kb, q12[:, cs], preferred_element_type=F32) + bias_ref[sl, qs]
                sw_ref[sl, cs] = sc
                mxs[p] = jnp.maximum(mxs[p], jnp.max(sc.reshape(kc // 8, 8, pw), axis=0))
        mw_ref[...] = jnp.concatenate(mxs, axis=1)
        aw_ref[...] = jnp.concatenate(pvs, axis=1)

        acc = ar_ref[...]
        lf = lam_ref[...]
        lam = (jnp.exp(jnp.sum(lf[0:1, :] * lf[1:2, :], axis=-1, keepdims=True))
               - jnp.exp(jnp.sum(lf[2:3, :] * lf[3:4, :], axis=-1, keepdims=True)) + lam_init)
        r = 1.0 / acc[DIFF_DV:DIFF_DV + 1, :]
        ot = acc[:DIFF_DV, :tq] * r[:, :tq] - acc[:DIFF_DV, tq:] * (lam * r[:, tq:])
        o = ot.T
        ms = jnp.mean(o * o, axis=-1, keepdims=True)
        res = (o * lax.rsqrt(ms + LN_EPS) * g_ref[...] * (1.0 - lam_init)).astype(o_ref.dtype)
        if j < 2:
            o_ref[group - 2 + j] = res
        else:
            fin_ref[j - 2] = res

    def pair(j0):
        sub_tile(j0)
        sub_tile(j0 + 1)

    for j0 in range(0, group, 2):
        pl.when(step >= j0 - group)(functools.partial(pair, j0))


def _diff_attention(qt, k, vt, bias_kt, diff_lambda, subln_g, layer, lam_init, tq=512, kc=256, pw=256, group=4):
    heads, batch, seq, _ = k.shape
    nqt = seq // tq
    nbg = batch // group
    nreal = heads * nqt * nbg

    def dec(gs):
        return gs // (nqt * nbg), (gs // nbg) % nqt, gs % nbg

    cur = lambda gs: dec(jnp.minimum(gs, nreal - 1))
    out = lambda gs: dec(jnp.maximum(gs - 1, 0))

    def qt_map(gs):
        h, i, bg = cur(gs)
        return (h, bg, 0, i)

    def kv_map(gs):
        h, i, bg = cur(gs)
        return (h, bg, 0, 0)

    def bias_map(gs):
        h, i, bg = cur(gs)
        return (h, 0, i)

    def vtp_map(gs):
        h, i, bg = dec(jnp.maximum(gs - 1, 0))
        return (h, bg * group + group - 1, 0, 0)

    def o_map(gs):
        h, i, bg = out(gs)
        return (h, bg, i, 0)

    kern = functools.partial(_diff_attn_kernel, lam_init=lam_init, kc=kc, pw=pw)
    return pl.pallas_call(
        kern,
        grid=(nreal + 1,),
        in_specs=[
            pl.BlockSpec((None, 4, DIFF_DK), lambda gs: (layer, 0, 0)),
            pl.BlockSpec((None, group, DIFF_DV, tq), qt_map),
            pl.BlockSpec((None, group, seq, DIFF_DV), kv_map),
            pl.BlockSpec((None, seq, tq), bias_map),
            pl.BlockSpec((None, group, DIFF_DV, seq), kv_map),
            pl.BlockSpec((None, None, DIFF_DV, seq), vtp_map),
            pl.BlockSpec((None, 1, DIFF_DV), lambda gs: (layer, 0, 0)),
        ],
        out_specs=pl.BlockSpec((None, group, tq, DIFF_DV), o_map),
        out_shape=jax.ShapeDtypeStruct((heads, batch, seq, DIFF_DV), BF16),
        scratch_shapes=[
            pltpu.VMEM((seq, 2 * tq), F32), pltpu.VMEM((seq, 2 * tq), F32),
            pltpu.VMEM((8, 2 * tq), F32), pltpu.VMEM((8, 2 * tq), F32),
            pltpu.VMEM((DIFF_DV + SUM_ROWS, 2 * tq), F32), pltpu.VMEM((DIFF_DV + SUM_ROWS, 2 * tq), F32),
            pltpu.VMEM((group - 2, tq, DIFF_DV), BF16),
        ],
        compiler_params=_params("arbitrary"),
        name="diff_attn",
    )(diff_lambda, qt, k, bias_kt, vt, vt, subln_g)


def _merge_kernel(x_ref, xb_ref, z_ref, att_ref, mkv_ref,
                  wmq_ref, wga_ref, wgb_ref, wgc_ref, bg_ref,
                  wca_ref, wdo_ref, wmo_ref, wo_ref, g_ref, b_ref,
                  y_ref, yb_ref, o_scr, *, alpha, nsplit):
    xb = xb_ref[...]
    d = D_MODEL
    w = MEM_HEADS * MEM_DH

    def gate(wg_ref, j):
        gl = jnp.dot(xb, wg_ref[...], preferred_element_type=F32) + bg_ref[:, j * d:(j + 1) * d]
        return 1.0 / (1.0 + jnp.exp(-gl))

    mq = (jnp.dot(xb, wmq_ref[...], preferred_element_type=F32) * (MEM_DH ** -0.5)).astype(BF16)
    scs = [lax.dot_general(mq[:, h * MEM_DH:(h + 1) * MEM_DH], mkv_ref[:, h * MEM_DH:(h + 1) * MEM_DH], NT_DIMS,
                           preferred_element_type=F32) for h in range(MEM_HEADS)]
    ga_a = gate(wga_ref, 0) * jnp.dot(z_ref[...], wca_ref[...], preferred_element_type=F32)
    ps = []
    for sc in scs:
        e = jnp.exp(sc - jnp.max(sc, axis=-1, keepdims=True))
        ps.append((e * (1.0 / jnp.sum(e, axis=-1, keepdims=True))).astype(BF16))
    att = jnp.concatenate([att_ref[h] for h in range(att_ref.shape[0])], axis=1)
    gb_b = gate(wgb_ref, 1) * jnp.dot(att, wdo_ref[...], preferred_element_type=F32)
    for h in range(MEM_HEADS):
        o_scr[:, h * MEM_DH:(h + 1) * MEM_DH] = jnp.dot(ps[h], mkv_ref[:, w + h * MEM_DH:w + (h + 1) * MEM_DH],
                                                         preferred_element_type=F32).astype(BF16)
    gc = gate(wgc_ref, 2)
    out_c = jnp.dot(o_scr[...], wmo_ref[...], preferred_element_type=F32)
    merged = (ga_a + gb_b + gc * out_c).astype(BF16)
    rows = merged.shape[0] // nsplit
    for r in range(nsplit):
        rs = slice(r * rows, (r + 1) * rows)
        hmix = jnp.dot(merged[rs], wo_ref[...], preferred_element_type=F32)
        y = _layer_norm(alpha * x_ref[rs, :] + hmix, g_ref[...], b_ref[...])
        y_ref[rs, :] = y
        yb_ref[rs, :] = y.astype(BF16)


def _merge(x, xb, z, att, mkv, w_in, b_gate, w_conv_out, w_diff_out, w_mem_out, w_o, ln_g, ln_b,
           layer, seq, mem_len, alpha, tm=512, nsplit=2):
    t, d = x.shape
    per_b = seq // tm
    row = lambda i: (i, 0)
    wspec = lambda: pl.BlockSpec((None, d, d), lambda i: (layer, 0, 0), pipeline_mode=pl.Buffered(1))
    win = lambda col: pl.BlockSpec((None, d, d), lambda i: (layer, 0, col // d), pipeline_mode=pl.Buffered(1))
    vec = lambda n: pl.BlockSpec((None, 1, n), lambda i: (layer, 0, 0))
    kern = functools.partial(_merge_kernel, alpha=alpha, nsplit=nsplit)
    return pl.pallas_call(
        kern,
        grid=(t // tm,),
        in_specs=[
            pl.BlockSpec((tm, d), row),
            pl.BlockSpec((tm, d), row),
            pl.BlockSpec((tm, d), row),
            pl.BlockSpec((DIFF_HEADS, tm, DIFF_DV), lambda i: (0, i, 0)),
            pl.BlockSpec((None, mem_len, 2 * d), lambda i: (layer, i // per_b, 0)),
            win(COL_MQ), win(COL_GATE), win(COL_GATE + d), win(COL_GATE + 2 * d),
            vec(N_BRANCH * d),
            wspec(), wspec(), wspec(), wspec(),
            vec(d), vec(d),
        ],
        out_specs=[pl.BlockSpec((tm, d), row), pl.BlockSpec((tm, d), row)],
        out_shape=[jax.ShapeDtypeStruct((t, d), F32), jax.ShapeDtypeStruct((t, d), BF16)],
        scratch_shapes=[pltpu.VMEM((tm, d), BF16)],
        compiler_params=_params("parallel"),
        name="merge",
    )(x, xb, z, att, mkv, w_in, w_in, w_in, w_in, b_gate,
      w_conv_out, w_diff_out, w_mem_out, w_o, ln_g, ln_b)


def _mlp_kernel(x_ref, xb_ref, w1_ref, w2_ref, g_ref, b_ref, y_ref, yb_ref, *, alpha, fc, nsplit):
    rows = x_ref.shape[0] // nsplit
    for r in range(nsplit):
        rs = slice(r * rows, (r + 1) * rows)
        xb = xb_ref[rs, :]
        f = jnp.zeros((rows, x_ref.shape[1]), F32)
        for c in range(w1_ref.shape[1] // fc):
            cs = slice(c * fc, (c + 1) * fc)
            hid = jnp.maximum(jnp.dot(xb, w1_ref[:, cs], preferred_element_type=F32), 0.0)
            f = f + jnp.dot((hid * hid).astype(BF16), w2_ref[cs, :], preferred_element_type=F32)
        y = _layer_norm(alpha * x_ref[rs, :] + f, g_ref[...], b_ref[...])
        y_ref[rs, :] = y
        yb_ref[rs, :] = y.astype(BF16)


def _mlp(x, xb, w1, w2, ln_g, ln_b, layer, alpha, tm=1024, fc=2048, nsplit=2):
    t, d = x.shape
    dff = w1.shape[2]
    row = lambda i: (i, 0)
    vec = lambda n: pl.BlockSpec((None, 1, n), lambda i: (layer, 0, 0))
    kern = functools.partial(_mlp_kernel, alpha=alpha, fc=fc, nsplit=nsplit)
    return pl.pallas_call(
        kern,
        grid=(t // tm,),
        in_specs=[
            pl.BlockSpec((tm, d), row),
            pl.BlockSpec((tm, d), row),
            pl.BlockSpec((None, d, dff), lambda i: (layer, 0, 0), pipeline_mode=pl.Buffered(1)),
            pl.BlockSpec((None, dff, d), lambda i: (layer, 0, 0), pipeline_mode=pl.Buffered(1)),
            vec(d), vec(d),
        ],
        out_specs=[pl.BlockSpec((tm, d), row), pl.BlockSpec((tm, d), row)],
        out_shape=[jax.ShapeDtypeStruct((t, d), F32), jax.ShapeDtypeStruct((t, d), BF16)],
        compiler_params=_params("parallel"),
        name="mlp",
    )(x, xb, w1, w2, ln_g, ln_b)


def kernel(x, mem, w_in, b_gate, conv_w, w_conv_out, diff_lambda, subln_g, w_diff_out, rel_bias,
           w_mem_kv, w_mem_out, w_o, ln1_g, ln1_b, w_mlp1, w_mlp2, ln2_g, ln2_b):
    batch, seq, d = x.shape
    mem_len = mem.shape[1]
    depth = w_in.shape[0]
    alpha = (2 * depth) ** 0.25

    w_in_b = w_in.astype(BF16)
    w_conv_out_b = w_conv_out.astype(BF16)
    w_diff_out_b = w_diff_out.astype(BF16)
    w_mem_kv_b = w_mem_kv.astype(BF16)
    w_mem_out_b = w_mem_out.astype(BF16)
    w_o_b = w_o.astype(BF16)
    w_mlp1_b = w_mlp1.astype(BF16)
    w_mlp2_b = w_mlp2.astype(BF16)
    mem_b = mem.reshape(batch * mem_len, d).astype(BF16)
    vec3 = lambda a: a.reshape(depth, 1, a.shape[-1])

    bias_kt = _rel_bias_kt(rel_bias, seq)

    mkv = _matmul_layers(mem_b, w_mem_kv_b, 1024, 1024, BF16)

    xf = x.reshape(batch * seq, d)
    xb = xf.astype(BF16)
    for l in range(depth):
        lam_init = 0.8 - 0.6 * math.exp(-0.3 * l)
        qt, k, vt = _qkv_projection(xb, w_in_b, l, DIFF_HEADS, DIFF_DV, batch, seq, DIFF_DK ** -0.5 * LOG2E)
        att = _diff_attention(qt, k, vt, bias_kt, diff_lambda,
                              vec3(subln_g), l, lam_init).reshape(DIFF_HEADS, batch * seq, DIFF_DV)
        z = _conv_branch(xb, w_in_b, conv_w, l, batch, seq)
        xf, xb = _merge(xf, xb, z, att, mkv, w_in_b, vec3(b_gate), w_conv_out_b, w_diff_out_b,
                        w_mem_out_b, w_o_b, vec3(ln1_g), vec3(ln1_b), l, seq, mem_len, alpha)
        xf, xb = _mlp(xf, xb, w_mlp1_b, w_mlp2_b, vec3(ln2_g), vec3(ln2_b), l, alpha)
    return xf.reshape(batch, seq, d)
```

```python
import functools
import math

import jax
import jax.numpy as jnp
from jax import lax
from jax.experimental import pallas as pl
from jax.experimental.pallas import tpu as pltpu

F32 = jnp.float32
BF16 = jnp.bfloat16

D_MODEL = 1024
CONV_K = 3
DIFF_HEADS = 8
DIFF_DK = 64
DIFF_DV = 2 * DIFF_DK
MEM_HEADS = 4
MEM_DH = 256
N_BRANCH = 3
REL_BUCKETS = 32
REL_MAX_DIST = 128
LN_EPS = 1e-5
LOG2E = math.log2(math.e)
SUM_ROWS = 16

COL_CH, COL_CB, COL_CC = 0, D_MODEL, 2 * D_MODEL
COL_Q, COL_K, COL_V = 3 * D_MODEL, 4 * D_MODEL, 5 * D_MODEL
COL_MQ = 6 * D_MODEL
COL_GATE = 7 * D_MODEL

VMEM_LIMIT = 56 * 1024 * 1024

NT_DIMS = (((1,), (1,)), ((), ()))
TN_DIMS = (((0,), (1,)), ((), ()))


def _params(*sem):
    return pltpu.CompilerParams(dimension_semantics=sem, vmem_limit_bytes=VMEM_LIMIT)


def _layer_norm(y, g, b):
    mu = jnp.mean(y, axis=-1, keepdims=True)
    yc = y - mu
    var = jnp.mean(yc * yc, axis=-1, keepdims=True)
    return yc * lax.rsqrt(var + LN_EPS) * g + b


def _mm_kernel(x_ref, w_ref, o_ref):
    o_ref[...] = jnp.dot(x_ref[...], w_ref[...], preferred_element_type=F32).astype(o_ref.dtype)


def _matmul_layers(x, w_stack, tm, tn, out_dtype):
    m, k = x.shape
    layers, _, n = w_stack.shape
    return pl.pallas_call(
        _mm_kernel,
        grid=(m // tm, layers, n // tn),
        in_specs=[
            pl.BlockSpec((tm, k), lambda i, l, j: (i, 0)),
            pl.BlockSpec((None, k, tn), lambda i, l, j: (l, 0, j)),
        ],
        out_specs=pl.BlockSpec((None, tm, tn), lambda i, l, j: (l, i, j)),
        out_shape=jax.ShapeDtypeStruct((layers, m, n), out_dtype),
        compiler_params=_params("parallel", "arbitrary", "arbitrary"),
        name="matmul_layers",
    )(x, w_stack)


def _qkv_kernel(x_ref, wq_ref, wk_ref, wv_ref, qt_ref, k_ref, vt_ref, *, q_scale):
    heads, dh, _ = qt_ref.shape
    x = x_ref[...]
    qt = (lax.dot_general(wq_ref[...], x, TN_DIMS, preferred_element_type=F32) * q_scale).astype(qt_ref.dtype)
    kk = jnp.dot(x, wk_ref[...], preferred_element_type=F32).astype(k_ref.dtype)
    vt = lax.dot_general(wv_ref[...], x, TN_DIMS, preferred_element_type=F32).astype(vt_ref.dtype)
    for h in range(heads):
        qt_ref[h] = qt[h * dh:(h + 1) * dh, :]
        k_ref[h] = kk[:, h * dh:(h + 1) * dh]
        vt_ref[h] = vt[h * dh:(h + 1) * dh, :]


def _qkv_projection(x, w_stack, layer, heads, dh, batch, seq, q_scale, tm=2048):
    _, k = x.shape
    n = heads * dh
    per_b = seq // tm
    wspec = lambda col0: pl.BlockSpec((None, k, n), lambda i: (layer, 0, col0 // n), pipeline_mode=pl.Buffered(1))
    fm = pl.BlockSpec((heads, None, dh, tm), lambda i: (0, i // per_b, 0, i % per_b))
    tmaj = pl.BlockSpec((heads, None, tm, dh), lambda i: (0, i // per_b, i % per_b, 0))
    return pl.pallas_call(
        functools.partial(_qkv_kernel, q_scale=q_scale),
        grid=(batch * per_b,),
        in_specs=[pl.BlockSpec((tm, k), lambda i: (i, 0)), wspec(COL_Q), wspec(COL_K), wspec(COL_V)],
        out_specs=[fm, tmaj, fm],
        out_shape=[jax.ShapeDtypeStruct((heads, batch, dh, seq), BF16),
                   jax.ShapeDtypeStruct((heads, batch, seq, dh), BF16),
                   jax.ShapeDtypeStruct((heads, batch, dh, seq), BF16)],
        compiler_params=_params("parallel"),
        name="qkv_projection",
    )(x, w_stack, w_stack, w_stack)


def _conv_kernel(x_ref, wh_ref, wb_ref, wc_ref, cw_ref, z_ref, u_ref, *, nsplit):
    s = x_ref.shape[0]
    rows = s // nsplit
    pad = jnp.zeros((8, u_ref.shape[1]), F32)
    u_ref[0:8, :] = pad
    u_ref[s + 8:s + 16, :] = pad
    for r in range(nsplit):
        x = x_ref[r * rows:(r + 1) * rows, :]
        u_ref[8 + r * rows:8 + (r + 1) * rows, :] = (jnp.dot(x, wc_ref[...], preferred_element_type=F32)
                                                     * jnp.dot(x, wh_ref[...], preferred_element_type=F32))
    cw = cw_ref[...]
    for r in range(nsplit):
        lo = r * rows
        cb = jnp.dot(x_ref[lo:lo + rows, :], wb_ref[...], preferred_element_type=F32)
        y = (cw[0:1, :] * u_ref[lo + 7:lo + rows + 7, :] + cw[1:2, :] * u_ref[lo + 8:lo + rows + 8, :]
             + cw[2:3, :] * u_ref[lo + 9:lo + rows + 9, :])
        z_ref[lo:lo + rows, :] = (cb * y).astype(z_ref.dtype)


def _conv_branch(xb, w_in, conv_w, layer, batch, seq, tc=1024, nsplit=2):
    t, d = xb.shape
    nct = D_MODEL // tc
    return pl.pallas_call(
        functools.partial(_conv_kernel, nsplit=nsplit),
        grid=(batch, nct),
        in_specs=[
            pl.BlockSpec((seq, d), lambda b, j: (b, 0)),
            pl.BlockSpec((None, d, tc), lambda b, j: (layer, 0, COL_CH // tc + j)),
            pl.BlockSpec((None, d, tc), lambda b, j: (layer, 0, COL_CB // tc + j)),
            pl.BlockSpec((None, d, tc), lambda b, j: (layer, 0, COL_CC // tc + j)),
            pl.BlockSpec((None, CONV_K, tc), lambda b, j: (layer, 0, j)),
        ],
        out_specs=pl.BlockSpec((seq, tc), lambda b, j: (b, j)),
        out_shape=jax.ShapeDtypeStruct((t, D_MODEL), BF16),
        scratch_shapes=[pltpu.VMEM((seq + 16, tc), F32)],
        compiler_params=_params("parallel", "arbitrary"),
        name="conv_branch",
    )(xb, w_in, w_in, w_in, conv_w)


def _bias_kernel(rel_ref, lo_ref, hi_ref, wbkt_ref, o_ref, win_ref, *, rows):
    h = pl.program_id(0)
    j = pl.program_id(1)
    s, tq = o_ref.shape

    @pl.when(j == 0)
    def _():
        bkt = wbkt_ref[...]
        acc = jnp.zeros(bkt.shape, F32)
        for b in range(REL_BUCKETS):
            acc = jnp.where(bkt == b, rel_ref[b, h], acc)
        win_ref[...] = acc * LOG2E

    def chunk(c, carry):
        r0 = pl.multiple_of(c * rows, rows)
        lo = lo_ref[c, j]

        @pl.when(lo == hi_ref[c, j])
        def _():
            o_ref[pl.ds(r0, rows), :] = jnp.full((rows, tq), rel_ref[lo, h], F32) * LOG2E

        @pl.when(lo != hi_ref[c, j])
        def _():
            w0 = jnp.clip(c * rows - j * tq + REL_MAX_DIST, 0, win_ref.shape[0] - rows)
            o_ref[pl.ds(r0, rows), :] = win_ref[pl.ds(pl.multiple_of(w0, rows), rows), :]

        return carry

    lax.fori_loop(0, s // rows, chunk, 0)


def _rel_bias_kt(rel_bias, seq, tq=256, rows=REL_MAX_DIST):
    wrows = tq + 2 * REL_MAX_DIST
    r = jnp.arange(wrows, dtype=jnp.int32)[:, None]
    c = jnp.arange(tq, dtype=jnp.int32)[None, :]
    win_bkt = _t5_bucket(r - REL_MAX_DIST - c)
    k0 = jnp.arange(seq // rows, dtype=jnp.int32)[:, None, None] * rows
    q0 = jnp.arange(seq // tq, dtype=jnp.int32)[None, :, None] * tq
    span = _t5_bucket(k0 - q0 - (tq - 1) + jnp.arange(rows + tq - 1, dtype=jnp.int32)[None, None, :])
    smem = pl.BlockSpec(memory_space=pltpu.SMEM)
    return pl.pallas_call(
        functools.partial(_bias_kernel, rows=rows),
        grid=(DIFF_HEADS, seq // tq),
        in_specs=[smem, smem, smem, pl.BlockSpec((wrows, tq), lambda h, j: (0, 0))],
        out_specs=pl.BlockSpec((None, seq, tq), lambda h, j: (h, 0, j)),
        out_shape=jax.ShapeDtypeStruct((DIFF_HEADS, seq, seq), F32),
        scratch_shapes=[pltpu.VMEM((wrows, tq), F32)],
        compiler_params=_params("parallel", "arbitrary"),
        name="rel_bias",
    )(rel_bias, span.min(axis=-1), span.max(axis=-1), win_bkt)


def _t5_bucket(rel):
    nb = REL_BUCKETS // 2
    max_exact = nb // 2
    ret = (rel > 0).astype(jnp.int32) * nb
    n = jnp.abs(rel)
    nf = jnp.maximum(n, 1).astype(F32)
    large = max_exact + (jnp.log(nf / max_exact) / math.log(REL_MAX_DIST / max_exact)
                         * (nb - max_exact)).astype(jnp.int32)
    large = jnp.minimum(large, nb - 1)
    return ret + jnp.where(n < max_exact, n, large)


def _diff_attn_kernel(lam_ref, qt_ref, k_ref, bias_ref, vt_ref, vtp_ref, g_ref, o_ref,
                      s0_ref, s1_ref, m0_ref, m1_ref, a0_ref, a1_ref, fin_ref, *, lam_init, kc, pw, nreal):
    group, _, tq = qt_ref.shape
    s = k_ref.shape[1]
    n = 2 * tq
    step = pl.program_id(0)

    @pl.when(step == 0)
    def _():
        s1_ref[...] = jnp.zeros(s1_ref.shape, F32)
        m1_ref[...] = jnp.zeros(m1_ref.shape, F32)
        a0_ref[...] = jnp.ones(a0_ref.shape, F32)
        fin_ref[...] = jnp.zeros(fin_ref.shape, BF16)

    for j in range(group - 2):
        o_ref[j] = fin_ref[j]

    def sub_tile(j):
        sw_ref, mw_ref, sr_ref, mr_ref = (s0_ref, m0_ref, s1_ref, m1_ref) if j % 2 == 0 else (s1_ref, m1_ref, s0_ref, m0_ref)
        aw_ref, ar_ref = (a1_ref, a0_ref) if j % 2 == 0 else (a0_ref, a1_ref)

        qt = qt_ref[j]
        row = lax.broadcasted_iota(jnp.int32, qt.shape, 0)
        zero = jnp.zeros_like(qt)
        q12 = jnp.concatenate([jnp.where(row < DIFF_DK, qt, zero), jnp.where(row >= DIFF_DK, qt, zero)], axis=1)
        m = jnp.max(mr_ref[...], axis=0, keepdims=True)
        orow = lax.broadcasted_iota(jnp.int32, (SUM_ROWS, kc), 0)
        ones = jnp.where(orow == 0, 1.0, 0.0).astype(BF16)
        vt_prev = vtp_ref if j == 0 else vt_ref.at[j - 1]
        nparts = n // pw
        pvs = [jnp.zeros((DIFF_DV + SUM_ROWS, pw), F32) for _ in range(nparts)]
        mxs = [jnp.full((8, pw), -jnp.inf, F32) for _ in range(nparts)]
        for c in range(s // kc):
            sl = slice(c * kc, (c + 1) * kc)
            vt = jnp.concatenate([vt_prev[:, sl], ones], axis=0)
            kb = k_ref[j, sl, :]
            for p in range(nparts):
                cs = slice(p * pw, (p + 1) * pw)
                qs = slice(p * pw % tq, p * pw % tq + pw)
                e = jnp.exp2(sr_ref[sl, cs] - m[:, cs]).astype(BF16)
                pvs[p] = pvs[p] + jnp.dot(vt, e, preferred_element_type=F32)
                sc = jnp.dot(kb, q12[:, cs], preferred_element_type=F32) + bias_ref[sl, qs]
                sw_ref[sl, cs] = sc
                mxs[p] = jnp.maximum(mxs[p], jnp.max(sc.reshape(kc // 8, 8, pw), axis=0))
        mw_ref[...] = jnp.concatenate(mxs, axis=1)
        aw_ref[...] = jnp.concatenate(pvs, axis=1)

        acc = ar_ref[...]
        lf = lam_ref[...]
        lam = (jnp.exp(jnp.sum(lf[0:1, :] * lf[1:2, :], axis=-1, keepdims=True))
               - jnp.exp(jnp.sum(lf[2:3, :] * lf[3:4, :], axis=-1, keepdims=True)) + lam_init)
        r = 1.0 / acc[DIFF_DV:DIFF_DV + 1, :]
        ot = acc[:DIFF_DV, :tq] * r[:, :tq] - acc[:DIFF_DV, tq:] * (lam * r[:, tq:])
        o = ot.T
        ms = jnp.mean(o * o, axis=-1, keepdims=True)
        res = (o * lax.rsqrt(ms + LN_EPS) * g_ref[...] * (1.0 - lam_init)).astype(o_ref.dtype)
        if j < 2:
            o_ref[group - 2 + j] = res
        else:
            fin_ref[j - 2] = res

    def pair(j0):
        sub_tile(j0)
        sub_tile(j0 + 1)

    assert group == 4
    pl.when(step >= -group)(functools.partial(pair, 0))
    pl.when(step < nreal)(functools.partial(pair, 2))


def _diff_attention(qt, k, vt, bias_kt, diff_lambda, subln_g, layer, lam_init, tq=512, kc=512, pw=256, group=4):
    heads, batch, seq, _ = k.shape
    nqt = seq // tq
    nbg = batch // group
    nreal = heads * nqt * nbg

    def dec(gs):
        return gs // (nqt * nbg), (gs // nbg) % nqt, gs % nbg

    cur = lambda gs: dec(jnp.minimum(gs, nreal - 1))
    out = lambda gs: dec(jnp.maximum(gs - 1, 0))

    def qt_map(gs):
        h, i, bg = cur(gs)
        return (h, bg, 0, i)

    def kv_map(gs):
        h, i, bg = cur(gs)
        return (h, bg, 0, 0)

    def bias_map(gs):
        h, i, bg = cur(gs)
        return (h, 0, i)

    def vtp_map(gs):
        h, i, bg = dec(jnp.maximum(gs - 1, 0))
        return (h, bg * group + group - 1, 0, 0)

    def o_map(gs):
        h, i, bg = out(gs)
        return (h, bg, i, 0)

    kern = functools.partial(_diff_attn_kernel, lam_init=lam_init, kc=kc, pw=pw, nreal=nreal)
    return pl.pallas_call(
        kern,
        grid=(nreal + 1,),
        in_specs=[
            pl.BlockSpec((None, 4, DIFF_DK), lambda gs: (layer, 0, 0)),
            pl.BlockSpec((None, group, DIFF_DV, tq), qt_map),
            pl.BlockSpec((None, group, seq, DIFF_DV), kv_map),
            pl.BlockSpec((None, seq, tq), bias_map),
            pl.BlockSpec((None, group, DIFF_DV, seq), kv_map),
            pl.BlockSpec((None, None, DIFF_DV, seq), vtp_map),
            pl.BlockSpec((None, 1, DIFF_DV), lambda gs: (layer, 0, 0)),
        ],
        out_specs=pl.BlockSpec((None, group, tq, DIFF_DV), o_map),
        out_shape=jax.ShapeDtypeStruct((heads, batch, seq, DIFF_DV), BF16),
        scratch_shapes=[
            pltpu.VMEM((seq, 2 * tq), F32), pltpu.VMEM((seq, 2 * tq), F32),
            pltpu.VMEM((8, 2 * tq), F32), pltpu.VMEM((8, 2 * tq), F32),
            pltpu.VMEM((DIFF_DV + SUM_ROWS, 2 * tq), F32), pltpu.VMEM((DIFF_DV + SUM_ROWS, 2 * tq), F32),
            pltpu.VMEM((group - 2, tq, DIFF_DV), BF16),
        ],
        compiler_params=_params("arbitrary"),
        name="diff_attn",
    )(diff_lambda, qt, k, bias_kt, vt, vt, subln_g)


def _merge_kernel(x_ref, xb_ref, z_ref, att_ref, mkv_ref,
                  wmq_ref, wga_ref, wgb_ref, wgc_ref, bg_ref,
                  wca_ref, wdo_ref, wmo_ref, wo_ref, g_ref, b_ref,
                  y_ref, yb_ref, o_scr, *, alpha, nsplit):
    xb = xb_ref[...]
    d = D_MODEL
    w = MEM_HEADS * MEM_DH

    def gate(wg_ref, j):
        gl = jnp.dot(xb, wg_ref[...], preferred_element_type=F32) + bg_ref[:, j * d:(j + 1) * d]
        return 1.0 / (1.0 + jnp.exp(-gl))

    mq = (jnp.dot(xb, wmq_ref[...], preferred_element_type=F32) * (MEM_DH ** -0.5)).astype(BF16)
    scs = [lax.dot_general(mq[:, h * MEM_DH:(h + 1) * MEM_DH], mkv_ref[:, h * MEM_DH:(h + 1) * MEM_DH], NT_DIMS,
                           preferred_element_type=F32) for h in range(MEM_HEADS)]
    ga_a = gate(wga_ref, 0) * jnp.dot(z_ref[...], wca_ref[...], preferred_element_type=F32)
    ps = []
    for sc in scs:
        e = jnp.exp(sc - jnp.max(sc, axis=-1, keepdims=True))
        ps.append((e * (1.0 / jnp.sum(e, axis=-1, keepdims=True))).astype(BF16))
    att = jnp.concatenate([att_ref[h] for h in range(att_ref.shape[0])], axis=1)
    gb_b = gate(wgb_ref, 1) * jnp.dot(att, wdo_ref[...], preferred_element_type=F32)
    for h in range(MEM_HEADS):
        o_scr[:, h * MEM_DH:(h + 1) * MEM_DH] = jnp.dot(ps[h], mkv_ref[:, w + h * MEM_DH:w + (h + 1) * MEM_DH],
                                                         preferred_element_type=F32).astype(BF16)
    gc = gate(wgc_ref, 2)
    out_c = jnp.dot(o_scr[...], wmo_ref[...], preferred_element_type=F32)
    merged = (ga_a + gb_b + gc * out_c).astype(BF16)
    rows = merged.shape[0] // nsplit
    for r in range(nsplit):
        rs = slice(r * rows, (r + 1) * rows)
        hmix = jnp.dot(merged[rs], wo_ref[...], preferred_element_type=F32)
        y = _layer_norm(alpha * x_ref[rs, :] + hmix, g_ref[...], b_ref[...])
        y_ref[rs, :] = y
        yb_ref[rs, :] = y.astype(BF16)


def _merge(x, xb, z, att, mkv, w_in, b_gate, w_conv_out, w_diff_out, w_mem_out, w_o, ln_g, ln_b,
           layer, seq, mem_len, alpha, tm=512, nsplit=2):
    t, d = x.shape
    per_b = seq // tm
    row = lambda i: (i, 0)
    wspec = lambda: pl.BlockSpec((None, d, d), lambda i: (layer, 0, 0), pipeline_mode=pl.Buffered(1))
    win = lambda col: pl.BlockSpec((None, d, d), lambda i: (layer, 0, col // d), pipeline_mode=pl.Buffered(1))
    vec = lambda n: pl.BlockSpec((None, 1, n), lambda i: (layer, 0, 0))
    kern = functools.partial(_merge_kernel, alpha=alpha, nsplit=nsplit)
    return pl.pallas_call(
        kern,
        grid=(t // tm,),
        in_specs=[
            pl.BlockSpec((tm, d), row),
            pl.BlockSpec((tm, d), row),
            pl.BlockSpec((tm, d), row),
            pl.BlockSpec((DIFF_HEADS, tm, DIFF_DV), lambda i: (0, i, 0)),
            pl.BlockSpec((None, mem_len, 2 * d), lambda i: (layer, i // per_b, 0)),
            win(COL_MQ), win(COL_GATE), win(COL_GATE + d), win(COL_GATE + 2 * d),
            vec(N_BRANCH * d),
            wspec(), wspec(), wspec(), wspec(),
            vec(d), vec(d),
        ],
        out_specs=[pl.BlockSpec((tm, d), row), pl.BlockSpec((tm, d), row)],
        out_shape=[jax.ShapeDtypeStruct((t, d), F32), jax.ShapeDtypeStruct((t, d), BF16)],
        scratch_shapes=[pltpu.VMEM((tm, d), BF16)],
        compiler_params=_params("parallel"),
        name="merge",
    )(x, xb, z, att, mkv, w_in, w_in, w_in, w_in, b_gate,
      w_conv_out, w_diff_out, w_mem_out, w_o, ln_g, ln_b)


def _mlp_kernel(x_ref, xb_ref, w1_ref, w2_ref, g_ref, b_ref, y_ref, yb_ref, *, alpha, fc, nsplit):
    rows = x_ref.shape[0] // nsplit
    for r in range(nsplit):
        rs = slice(r * rows, (r + 1) * rows)
        xb = xb_ref[rs, :]
        f = jnp.zeros((rows, x_ref.shape[1]), F32)
        for c in range(w1_ref.shape[1] // fc):
            cs = slice(c * fc, (c + 1) * fc)
            hid = jnp.maximum(jnp.dot(xb, w1_ref[:, cs], preferred_element_type=F32), 0.0)
            f = f + jnp.dot((hid * hid).astype(BF16), w2_ref[cs, :], preferred_element_type=F32)
        y = _layer_norm(alpha * x_ref[rs, :] + f, g_ref[...], b_ref[...])
        y_ref[rs, :] = y
        yb_ref[rs, :] = y.astype(BF16)


def _mlp(x, xb, w1, w2, ln_g, ln_b, layer, alpha, tm=1024, fc=2048, nsplit=2):
    t, d = x.shape
    dff = w1.shape[2]
    row = lambda i: (i, 0)
    vec = lambda n: pl.BlockSpec((None, 1, n), lambda i: (layer, 0, 0))
    kern = functools.partial(_mlp_kernel, alpha=alpha, fc=fc, nsplit=nsplit)
    return pl.pallas_call(
        kern,
        grid=(t // tm,),
        in_specs=[
            pl.BlockSpec((tm, d), row),
            pl.BlockSpec((tm, d), row),
            pl.BlockSpec((None, d, dff), lambda i: (layer, 0, 0), pipeline_mode=pl.Buffered(1)),
            pl.BlockSpec((None, dff, d), lambda i: (layer, 0, 0), pipeline_mode=pl.Buffered(1)),
            vec(d), vec(d),
        ],
        out_specs=[pl.BlockSpec((tm, d), row), pl.BlockSpec((tm, d), row)],
        out_shape=[jax.ShapeDtypeStruct((t, d), F32), jax.ShapeDtypeStruct((t, d), BF16)],
        compiler_params=_params("parallel"),
        name="mlp",
    )(x, xb, w1, w2, ln_g, ln_b)


def kernel(x, mem, w_in, b_gate, conv_w, w_conv_out, diff_lambda, subln_g, w_diff_out, rel_bias,
           w_mem_kv, w_mem_out, w_o, ln1_g, ln1_b, w_mlp1, w_mlp2, ln2_g, ln2_b):
    batch, seq, d = x.shape
    mem_len = mem.shape[1]
    depth = w_in.shape[0]
    alpha = (2 * depth) ** 0.25

    w_in_b = w_in.astype(BF16)
    w_conv_out_b = w_conv_out.astype(BF16)
    w_diff_out_b = w_diff_out.astype(BF16)
    w_mem_kv_b = w_mem_kv.astype(BF16)
    w_mem_out_b = w_mem_out.astype(BF16)
    w_o_b = w_o.astype(BF16)
    w_mlp1_b = w_mlp1.astype(BF16)
    w_mlp2_b = w_mlp2.astype(BF16)
    mem_b = mem.reshape(batch * mem_len, d).astype(BF16)
    vec3 = lambda a: a.reshape(depth, 1, a.shape[-1])

    bias_kt = _rel_bias_kt(rel_bias, seq)

    mkv = _matmul_layers(mem_b, w_mem_kv_b, 1024, 1024, BF16)

    xf = x.reshape(batch * seq, d)
    xb = xf.astype(BF16)
    for l in range(depth):
        lam_init = 0.8 - 0.6 * math.exp(-0.3 * l)
        qt, k, vt = _qkv_projection(xb, w_in_b, l, DIFF_HEADS, DIFF_DV, batch, seq, DIFF_DK ** -0.5 * LOG2E)
        att = _diff_attention(qt, k, vt, bias_kt, diff_lambda,
                              vec3(subln_g), l, lam_init).reshape(DIFF_HEADS, batch * seq, DIFF_DV)
        z = _conv_branch(xb, w_in_b, conv_w, l, batch, seq)
        xf, xb = _merge(xf, xb, z, att, mkv, w_in_b, vec3(b_gate), w_conv_out_b, w_diff_out_b,
                        w_mem_out_b, w_o_b, vec3(ln1_g), vec3(ln1_b), l, seq, mem_len, alpha)
        xf, xb = _mlp(xf, xb, w_mlp1_b, w_mlp2_b, vec3(ln2_g), vec3(ln2_b), l, alpha)
    return xf.reshape(batch, seq, d)
```

```python
import functools
import math

import jax
import jax.numpy as jnp
from jax import lax
from jax.experimental import pallas as pl
from jax.experimental.pallas import tpu as pltpu

F32 = jnp.float32
BF16 = jnp.bfloat16

D_MODEL = 1024
CONV_K = 3
DIFF_HEADS = 8
DIFF_DK = 64
DIFF_DV = 2 * DIFF_DK
MEM_HEADS = 4
MEM_DH = 256
N_BRANCH = 3
REL_BUCKETS = 32
REL_MAX_DIST = 128
LN_EPS = 1e-5
LOG2E = math.log2(math.e)
SUM_ROWS = 16

COL_CH, COL_CB, COL_CC = 0, D_MODEL, 2 * D_MODEL
COL_Q, COL_K, COL_V = 3 * D_MODEL, 4 * D_MODEL, 5 * D_MODEL
COL_MQ = 6 * D_MODEL
COL_GATE = 7 * D_MODEL

VMEM_LIMIT = 56 * 1024 * 1024

NT_DIMS = (((1,), (1,)), ((), ()))
TN_DIMS = (((0,), (1,)), ((), ()))


def _params(*sem):
    return pltpu.CompilerParams(dimension_semantics=sem, vmem_limit_bytes=VMEM_LIMIT)


def _layer_norm(y, g, b):
    mu = jnp.mean(y, axis=-1, keepdims=True)
    yc = y - mu
    var = jnp.mean(yc * yc, axis=-1, keepdims=True)
    return yc * lax.rsqrt(var + LN_EPS) * g + b


def _mm_kernel(x_ref, w_ref, o_ref):
    o_ref[...] = jnp.dot(x_ref[...], w_ref[...], preferred_element_type=F32).astype(o_ref.dtype)


def _matmul_layers(x, w_stack, tm, tn, out_dtype):
    m, k = x.shape
    layers, _, n = w_stack.shape
    return pl.pallas_call(
        _mm_kernel,
        grid=(m // tm, layers, n // tn),
        in_specs=[
            pl.BlockSpec((tm, k), lambda i, l, j: (i, 0)),
            pl.BlockSpec((None, k, tn), lambda i, l, j: (l, 0, j)),
        ],
        out_specs=pl.BlockSpec((None, tm, tn), lambda i, l, j: (l, i, j)),
        out_shape=jax.ShapeDtypeStruct((layers, m, n), out_dtype),
        compiler_params=_params("parallel", "arbitrary", "arbitrary"),
        name="matmul_layers",
    )(x, w_stack)


def _qkv_kernel(x_ref, wq_ref, wk_ref, wv_ref, qt_ref, k_ref, vt_ref, *, q_scale):
    heads, dh, _ = qt_ref.shape
    x = x_ref[...]
    qt = (lax.dot_general(wq_ref[...], x, TN_DIMS, preferred_element_type=F32) * q_scale).astype(qt_ref.dtype)
    kk = jnp.dot(x, wk_ref[...], preferred_element_type=F32).astype(k_ref.dtype)
    vt = lax.dot_general(wv_ref[...], x, TN_DIMS, preferred_element_type=F32).astype(vt_ref.dtype)
    for h in range(heads):
        qt_ref[h] = qt[h * dh:(h + 1) * dh, :]
        k_ref[h] = kk[:, h * dh:(h + 1) * dh]
        vt_ref[h] = vt[h * dh:(h + 1) * dh, :]


def _qkv_projection(x, w_stack, layer, heads, dh, batch, seq, q_scale, tm=2048):
    _, k = x.shape
    n = heads * dh
    per_b = seq // tm
    wspec = lambda col0: pl.BlockSpec((None, k, n), lambda i: (layer, 0, col0 // n), pipeline_mode=pl.Buffered(1))
    fm = pl.BlockSpec((heads, None, dh, tm), lambda i: (0, i // per_b, 0, i % per_b))
    tmaj = pl.BlockSpec((heads, None, tm, dh), lambda i: (0, i // per_b, i % per_b, 0))
    return pl.pallas_call(
        functools.partial(_qkv_kernel, q_scale=q_scale),
        grid=(batch * per_b,),
        in_specs=[pl.BlockSpec((tm, k), lambda i: (i, 0)), wspec(COL_Q), wspec(COL_K), wspec(COL_V)],
        out_specs=[fm, tmaj, fm],
        out_shape=[jax.ShapeDtypeStruct((heads, batch, dh, seq), BF16),
                   jax.ShapeDtypeStruct((heads, batch, seq, dh), BF16),
                   jax.ShapeDtypeStruct((heads, batch, dh, seq), BF16)],
        compiler_params=_params("parallel"),
        name="qkv_projection",
    )(x, w_stack, w_stack, w_stack)


def _conv_kernel(x_ref, wh_ref, wb_ref, wc_ref, cw_ref, z_ref, u_ref, *, nsplit):
    s = x_ref.shape[0]
    rows = s // nsplit
    pad = jnp.zeros((8, u_ref.shape[1]), F32)
    u_ref[0:8, :] = pad
    u_ref[s + 8:s + 16, :] = pad
    for r in range(nsplit):
        x = x_ref[r * rows:(r + 1) * rows, :]
        u_ref[8 + r * rows:8 + (r + 1) * rows, :] = (jnp.dot(x, wc_ref[...], preferred_element_type=F32)
                                                     * jnp.dot(x, wh_ref[...], preferred_element_type=F32))
    cw = cw_ref[...]
    for r in range(nsplit):
        lo = r * rows
        cb = jnp.dot(x_ref[lo:lo + rows, :], wb_ref[...], preferred_element_type=F32)
        y = (cw[0:1, :] * u_ref[lo + 7:lo + rows + 7, :] + cw[1:2, :] * u_ref[lo + 8:lo + rows + 8, :]
             + cw[2:3, :] * u_ref[lo + 9:lo + rows + 9, :])
        z_ref[lo:lo + rows, :] = (cb * y).astype(z_ref.dtype)


def _conv_branch(xb, w_in, conv_w, layer, batch, seq, tc=1024, nsplit=2):
    t, d = xb.shape
    nct = D_MODEL // tc
    return pl.pallas_call(
        functools.partial(_conv_kernel, nsplit=nsplit),
        grid=(batch, nct),
        in_specs=[
            pl.BlockSpec((seq, d), lambda b, j: (b, 0)),
            pl.BlockSpec((None, d, tc), lambda b, j: (layer, 0, COL_CH // tc + j)),
            pl.BlockSpec((None, d, tc), lambda b, j: (layer, 0, COL_CB // tc + j)),
            pl.BlockSpec((None, d, tc), lambda b, j: (layer, 0, COL_CC // tc + j)),
            pl.BlockSpec((None, CONV_K, tc), lambda b, j: (layer, 0, j)),
        ],
        out_specs=pl.BlockSpec((seq, tc), lambda b, j: (b, j)),
        out_shape=jax.ShapeDtypeStruct((t, D_MODEL), BF16),
        scratch_shapes=[pltpu.VMEM((seq + 16, tc), F32)],
        compiler_params=_params("parallel", "arbitrary"),
        name="conv_branch",
    )(xb, w_in, w_in, w_in, conv_w)


def _bias_kernel(rel_ref, lo_ref, hi_ref, wbkt_ref, o_ref, win_ref, *, rows):
    h = pl.program_id(0)
    j = pl.program_id(1)
    s, tq = o_ref.shape

    @pl.when(j == 0)
    def _():
        bkt = wbkt_ref[...]
        acc = jnp.zeros(bkt.shape, F32)
        for b in range(REL_BUCKETS):
            acc = jnp.where(bkt == b, rel_ref[b, h], acc)
        win_ref[...] = acc * LOG2E

    def chunk(c, carry):
        r0 = pl.multiple_of(c * rows, rows)
        lo = lo_ref[c, j]

        @pl.when(lo == hi_ref[c, j])
        def _():
            o_ref[pl.ds(r0, rows), :] = jnp.full((rows, tq), rel_ref[lo, h], F32) * LOG2E

        @pl.when(lo != hi_ref[c, j])
        def _():
            w0 = jnp.clip(c * rows - j * tq + REL_MAX_DIST, 0, win_ref.shape[0] - rows)
            o_ref[pl.ds(r0, rows), :] = win_ref[pl.ds(pl.multiple_of(w0, rows), rows), :]

        return carry

    lax.fori_loop(0, s // rows, chunk, 0)


def _rel_bias_kt(rel_bias, seq, tq=256, rows=REL_MAX_DIST):
    wrows = tq + 2 * REL_MAX_DIST
    r = jnp.arange(wrows, dtype=jnp.int32)[:, None]
    c = jnp.arange(tq, dtype=jnp.int32)[None, :]
    win_bkt = _t5_bucket(r - REL_MAX_DIST - c)
    k0 = jnp.arange(seq // rows, dtype=jnp.int32)[:, None, None] * rows
    q0 = jnp.arange(seq // tq, dtype=jnp.int32)[None, :, None] * tq
    span = _t5_bucket(k0 - q0 - (tq - 1) + jnp.arange(rows + tq - 1, dtype=jnp.int32)[None, None, :])
    smem = pl.BlockSpec(memory_space=pltpu.SMEM)
    return pl.pallas_call(
        functools.partial(_bias_kernel, rows=rows),
        grid=(DIFF_HEADS, seq // tq),
        in_specs=[smem, smem, smem, pl.BlockSpec((wrows, tq), lambda h, j: (0, 0))],
        out_specs=pl.BlockSpec((None, seq, tq), lambda h, j: (h, 0, j)),
        out_shape=jax.ShapeDtypeStruct((DIFF_HEADS, seq, seq), F32),
        scratch_shapes=[pltpu.VMEM((wrows, tq), F32)],
        compiler_params=_params("parallel", "arbitrary"),
        name="rel_bias",
    )(rel_bias, span.min(axis=-1), span.max(axis=-1), win_bkt)


def _t5_bucket(rel):
    nb = REL_BUCKETS // 2
    max_exact = nb // 2
    ret = (rel > 0).astype(jnp.int32) * nb
    n = jnp.abs(rel)
    nf = jnp.maximum(n, 1).astype(F32)
    large = max_exact + (jnp.log(nf / max_exact) / math.log(REL_MAX_DIST / max_exact)
                         * (nb - max_exact)).astype(jnp.int32)
    large = jnp.minimum(large, nb - 1)
    return ret + jnp.where(n < max_exact, n, large)


def _diff_attn_kernel(lam_ref, qt_ref, k_ref, bias_ref, vt_ref, vtp_ref, g_ref, o_ref,
                      s0_ref, s1_ref, m0_ref, m1_ref, a0_ref, a1_ref, fin_ref, *, lam_init, kc, pw, nreal):
    group, _, tq = qt_ref.shape
    s = k_ref.shape[1]
    n = 2 * tq
    step = pl.program_id(0)

    @pl.when(step == 0)
    def _():
        s1_ref[...] = jnp.zeros(s1_ref.shape, F32)
        m1_ref[...] = jnp.zeros(m1_ref.shape, F32)
        a0_ref[...] = jnp.ones(a0_ref.shape, F32)
        fin_ref[...] = jnp.zeros(fin_ref.shape, BF16)

    for j in range(group - 2):
        o_ref[j] = fin_ref[j]

    def sub_tile(j):
        sw_ref, mw_ref, sr_ref, mr_ref = (s0_ref, m0_ref, s1_ref, m1_ref) if j % 2 == 0 else (s1_ref, m1_ref, s0_ref, m0_ref)
        aw_ref, ar_ref = (a1_ref, a0_ref) if j % 2 == 0 else (a0_ref, a1_ref)

        qt = qt_ref[j]
        row = lax.broadcasted_iota(jnp.int32, qt.shape, 0)
        zero = jnp.zeros_like(qt)
        q12 = jnp.concatenate([jnp.where(row < DIFF_DK, qt, zero), jnp.where(row >= DIFF_DK, qt, zero)], axis=1)
        m = jnp.max(mr_ref[...], axis=0, keepdims=True)
        orow = lax.broadcasted_iota(jnp.int32, (SUM_ROWS, kc), 0)
        ones = jnp.where(orow == 0, 1.0, 0.0).astype(BF16)
        vt_prev = vtp_ref if j == 0 else vt_ref.at[j - 1]
        nparts = n // pw
        pvs = [jnp.zeros((DIFF_DV + SUM_ROWS, pw), F32) for _ in range(nparts)]
        mxs = [jnp.full((8, pw), -jnp.inf, F32) for _ in range(nparts)]
        for c in range(s // kc):
            sl = slice(c * kc, (c + 1) * kc)
            vt = jnp.concatenate([vt_prev[:, sl], ones], axis=0)
            kb = k_ref[j, sl, :]
            for p in range(nparts):
                cs = slice(p * pw, (p + 1) * pw)
                qs = slice(p * pw % tq, p * pw % tq + pw)
                e = jnp.exp2(sr_ref[sl, cs] - m[:, cs]).astype(BF16)
                pvs[p] = pvs[p] + jnp.dot(vt, e, preferred_element_type=F32)
                sc = jnp.dot(kb, q12[:, cs], preferred_element_type=F32) + bias_ref[sl, qs]
                sw_ref[sl, cs] = sc
                mxs[p] = jnp.maximum(mxs[p], jnp.max(sc.reshape(kc // 8, 8, pw), axis=0))
        mw_ref[...] = jnp.concatenate(mxs, axis=1)
        aw_ref[...] = jnp.concatenate(pvs, axis=1)

        acc = ar_ref[...]
        lf = lam_ref[...]
        lam = (jnp.exp(jnp.sum(lf[0:1, :] * lf[1:2, :], axis=-1, keepdims=True))
               - jnp.exp(jnp.sum(lf[2:3, :] * lf[3:4, :], axis=-1, keepdims=True)) + lam_init)
        r = 1.0 / acc[DIFF_DV:DIFF_DV + 1, :]
        ot = acc[:DIFF_DV, :tq] * r[:, :tq] - acc[:DIFF_DV, tq:] * (lam * r[:, tq:])
        o = ot.T
        ms = jnp.mean(o * o, axis=-1, keepdims=True)
        res = (o * lax.rsqrt(ms + LN_EPS) * g_ref[...] * (1.0 - lam_init)).astype(o_ref.dtype)
        if j < 2:
            o_ref[group - 2 + j] = res
        else:
            fin_ref[j - 2] = res

    def pair(j0):
        sub_tile(j0)
        sub_tile(j0 + 1)

    assert group == 4
    pl.when(step >= -group)(functools.partial(pair, 0))
    pl.when(step < nreal)(functools.partial(pair, 2))


def _diff_attention(qt, k, vt, bias_kt, diff_lambda, subln_g, layer, lam_init, tq=512, kc=512, pw=256, group=4):
    heads, batch, seq, _ = k.shape
    nqt = seq // tq
    nbg = batch // group
    nreal = heads * nqt * nbg

    def dec(gs):
        return gs // (nqt * nbg), (gs // nbg) % nqt, gs % nbg

    cur = lambda gs: dec(jnp.minimum(gs, nreal - 1))
    out = lambda gs: dec(jnp.maximum(gs - 1, 0))

    def qt_map(gs):
        h, i, bg = cur(gs)
        return (h, bg, 0, i)

    def kv_map(gs):
        h, i, bg = cur(gs)
        return (h, bg, 0, 0)

    def bias_map(gs):
        h, i, bg = cur(gs)
        return (h, 0, i)

    def vtp_map(gs):
        h, i, bg = dec(jnp.maximum(gs - 1, 0))
        return (h, bg * group + group - 1, 0, 0)

    def o_map(gs):
        h, i, bg = out(gs)
        return (h, bg, i, 0)

    kern = functools.partial(_diff_attn_kernel, lam_init=lam_init, kc=kc, pw=pw, nreal=nreal)
    return pl.pallas_call(
        kern,
        grid=(nreal + 1,),
        in_specs=[
            pl.BlockSpec((None, 4, DIFF_DK), lambda gs: (layer, 0, 0)),
            pl.BlockSpec((None, group, DIFF_DV, tq), qt_map),
            pl.BlockSpec((None, group, seq, DIFF_DV), kv_map),
            pl.BlockSpec((None, seq, tq), bias_map),
            pl.BlockSpec((None, group, DIFF_DV, seq), kv_map),
            pl.BlockSpec((None, None, DIFF_DV, seq), vtp_map),
            pl.BlockSpec((None, 1, DIFF_DV), lambda gs: (layer, 0, 0)),
        ],
        out_specs=pl.BlockSpec((None, group, tq, DIFF_DV), o_map),
        out_shape=jax.ShapeDtypeStruct((heads, batch, seq, DIFF_DV), BF16),
        scratch_shapes=[
            pltpu.VMEM((seq, 2 * tq), F32), pltpu.VMEM((seq, 2 * tq), F32),
            pltpu.VMEM((8, 2 * tq), F32), pltpu.VMEM((8, 2 * tq), F32),
            pltpu.VMEM((DIFF_DV + SUM_ROWS, 2 * tq), F32), pltpu.VMEM((DIFF_DV + SUM_ROWS, 2 * tq), F32),
            pltpu.VMEM((group - 2, tq, DIFF_DV), BF16),
        ],
        compiler_params=_params("arbitrary"),
        name="diff_attn",
    )(diff_lambda, qt, k, bias_kt, vt, vt, subln_g)


def _merge_kernel(x_ref, xb_ref, z_ref, att_ref, mkv_ref,
                  wmq_ref, wga_ref, wgb_ref, wgc_ref, bg_ref,
                  wca_ref, wdo_ref, wmo_ref, wo_ref, g_ref, b_ref,
                  y_ref, yb_ref, o_scr, *, alpha, nsplit):
    xb = xb_ref[...]
    d = D_MODEL
    w = MEM_HEADS * MEM_DH

    def gate(wg_ref, j):
        gl = jnp.dot(xb, wg_ref[...], preferred_element_type=F32) + bg_ref[:, j * d:(j + 1) * d]
        return 1.0 / (1.0 + jnp.exp(-gl))

    mq = (jnp.dot(xb, wmq_ref[...], preferred_element_type=F32) * (MEM_DH ** -0.5)).astype(BF16)
    scs = [lax.dot_general(mq[:, h * MEM_DH:(h + 1) * MEM_DH], mkv_ref[:, h * MEM_DH:(h + 1) * MEM_DH], NT_DIMS,
                           preferred_element_type=F32) for h in range(MEM_HEADS)]
    cw = 256
    ncb = d // cw

    def gate_blk(wg_ref, j, cs):
        gl = jnp.dot(xb, wg_ref[:, cs], preferred_element_type=F32) + bg_ref[:, j * d + cs.start:j * d + cs.stop]
        return 1.0 / (1.0 + jnp.exp(-gl))

    z = z_ref[...]
    parts = []
    for p in range(ncb):
        cs = slice(p * cw, (p + 1) * cw)
        parts.append(gate_blk(wga_ref, 0, cs) * jnp.dot(z, wca_ref[:, cs], preferred_element_type=F32))
    ps = []
    for sc in scs:
        e = jnp.exp(sc - jnp.max(sc, axis=-1, keepdims=True))
        ps.append((e * (1.0 / jnp.sum(e, axis=-1, keepdims=True))).astype(BF16))
    att = jnp.concatenate([att_ref[h] for h in range(att_ref.shape[0])], axis=1)
    for p in range(ncb):
        cs = slice(p * cw, (p + 1) * cw)
        parts[p] = parts[p] + gate_blk(wgb_ref, 1, cs) * jnp.dot(att, wdo_ref[:, cs], preferred_element_type=F32)
    for h in range(MEM_HEADS):
        o_scr[:, h * MEM_DH:(h + 1) * MEM_DH] = jnp.dot(ps[h], mkv_ref[:, w + h * MEM_DH:w + (h + 1) * MEM_DH],
                                                         preferred_element_type=F32).astype(BF16)
    o = o_scr[...]
    for p in range(ncb):
        cs = slice(p * cw, (p + 1) * cw)
        parts[p] = parts[p] + gate_blk(wgc_ref, 2, cs) * jnp.dot(o, wmo_ref[:, cs], preferred_element_type=F32)
    merged = jnp.concatenate(parts, axis=1).astype(BF16)
    rows = merged.shape[0] // nsplit
    for r in range(nsplit):
        rs = slice(r * rows, (r + 1) * rows)
        hmix = jnp.dot(merged[rs], wo_ref[...], preferred_element_type=F32)
        y = _layer_norm(alpha * x_ref[rs, :] + hmix, g_ref[...], b_ref[...])
        y_ref[rs, :] = y
        yb_ref[rs, :] = y.astype(BF16)


def _merge(x, xb, z, att, mkv, w_in, b_gate, w_conv_out, w_diff_out, w_mem_out, w_o, ln_g, ln_b,
           layer, seq, mem_len, alpha, tm=512, nsplit=2):
    t, d = x.shape
    per_b = seq // tm
    row = lambda i: (i, 0)
    wspec = lambda: pl.BlockSpec((None, d, d), lambda i: (layer, 0, 0), pipeline_mode=pl.Buffered(1))
    win = lambda col: pl.BlockSpec((None, d, d), lambda i: (layer, 0, col // d), pipeline_mode=pl.Buffered(1))
    vec = lambda n: pl.BlockSpec((None, 1, n), lambda i: (layer, 0, 0))
    kern = functools.partial(_merge_kernel, alpha=alpha, nsplit=nsplit)
    return pl.pallas_call(
        kern,
        grid=(t // tm,),
        in_specs=[
            pl.BlockSpec((tm, d), row),
            pl.BlockSpec((tm, d), row),
            pl.BlockSpec((tm, d), row),
            pl.BlockSpec((DIFF_HEADS, tm, DIFF_DV), lambda i: (0, i, 0)),
            pl.BlockSpec((None, mem_len, 2 * d), lambda i: (layer, i // per_b, 0)),
            win(COL_MQ), win(COL_GATE), win(COL_GATE + d), win(COL_GATE + 2 * d),
            vec(N_BRANCH * d),
            wspec(), wspec(), wspec(), wspec(),
            vec(d), vec(d),
        ],
        out_specs=[pl.BlockSpec((tm, d), row), pl.BlockSpec((tm, d), row)],
        out_shape=[jax.ShapeDtypeStruct((t, d), F32), jax.ShapeDtypeStruct((t, d), BF16)],
        scratch_shapes=[pltpu.VMEM((tm, d), BF16)],
        compiler_params=_params("parallel"),
        name="merge",
    )(x, xb, z, att, mkv, w_in, w_in, w_in, w_in, b_gate,
      w_conv_out, w_diff_out, w_mem_out, w_o, ln_g, ln_b)


def _mlp_kernel(x_ref, xb_ref, w1_ref, w2_ref, g_ref, b_ref, y_ref, yb_ref, *, alpha, fc, nsplit):
    rows = x_ref.shape[0] // nsplit
    for r in range(nsplit):
        rs = slice(r * rows, (r + 1) * rows)
        xb = xb_ref[rs, :]
        f = jnp.zeros((rows, x_ref.shape[1]), F32)
        for c in range(w1_ref.shape[1] // fc):
            cs = slice(c * fc, (c + 1) * fc)
            hid = jnp.maximum(jnp.dot(xb, w1_ref[:, cs], preferred_element_type=F32), 0.0)
            f = f + jnp.dot((hid * hid).astype(BF16), w2_ref[cs, :], preferred_element_type=F32)
        y = _layer_norm(alpha * x_ref[rs, :] + f, g_ref[...], b_ref[...])
        y_ref[rs, :] = y
        yb_ref[rs, :] = y.astype(BF16)


def _mlp(x, xb, w1, w2, ln_g, ln_b, layer, alpha, tm=1024, fc=2048, nsplit=2):
    t, d = x.shape
    dff = w1.shape[2]
    row = lambda i: (i, 0)
    vec = lambda n: pl.BlockSpec((None, 1, n), lambda i: (layer, 0, 0))
    kern = functools.partial(_mlp_kernel, alpha=alpha, fc=fc, nsplit=nsplit)
    return pl.pallas_call(
        kern,
        grid=(t // tm,),
        in_specs=[
            pl.BlockSpec((tm, d), row),
            pl.BlockSpec((tm, d), row),
            pl.BlockSpec((None, d, dff), lambda i: (layer, 0, 0), pipeline_mode=pl.Buffered(1)),
            pl.BlockSpec((None, dff, d), lambda i: (layer, 0, 0), pipeline_mode=pl.Buffered(1)),
            vec(d), vec(d),
        ],
        out_specs=[pl.BlockSpec((tm, d), row), pl.BlockSpec((tm, d), row)],
        out_shape=[jax.ShapeDtypeStruct((t, d), F32), jax.ShapeDtypeStruct((t, d), BF16)],
        compiler_params=_params("parallel"),
        name="mlp",
    )(x, xb, w1, w2, ln_g, ln_b)


def kernel(x, mem, w_in, b_gate, conv_w, w_conv_out, diff_lambda, subln_g, w_diff_out, rel_bias,
           w_mem_kv, w_mem_out, w_o, ln1_g, ln1_b, w_mlp1, w_mlp2, ln2_g, ln2_b):
    batch, seq, d = x.shape
    mem_len = mem.shape[1]
    depth = w_in.shape[0]
    alpha = (2 * depth) ** 0.25

    w_in_b = w_in.astype(BF16)
    w_conv_out_b = w_conv_out.astype(BF16)
    w_diff_out_b = w_diff_out.astype(BF16)
    w_mem_kv_b = w_mem_kv.astype(BF16)
    w_mem_out_b = w_mem_out.astype(BF16)
    w_o_b = w_o.astype(BF16)
    w_mlp1_b = w_mlp1.astype(BF16)
    w_mlp2_b = w_mlp2.astype(BF16)
    mem_b = mem.reshape(batch * mem_len, d).astype(BF16)
    vec3 = lambda a: a.reshape(depth, 1, a.shape[-1])

    bias_kt = _rel_bias_kt(rel_bias, seq)

    mkv = _matmul_layers(mem_b, w_mem_kv_b, 1024, 1024, BF16)

    xf = x.reshape(batch * seq, d)
    xb = xf.astype(BF16)
    for l in range(depth):
        lam_init = 0.8 - 0.6 * math.exp(-0.3 * l)
        qt, k, vt = _qkv_projection(xb, w_in_b, l, DIFF_HEADS, DIFF_DV, batch, seq, DIFF_DK ** -0.5 * LOG2E)
        att = _diff_attention(qt, k, vt, bias_kt, diff_lambda,
                              vec3(subln_g), l, lam_init).reshape(DIFF_HEADS, batch * seq, DIFF_DV)
        z = _conv_branch(xb, w_in_b, conv_w, l, batch, seq)
        xf, xb = _merge(xf, xb, z, att, mkv, w_in_b, vec3(b_gate), w_conv_out_b, w_diff_out_b,
                        w_mem_out_b, w_o_b, vec3(ln1_g), vec3(ln1_b), l, seq, mem_len, alpha)
        xf, xb = _mlp(xf, xb, w_mlp1_b, w_mlp2_b, vec3(ln2_g), vec3(ln2_b), l, alpha)
    return xf.reshape(batch, seq, d)
```
